```python
import jax, jax.numpy as jnp
from jax import lax
import numpy as np

D_MODEL = 1024
BATCH = 2
SEQ = 16384
DEPTH = 4

N_MIXERS = 2
RMS_EPS = 1e-6
LN_EPS = 1e-5

GMLP_EXPAND = 2
GMLP_WIDTH = GMLP_EXPAND * D_MODEL
GMLP_CHUNK = 128
GMLP_GROUPS = 8
GMLP_GROUP_WIDTH = GMLP_WIDTH // GMLP_GROUPS

HEAD_DIM = 64
HEADS_PER_GROUP = D_MODEL // HEAD_DIM
DILATED_GROUPS = ((128, 1), (512, 4), (2048, 16))
N_DIL_GROUPS = len(DILATED_GROUPS)
ATTN_WIDTH = HEADS_PER_GROUP * HEAD_DIM
ATTN_BLOCK = 128
ROPE_THETA = 10000.0

N_A_LAYERS = (DEPTH + 1) // 2
N_B_LAYERS = DEPTH // 2

kernel_name = "hybrid_gmlp_dilated_attn_trunk"


def rms_norm(x, gain):
    xf = x.astype(jnp.float32)
    y = xf * lax.rsqrt(jnp.mean(xf * xf, axis=-1, keepdims=True) + RMS_EPS)
    return (y * gain.astype(jnp.float32)).astype(x.dtype)


def layer_norm(x, gain, bias):
    xf = x.astype(jnp.float32)
    mu = jnp.mean(xf, axis=-1, keepdims=True)
    var = jnp.mean(jnp.square(xf - mu), axis=-1, keepdims=True)
    y = (xf - mu) * lax.rsqrt(var + LN_EPS)
    return (y * gain.astype(jnp.float32) + bias.astype(jnp.float32)).astype(x.dtype)


def rope_tables(seq_len):
    inv_freq = 1.0 / (ROPE_THETA ** (jnp.arange(0, HEAD_DIM, 2, dtype=jnp.float32) / HEAD_DIM))
    ang = jnp.arange(seq_len, dtype=jnp.float32)[:, None] * inv_freq[None, :]
    return jnp.cos(ang), jnp.sin(ang)


def apply_rope(t, cos, sin):
    tf = t.astype(jnp.float32)
    t1, t2 = tf[..., : HEAD_DIM // 2], tf[..., HEAD_DIM // 2:]
    c = cos[None, :, None, :]
    s = sin[None, :, None, :]
    return jnp.concatenate([t1 * c - t2 * s, t2 * c + t1 * s], axis=-1).astype(t.dtype)


def gmlp_mixer(h, w_in, ln_g, ln_b, w_s, b_s, w_out):
    B, S, _ = h.shape
    proj = h @ w_in
    uv = jax.nn.gelu(proj[..., : 2 * GMLP_WIDTH], approximate=False)
    z = proj[..., 2 * GMLP_WIDTH:]
    u, v = uv[..., :GMLP_WIDTH], uv[..., GMLP_WIDTH:]
    v = layer_norm(v, ln_g, ln_b)
    n_chunks = S // GMLP_CHUNK
    vc = v.reshape(B, n_chunks, GMLP_CHUNK, GMLP_GROUPS, GMLP_GROUP_WIDTH)
    causal = jnp.tril(jnp.ones((GMLP_CHUNK, GMLP_CHUNK), dtype=bool))
    w_masked = jnp.where(causal[None], w_s, jnp.zeros_like(w_s))
    sv = jnp.einsum('gts,bnsgc->bntgc', w_masked, vc) + jnp.transpose(b_s)[None, None, :, :, None]
    sv = sv.reshape(B, S, GMLP_WIDTH)
    y = u * sv * jax.nn.silu(z)
    return y @ w_out


def dilated_window_attention(q, k, v, window, dilation):
    B, S, H, Dh = q.shape
    steps = window // dilation
    L = S // dilation
    Lp = ((L + ATTN_BLOCK - 1) // ATTN_BLOCK) * ATTN_BLOCK
    N = Lp // ATTN_BLOCK

    def to_blocks(t):
        t = t.reshape(B, L, dilation, H, Dh).transpose(0, 2, 1, 3, 4)
        t = jnp.pad(t, ((0, 0), (0, 0), (0, Lp - L), (0, 0), (0, 0)))
        return t.reshape(B, dilation, N, ATTN_BLOCK, H, Dh)

    qb, kb, vb = to_blocks(q), to_blocks(k), to_blocks(v)
    pad_prev = ((0, 0), (0, 0), (1, 0), (0, 0), (0, 0), (0, 0))
    kk = jnp.concatenate([jnp.pad(kb, pad_prev)[:, :, :-1], kb], axis=3)
    vv = jnp.concatenate([jnp.pad(vb, pad_prev)[:, :, :-1], vb], axis=3)

    scale = 1.0 / np.sqrt(HEAD_DIM)
    scores = jnp.einsum('brnihk,brnjhk->brnhij', qb.astype(jnp.float32), kk.astype(jnp.float32)) * scale
    i_idx = jnp.arange(ATTN_BLOCK)[:, None]
    j_idx = jnp.arange(2 * ATTN_BLOCK)[None, :]
    n_idx = jnp.arange(N)[:, None, None]
    rel = ATTN_BLOCK + i_idx - j_idx
    valid = (rel >= 0) & (rel <= steps)
    valid = valid[None] & ((n_idx - 1) * ATTN_BLOCK + j_idx[None] >= 0)
    scores = jnp.where(valid[None, None, :, None], scores, -jnp.inf)
    lse = jax.nn.logsumexp(scores, axis=-1)
    p = jnp.exp(scores - lse[..., None])
    o = jnp.einsum('brnhij,brnjhk->brnihk', p.astype(v.dtype), vv)

    o = o.reshape(B, dilation, Lp, H, Dh)[:, :, :L]
    o = o.transpose(0, 2, 1, 3, 4).reshape(B, S, H, Dh)
    lse = lse.transpose(0, 1, 2, 4, 3).reshape(B, dilation, Lp, H)[:, :, :L]
    lse = lse.transpose(0, 2, 1, 3).reshape(B, S, H)
    return o, lse


def dilated_attention_mixer(h, w_in, w_out, cos, sin):
    B, S, _ = h.shape
    proj = h @ w_in
    qkv_width = N_DIL_GROUPS * 3 * ATTN_WIDTH
    qkv = proj[..., :qkv_width].reshape(B, S, N_DIL_GROUPS, 3, HEADS_PER_GROUP, HEAD_DIM)
    z = proj[..., qkv_width:]
    outs, lses = [], []
    for g, (window, dilation) in enumerate(DILATED_GROUPS):
        q = apply_rope(qkv[:, :, g, 0], cos, sin)
        k = apply_rope(qkv[:, :, g, 1], cos, sin)
        o, lse = dilated_window_attention(q, k, qkv[:, :, g, 2], window, dilation)
        outs.append(o)
        lses.append(lse)
    alpha = jax.nn.softmax(jnp.stack(lses, axis=0), axis=0)
    o = jnp.einsum('gbsh,gbshk->bshk', alpha, jnp.stack(outs, axis=0).astype(jnp.float32))
    y = o.astype(h.dtype).reshape(B, S, ATTN_WIDTH) * jax.nn.silu(z)
    return y @ w_out


def setup_inputs(seed: int = 0) -> dict:
    key = jax.random.key(seed)
    ks = jax.random.split(key, 12)
    D = D_MODEL
    x = jax.random.normal(ks[0], (BATCH, SEQ, D), jnp.float32)
    norm_pre = 1.0 + 0.05 * jax.random.normal(ks[1], (DEPTH, D), jnp.float32)
    norm_post = 1.0 + 0.05 * jax.random.normal(ks[2], (DEPTH, D), jnp.float32)
    a_w_in = jax.random.normal(ks[3], (N_A_LAYERS, D, 3 * GMLP_WIDTH), jnp.float32) * D ** -0.5
    a_ln_g = 1.0 + 0.05 * jax.random.normal(ks[4], (N_A_LAYERS, GMLP_WIDTH), jnp.float32)
    a_ln_b = 0.02 * jax.random.normal(ks[5], (N_A_LAYERS, GMLP_WIDTH), jnp.float32)
    a_w_s = jax.random.normal(ks[6], (N_A_LAYERS, GMLP_GROUPS, GMLP_CHUNK, GMLP_CHUNK), jnp.float32) * GMLP_CHUNK ** -0.5
    a_b_s = 1.0 + 0.1 * jax.random.normal(ks[7], (N_A_LAYERS, GMLP_GROUPS, GMLP_CHUNK), jnp.float32)
    a_w_out = jax.random.normal(ks[8], (N_A_LAYERS, GMLP_WIDTH, D), jnp.float32) * GMLP_WIDTH ** -0.5
    b_w_in = jax.random.normal(ks[9], (N_B_LAYERS, D, N_DIL_GROUPS * 3 * ATTN_WIDTH + ATTN_WIDTH), jnp.float32) * D ** -0.5
    b_w_out = jax.random.normal(ks[10], (N_B_LAYERS, ATTN_WIDTH, D), jnp.float32) * ATTN_WIDTH ** -0.5
    return {"x": x, "norm_pre": norm_pre, "norm_post": norm_post,
            "a_w_in": a_w_in, "a_ln_g": a_ln_g, "a_ln_b": a_ln_b, "a_w_s": a_w_s,
            "a_b_s": a_b_s, "a_w_out": a_w_out, "b_w_in": b_w_in, "b_w_out": b_w_out}


def reference(x, norm_pre, norm_post, a_w_in, a_ln_g, a_ln_b, a_w_s, a_b_s, a_w_out, b_w_in, b_w_out):
    cos, sin = rope_tables(x.shape[1])
    for i in range(DEPTH):
        h = rms_norm(x, norm_pre[i])
        j = i // N_MIXERS
        if i % N_MIXERS == 0:
            y = gmlp_mixer(h, a_w_in[j], a_ln_g[j], a_ln_b[j], a_w_s[j], a_b_s[j], a_w_out[j])
        else:
            y = dilated_attention_mixer(h, b_w_in[j], b_w_out[j], cos, sin)
        x = x + rms_norm(y, norm_post[i])
    return x
```

```python
import numpy as np
import jax
import jax.numpy as jnp
from jax import lax
from jax.experimental import pallas as pl
from jax.experimental.pallas import tpu as pltpu

F32 = jnp.float32
BF16 = jnp.bfloat16

D_MODEL = 1024
RMS_EPS = 1e-6
LN_EPS = 1e-5

GMLP_WIDTH = 2048
GMLP_CHUNK = 128
GMLP_GROUPS = 8
GMLP_GROUP_WIDTH = GMLP_WIDTH // GMLP_GROUPS

HEAD_DIM = 64
N_HEADS = 16
ATTN_WIDTH = N_HEADS * HEAD_DIM
DILATED_GROUPS = ((128, 1), (512, 4), (2048, 16))
N_DIL_GROUPS = len(DILATED_GROUPS)
ATTN_BLOCK = 128
ROPE_THETA = 10000.0
N_PROJ_SLABS = 3 * N_DIL_GROUPS + 1
Z_SLAB = 3 * N_DIL_GROUPS

LANES = 128
NEG_BIG = -1e30

GMLP_TOKENS = 512
PROJ_TOKENS = 1024
COMBINE_TOKENS = 512
VMEM_LIMIT = 56 * 1024 * 1024


def _gelu(x):
    return 0.5 * x * (1.0 + lax.erf(x * np.float32(np.sqrt(0.5))))


def _silu(x):
    return x * jax.nn.sigmoid(x)


def _rms_normalize(x, gain):
    ms = jnp.mean(x * x, axis=-1, keepdims=True)
    return x * lax.rsqrt(ms + RMS_EPS) * gain


def _gmlp_kernel(x_ref, gpre_ref, gpost_ref, win_ref, lng_ref, lnb_ref, ws_ref,
                 bs_ref, wout_ref, o_ref, h_sc, v_sc, y_sc):
    tm = x_ref.shape[0]
    gw = GMLP_GROUP_WIDTH
    x = x_ref[...]
    h_sc[...] = _rms_normalize(x, gpre_ref[...]).astype(BF16)

    s1 = jnp.zeros((tm, 1), F32)
    for g in range(GMLP_GROUPS):
        c0 = GMLP_WIDTH + g * gw
        v = _gelu(jnp.dot(h_sc[...], win_ref[:, c0:c0 + gw], preferred_element_type=F32))
        v_sc[:, g * gw:(g + 1) * gw] = v
        s1 = s1 + jnp.sum(v, axis=-1, keepdims=True)
    mu = s1 * (1.0 / GMLP_WIDTH)
    s2 = jnp.zeros((tm, 1), F32)
    for g in range(GMLP_GROUPS):
        dv = v_sc[:, g * gw:(g + 1) * gw] - mu
        s2 = s2 + jnp.sum(dv * dv, axis=-1, keepdims=True)
    rstd = lax.rsqrt(s2 * (1.0 / GMLP_WIDTH) + LN_EPS)

    row = lax.broadcasted_iota(jnp.int32, (GMLP_CHUNK, GMLP_CHUNK), 0)
    col = lax.broadcasted_iota(jnp.int32, (GMLP_CHUNK, GMLP_CHUNK), 1)
    causal = col <= row
    for g in range(GMLP_GROUPS):
        sl = slice(g * gw, (g + 1) * gw)
        vn = ((v_sc[:, sl] - mu) * rstd * lng_ref[:, sl] + lnb_ref[:, sl]).astype(BF16)
        wm = jnp.where(causal, ws_ref[g], 0.0).astype(BF16)
        bias = bs_ref[:, g:g + 1]
        sv = jnp.concatenate(
            [jnp.dot(wm, vn[c * GMLP_CHUNK:(c + 1) * GMLP_CHUNK], preferred_element_type=F32) + bias
             for c in range(tm // GMLP_CHUNK)], axis=0)
        u = _gelu(jnp.dot(h_sc[...], win_ref[:, sl], preferred_element_type=F32))
        z0 = 2 * GMLP_WIDTH + g * gw
        z = jnp.dot(h_sc[...], win_ref[:, z0:z0 + gw], preferred_element_type=F32)
        y_sc[:, sl] = (u * sv * _silu(z)).astype(BF16)

    out = jnp.dot(y_sc[...], wout_ref[...], preferred_element_type=F32)
    o_ref[...] = x_ref[...] + _rms_normalize(out, gpost_ref[...])


def _gmlp_layer(x2, gpre, gpost, w_in, ln_g, ln_b, w_s, b_s, w_out):
    n_tok = x2.shape[0]
    tm = GMLP_TOKENS
    const = lambda i: (0, 0)
    return pl.pallas_call(
        _gmlp_kernel,
        grid=(n_tok // tm,),
        in_specs=[
            pl.BlockSpec((tm, D_MODEL), lambda i: (i, 0)),
            pl.BlockSpec((1, D_MODEL), const),
            pl.BlockSpec((1, D_MODEL), const),
            pl.BlockSpec((D_MODEL, 3 * GMLP_WIDTH), const, pipeline_mode=pl.Buffered(1)),
            pl.BlockSpec((1, GMLP_WIDTH), const),
            pl.BlockSpec((1, GMLP_WIDTH), const),
            pl.BlockSpec((GMLP_GROUPS, GMLP_CHUNK, GMLP_CHUNK), lambda i: (0, 0, 0)),
            pl.BlockSpec((GMLP_CHUNK, GMLP_GROUPS), const),
            pl.BlockSpec((GMLP_WIDTH, D_MODEL), const, pipeline_mode=pl.Buffered(1)),
        ],
        out_specs=pl.BlockSpec((tm, D_MODEL), lambda i: (i, 0)),
        out_shape=jax.ShapeDtypeStruct(x2.shape, F32),
        scratch_shapes=[
            pltpu.VMEM((tm, D_MODEL), BF16),
            pltpu.VMEM((tm, GMLP_WIDTH), F32),
            pltpu.VMEM((tm, GMLP_WIDTH), BF16),
        ],
        compiler_params=pltpu.CompilerParams(
            dimension_semantics=("arbitrary",), vmem_limit_bytes=VMEM_LIMIT),
        name="gmlp_layer",
    )(x2, gpre.reshape(1, -1), gpost.reshape(1, -1), w_in.astype(BF16),
      ln_g.reshape(1, -1), ln_b.reshape(1, -1), w_s, jnp.transpose(b_s), w_out.astype(BF16))


def _rope(p, cos, sin_signed):
    lane = lax.broadcasted_iota(jnp.int32, (1, LANES), 1)
    first_half = (lane % HEAD_DIM) < (HEAD_DIM // 2)
    outs = []
    for s in range(p.shape[1] // LANES):
        t = p[:, s * LANES:(s + 1) * LANES]
        partner = jnp.where(first_half,
                            pltpu.roll(t, LANES - HEAD_DIM // 2, 1),
                            pltpu.roll(t, HEAD_DIM // 2, 1))
        outs.append(t * cos + partner * sin_signed)
    return jnp.concatenate(outs, axis=1)


def _proj_kernel(x_ref, gpre_ref, w_ref, cos_ref, sin_ref, o_ref, h_sc):
    j = pl.program_id(1)

    @pl.when(j == 0)
    def _():
        h_sc[...] = _rms_normalize(x_ref[...], gpre_ref[...]).astype(BF16)

    p = jnp.dot(h_sc[...], w_ref[...], preferred_element_type=F32)
    kind = j % 3
    is_gate = j == Z_SLAB

    @pl.when(jnp.logical_and(kind == 0, jnp.logical_not(is_gate)))
    def _():
        scale = np.float32(1.0 / np.sqrt(HEAD_DIM))
        o_ref[...] = (_rope(p, cos_ref[...], sin_ref[...]) * scale).astype(BF16)

    @pl.when(kind == 1)
    def _():
        o_ref[...] = _rope(p, cos_ref[...], sin_ref[...]).astype(BF16)

    @pl.when(jnp.logical_or(kind == 2, is_gate))
    def _():
        o_ref[...] = p.astype(BF16)


def _proj_layer(x2, gpre, w_in, cos2, sin2, seq_len):
    n_tok = x2.shape[0]
    tm = PROJ_TOKENS
    tiles_per_seq = seq_len // tm
    return pl.pallas_call(
        _proj_kernel,
        grid=(n_tok // tm, N_PROJ_SLABS),
        in_specs=[
            pl.BlockSpec((tm, D_MODEL), lambda i, j: (i, 0)),
            pl.BlockSpec((1, D_MODEL), lambda i, j: (0, 0)),
            pl.BlockSpec((D_MODEL, ATTN_WIDTH), lambda i, j: (0, j)),
            pl.BlockSpec((tm, LANES), lambda i, j: (i % tiles_per_seq, 0)),
            pl.BlockSpec((tm, LANES), lambda i, j: (i % tiles_per_seq, 0)),
        ],
        out_specs=pl.BlockSpec((tm, ATTN_WIDTH), lambda i, j: (i, j)),
        out_shape=jax.ShapeDtypeStruct((n_tok, N_PROJ_SLABS * ATTN_WIDTH), BF16),
        scratch_shapes=[pltpu.VMEM((tm, D_MODEL), BF16)],
        compiler_params=pltpu.CompilerParams(
            dimension_semantics=("arbitrary", "arbitrary"), vmem_limit_bytes=VMEM_LIMIT),
        name="attn_proj",
    )(x2, gpre.reshape(1, -1), w_in.astype(BF16), cos2, sin2)


def _attn_kernel(q_ref, kp_ref, kc_ref, vp_ref, vc_ref, o_ref, st_ref):
    n = pl.program_id(2)
    t = ATTN_BLOCK
    qi = lax.broadcasted_iota(jnp.int32, (t, 2 * t), 0)
    kj = lax.broadcasted_iota(jnp.int32, (t, 2 * t), 1)
    valid = jnp.logical_or(
        jnp.logical_and(jnp.logical_and(kj < t, kj >= qi), n > 0),
        jnp.logical_and(kj >= t, (kj - t) <= qi))
    lane = lax.broadcasted_iota(jnp.int32, (1, LANES), 1)
    st = jnp.zeros((t, LANES), F32)
    for hp in range(N_HEADS // 2):
        sl = slice(hp * LANES, (hp + 1) * LANES)
        q2 = q_ref[0, :, sl]
        k2 = jnp.concatenate([kp_ref[0, :, sl], kc_ref[0, :, sl]], axis=0)
        v2 = jnp.concatenate([vp_ref[0, :, sl], vc_ref[0, :, sl]], axis=0)
        o2 = jnp.zeros((t, LANES), F32)
        for e in range(2):
            in_head = (lane // HEAD_DIM) == e
            qe = jnp.where(in_head, q2, jnp.zeros_like(q2))
            s = lax.dot_general(qe, k2, (((1,), (1,)), ((), ())), preferred_element_type=F32)
            s = jnp.where(valid, s, NEG_BIG)
            m = jnp.max(s, axis=-1, keepdims=True)
            p = jnp.exp(s - m)
            l = jnp.sum(p, axis=-1, keepdims=True)
            pv = jnp.dot(p.astype(BF16), v2, preferred_element_type=F32)
            o2 = jnp.where(in_head, pv / l, o2)
            st = jnp.where(lane == 2 * hp + e, m + jnp.log(l), st)
        o_ref[0, :, sl] = o2.astype(BF16)
    st_ref[0] = st


def _attn_group(proj3, g, dilation):
    batch, seq_len, _ = proj3.shape
    length = seq_len // dilation
    n_blocks = length // ATTN_BLOCK
    pv = proj3.reshape(batch, length, dilation * N_PROJ_SLABS * ATTN_WIDTH)
    t = ATTN_BLOCK

    def spec(c, prev):
        if prev:
            return pl.BlockSpec((1, t, ATTN_WIDTH),
                                lambda b, r, n: (b, jnp.maximum(n - 1, 0), r * N_PROJ_SLABS + 3 * g + c))
        return pl.BlockSpec((1, t, ATTN_WIDTH), lambda b, r, n: (b, n, r * N_PROJ_SLABS + 3 * g + c))

    o, st = pl.pallas_call(
        _attn_kernel,
        grid=(batch, dilation, n_blocks),
        in_specs=[spec(0, False), spec(1, True), spec(1, False), spec(2, True), spec(2, False)],
        out_specs=[
            pl.BlockSpec((1, t, ATTN_WIDTH), lambda b, r, n: (b, n, r)),
            pl.BlockSpec((1, t, LANES), lambda b, r, n: (b, n, r)),
        ],
        out_shape=[
            jax.ShapeDtypeStruct((batch, length, dilation * ATTN_WIDTH), BF16),
            jax.ShapeDtypeStruct((batch, length, dilation * LANES), F32),
        ],
        compiler_params=pltpu.CompilerParams(
            dimension_semantics=("arbitrary", "arbitrary", "arbitrary"), vmem_limit_bytes=VMEM_LIMIT),
        name=f"dilated_attn_g{g}",
    )(pv, pv, pv, pv, pv)
    return (o.reshape(batch * seq_len, ATTN_WIDTH), st.reshape(batch * seq_len, LANES))


def _combine_kernel(x_ref, o0_ref, o1_ref, o2_ref, s0_ref, s1_ref, s2_ref, z_ref,
                    expand_ref, wout_ref, gpost_ref, out_ref):
    lses = [s0_ref[...], s1_ref[...], s2_ref[...]]
    m = jnp.maximum(jnp.maximum(lses[0], lses[1]), lses[2])
    es = [jnp.exp(s - m) for s in lses]
    inv = 1.0 / (es[0] + es[1] + es[2])
    o = None
    for e, o_ref in zip(es, (o0_ref, o1_ref, o2_ref)):
        alpha = e * inv
        hi = alpha.astype(BF16)
        lo = (alpha - hi.astype(F32)).astype(BF16)
        a = (jnp.dot(hi, expand_ref[...], preferred_element_type=F32)
             + jnp.dot(lo, expand_ref[...], preferred_element_type=F32))
        term = a * o_ref[...].astype(F32)
        o = term if o is None else o + term
    y = (o * _silu(z_ref[...].astype(F32))).astype(BF16)
    out = jnp.dot(y, wout_ref[...], preferred_element_type=F32)
    out_ref[...] = x_ref[...] + _rms_normalize(out, gpost_ref[...])


def _combine_layer(x2, outs, stats, proj2, w_out, gpost):
    n_tok = x2.shape[0]
    tm = COMBINE_TOKENS
    head_of_lane = np.arange(ATTN_WIDTH) // HEAD_DIM
    expand = jnp.asarray(np.arange(LANES)[:, None] == head_of_lane[None, :], dtype=BF16)
    tok = lambda i: (i, 0)
    const = lambda i: (0, 0)
    return pl.pallas_call(
        _combine_kernel,
        grid=(n_tok // tm,),
        in_specs=[pl.BlockSpec((tm, D_MODEL), tok)]
        + [pl.BlockSpec((tm, ATTN_WIDTH), tok)] * 3
        + [pl.BlockSpec((tm, LANES), tok)] * 3
        + [pl.BlockSpec((tm, ATTN_WIDTH), lambda i: (i, Z_SLAB)),
           pl.BlockSpec((LANES, ATTN_WIDTH), const),
           pl.BlockSpec((ATTN_WIDTH, D_MODEL), const),
           pl.BlockSpec((1, D_MODEL), const)],
        out_specs=pl.BlockSpec((tm, D_MODEL), tok),
        out_shape=jax.ShapeDtypeStruct(x2.shape, F32),
        compiler_params=pltpu.CompilerParams(
            dimension_semantics=("arbitrary",), vmem_limit_bytes=VMEM_LIMIT),
        name="attn_combine",
    )(x2, *outs, *stats, proj2, expand, w_out.astype(BF16), gpost.reshape(1, -1))


def _rope_tables(seq_len):
    inv_freq = 1.0 / (ROPE_THETA ** (jnp.arange(0, HEAD_DIM, 2, dtype=F32) / HEAD_DIM))
    ang = jnp.arange(seq_len, dtype=F32)[:, None] * inv_freq[None, :]
    cos, sin = jnp.cos(ang), jnp.sin(ang)
    reps = LANES // HEAD_DIM
    cos2 = jnp.tile(jnp.concatenate([cos, cos], axis=1), (1, reps))
    sin2 = jnp.tile(jnp.concatenate([-sin, sin], axis=1), (1, reps))
    return cos2, sin2


def kernel(x, norm_pre, norm_post, a_w_in, a_ln_g, a_ln_b, a_w_s, a_b_s, a_w_out, b_w_in, b_w_out):
    batch, seq_len, d = x.shape
    depth = norm_pre.shape[0]
    cos2, sin2 = _rope_tables(seq_len)
    x2 = x.reshape(batch * seq_len, d)
    for i in range(depth):
        j = i // 2
        if i % 2 == 0:
            x2 = _gmlp_layer(x2, norm_pre[i], norm_post[i], a_w_in[j], a_ln_g[j], a_ln_b[j],
                             a_w_s[j], a_b_s[j], a_w_out[j])
        else:
            proj2 = _proj_layer(x2, norm_pre[i], b_w_in[j], cos2, sin2, seq_len)
            proj3 = proj2.reshape(batch, seq_len, -1)
            outs, stats = [], []
            for g, (_, dilation) in enumerate(DILATED_GROUPS):
                o, st = _attn_group(proj3, g, dilation)
                outs.append(o)
                stats.append(st)
            x2 = _combine_layer(x2, outs, stats, proj2, b_w_out[j], norm_post[i])
    return x2.reshape(batch, seq_len, d)
```

```python
import functools

import numpy as np
import jax
import jax.numpy as jnp
from jax import lax
from jax.experimental import pallas as pl
from jax.experimental.pallas import tpu as pltpu

F32 = jnp.float32
BF16 = jnp.bfloat16

D_MODEL = 1024
RMS_EPS = 1e-6
LN_EPS = 1e-5

GMLP_WIDTH = 2048
GMLP_CHUNK = 128
GMLP_GROUPS = 8
GMLP_GROUP_WIDTH = GMLP_WIDTH // GMLP_GROUPS

HEAD_DIM = 64
N_HEADS = 16
ATTN_WIDTH = N_HEADS * HEAD_DIM
DILATED_GROUPS = ((128, 1), (512, 4), (2048, 16))
N_DIL_GROUPS = len(DILATED_GROUPS)
ATTN_BLOCK = 128
ROPE_THETA = 10000.0
N_PROJ_SLABS = 3 * N_DIL_GROUPS + 1
Z_SLAB = 3 * N_DIL_GROUPS
GROUP_FIRST_SLAB = (0, 4, 7)
GATE_OUT_SLAB = 3
TILE = 2048

LANES = 128
NEG_BIG = -1e30

GMLP_TOKENS = 512
PROJ_ROW_CHUNK = 512
COMBINE_TOKENS = 512
VMEM_LIMIT = 56 * 1024 * 1024


def _gelu(x):
    return 0.5 * x * (1.0 + lax.erf(x * np.float32(np.sqrt(0.5))))


def _silu(x):
    return x * jax.nn.sigmoid(x)


def _rms_normalize(x, gain):
    ms = jnp.mean(x * x, axis=-1, keepdims=True)
    return x * lax.rsqrt(ms + RMS_EPS) * gain


def _gmlp_kernel(x_ref, gpre_ref, gpost_ref, win_ref, lng_ref, lnb_ref, ws_ref,
                 bs_ref, wout_ref, o_ref, h_sc, v_sc, y_sc):
    tm = x_ref.shape[0]
    gw = GMLP_GROUP_WIDTH
    x = x_ref[...]
    h_sc[...] = _rms_normalize(x, gpre_ref[...]).astype(BF16)

    s1 = jnp.zeros((tm, 1), F32)
    for g in range(GMLP_GROUPS):
        c0 = GMLP_WIDTH + g * gw
        v = _gelu(jnp.dot(h_sc[...], win_ref[:, c0:c0 + gw], preferred_element_type=F32))
        v_sc[:, g * gw:(g + 1) * gw] = v
        s1 = s1 + jnp.sum(v, axis=-1, keepdims=True)
    mu = s1 * (1.0 / GMLP_WIDTH)
    s2 = jnp.zeros((tm, 1), F32)
    for g in range(GMLP_GROUPS):
        dv = v_sc[:, g * gw:(g + 1) * gw] - mu
        s2 = s2 + jnp.sum(dv * dv, axis=-1, keepdims=True)
    rstd = lax.rsqrt(s2 * (1.0 / GMLP_WIDTH) + LN_EPS)

    row = lax.broadcasted_iota(jnp.int32, (GMLP_CHUNK, GMLP_CHUNK), 0)
    col = lax.broadcasted_iota(jnp.int32, (GMLP_CHUNK, GMLP_CHUNK), 1)
    causal = col <= row
    for g in range(GMLP_GROUPS):
        sl = slice(g * gw, (g + 1) * gw)
        vn = ((v_sc[:, sl] - mu) * rstd * lng_ref[:, sl] + lnb_ref[:, sl]).astype(BF16)
        wm = jnp.where(causal, ws_ref[g], 0.0).astype(BF16)
        bias = bs_ref[:, g:g + 1]
        sv = jnp.concatenate(
            [jnp.dot(wm, vn[c * GMLP_CHUNK:(c + 1) * GMLP_CHUNK], preferred_element_type=F32) + bias
             for c in range(tm // GMLP_CHUNK)], axis=0)
        u = _gelu(jnp.dot(h_sc[...], win_ref[:, sl], preferred_element_type=F32))
        z0 = 2 * GMLP_WIDTH + g * gw
        z = jnp.dot(h_sc[...], win_ref[:, z0:z0 + gw], preferred_element_type=F32)
        y_sc[:, sl] = (u * sv * _silu(z)).astype(BF16)

    out = jnp.dot(y_sc[...], wout_ref[...], preferred_element_type=F32)
    o_ref[...] = x_ref[...] + _rms_normalize(out, gpost_ref[...])


def _gmlp_layer(x2, gpre, gpost, w_in, ln_g, ln_b, w_s, b_s, w_out):
    n_tok = x2.shape[0]
    tm = GMLP_TOKENS
    const = lambda i: (0, 0)
    return pl.pallas_call(
        _gmlp_kernel,
        grid=(n_tok // tm,),
        in_specs=[
            pl.BlockSpec((tm, D_MODEL), lambda i: (i, 0)),
            pl.BlockSpec((1, D_MODEL), const),
            pl.BlockSpec((1, D_MODEL), const),
            pl.BlockSpec((D_MODEL, 3 * GMLP_WIDTH), const, pipeline_mode=pl.Buffered(1)),
            pl.BlockSpec((1, GMLP_WIDTH), const),
            pl.BlockSpec((1, GMLP_WIDTH), const),
            pl.BlockSpec((GMLP_GROUPS, GMLP_CHUNK, GMLP_CHUNK), lambda i: (0, 0, 0)),
            pl.BlockSpec((GMLP_CHUNK, GMLP_GROUPS), const),
            pl.BlockSpec((GMLP_WIDTH, D_MODEL), const, pipeline_mode=pl.Buffered(1)),
        ],
        out_specs=pl.BlockSpec((tm, D_MODEL), lambda i: (i, 0)),
        out_shape=jax.ShapeDtypeStruct(x2.shape, F32),
        scratch_shapes=[
            pltpu.VMEM((tm, D_MODEL), BF16),
            pltpu.VMEM((tm, GMLP_WIDTH), F32),
            pltpu.VMEM((tm, GMLP_WIDTH), BF16),
        ],
        compiler_params=pltpu.CompilerParams(
            dimension_semantics=("arbitrary",), vmem_limit_bytes=VMEM_LIMIT),
        name="gmlp_layer",
    )(x2, gpre.reshape(1, -1), gpost.reshape(1, -1), w_in.astype(BF16),
      ln_g.reshape(1, -1), ln_b.reshape(1, -1), w_s, jnp.transpose(b_s), w_out.astype(BF16))


def _fill_permuted(hs_sc, cos_ref, sin_ref, h_sc, cos_sc, sin_sc, dilation):
    rows = TILE // dilation
    for r in range(dilation):
        dst = slice(r * rows, (r + 1) * rows)
        src = slice(None) if dilation == 1 else pl.ds(r, rows, stride=dilation)
        for c in range(D_MODEL // LANES):
            h_sc[dst, c * LANES:(c + 1) * LANES] = hs_sc[c, src, :].astype(BF16)
        cos_sc[dst, :] = cos_ref[src, :]
        sin_sc[dst, :] = sin_ref[src, :]


def _proj_kernel(x_ref, gpre_ref, w_ref, cos_ref, sin_ref, o_ref, hs_sc, h_sc, cos_sc, sin_sc):
    j = pl.program_id(1)
    n_chunks = TILE // PROJ_ROW_CHUNK

    @pl.when(j == 0)
    def _():
        for rc in range(n_chunks):
            rows = slice(rc * PROJ_ROW_CHUNK, (rc + 1) * PROJ_ROW_CHUNK)
            hs = _rms_normalize(x_ref[rows, :], gpre_ref[...])
            for c in range(D_MODEL // LANES):
                hs_sc[c, rows, :] = hs[:, c * LANES:(c + 1) * LANES]

    for g, (_, dilation) in enumerate(DILATED_GROUPS):
        @pl.when(j == GROUP_FIRST_SLAB[g])
        def _(dilation=dilation):
            _fill_permuted(hs_sc, cos_ref, sin_ref, h_sc, cos_sc, sin_sc, dilation)

    is_q = functools.reduce(jnp.logical_or, [j == s for s in GROUP_FIRST_SLAB])
    is_k = functools.reduce(jnp.logical_or, [j == s + 1 for s in GROUP_FIRST_SLAB])
    is_rope = jnp.logical_or(is_q, is_k)

    @pl.when(is_rope)
    def _():
        scale = jnp.where(is_q, np.float32(1.0 / np.sqrt(HEAD_DIM)), np.float32(1.0))
        for rc in range(n_chunks):
            rows = slice(rc * PROJ_ROW_CHUNK, (rc + 1) * PROJ_ROW_CHUNK)
            p = jnp.dot(h_sc[rows, :], w_ref[...], preferred_element_type=F32)
            cos = cos_sc[rows, :] * scale
            sin = sin_sc[rows, :] * scale
            for s in range(ATTN_WIDTH // LANES):
                t = p[:, s * LANES:(s + 1) * LANES]
                o_ref[rows, s * LANES:(s + 1) * LANES] = (
                    t * cos + pltpu.roll(t, LANES // 2, 1) * sin).astype(BF16)

    @pl.when(jnp.logical_not(is_rope))
    def _():
        for rc in range(n_chunks):
            rows = slice(rc * PROJ_ROW_CHUNK, (rc + 1) * PROJ_ROW_CHUNK)
            o_ref[rows, :] = jnp.dot(h_sc[rows, :], w_ref[...], preferred_element_type=F32).astype(BF16)


def _proj_weight(w_in):
    half = HEAD_DIM // 2
    w6 = w_in.reshape(D_MODEL, N_PROJ_SLABS, N_HEADS // 2, 2, 2, half)
    swapped = jnp.transpose(w6, (0, 1, 2, 4, 3, 5))
    is_qk = np.array([s != Z_SLAB and s % 3 != 2 for s in range(N_PROJ_SLABS)])
    w6 = jnp.where(is_qk[None, :, None, None, None, None], swapped, w6)
    w3 = w6.reshape(D_MODEL, N_PROJ_SLABS, ATTN_WIDTH)
    order = np.array([0, 1, 2, Z_SLAB, 3, 4, 5, 6, 7, 8])
    return w3[:, order, :].reshape(D_MODEL, N_PROJ_SLABS * ATTN_WIDTH).astype(BF16)


def _proj_layer(x2, gpre, w_in, cos2, sin2, seq_len):
    n_tok = x2.shape[0]
    tiles_per_seq = seq_len // TILE
    return pl.pallas_call(
        _proj_kernel,
        grid=(n_tok // TILE, N_PROJ_SLABS),
        in_specs=[
            pl.BlockSpec((TILE, D_MODEL), lambda i, j: (i, 0)),
            pl.BlockSpec((1, D_MODEL), lambda i, j: (0, 0)),
            pl.BlockSpec((D_MODEL, ATTN_WIDTH), lambda i, j: (0, j)),
            pl.BlockSpec((TILE, LANES), lambda i, j: (i % tiles_per_seq, 0)),
            pl.BlockSpec((TILE, LANES), lambda i, j: (i % tiles_per_seq, 0)),
        ],
        out_specs=pl.BlockSpec((TILE, ATTN_WIDTH), lambda i, j: (i, j)),
        out_shape=jax.ShapeDtypeStruct((n_tok, N_PROJ_SLABS * ATTN_WIDTH), BF16),
        scratch_shapes=[
            pltpu.VMEM((D_MODEL // LANES, TILE, LANES), F32),
            pltpu.VMEM((TILE, D_MODEL), BF16),
            pltpu.VMEM((TILE, LANES), F32),
            pltpu.VMEM((TILE, LANES), F32),
        ],
        compiler_params=pltpu.CompilerParams(
            dimension_semantics=("arbitrary", "arbitrary"), vmem_limit_bytes=VMEM_LIMIT),
        name="attn_proj",
    )(x2, gpre.reshape(1, -1), _proj_weight(w_in), cos2, sin2)


def _attn_kernel(q_ref, kp_ref, kc_ref, vp_ref, vc_ref, o_ref, st_ref):
    n = pl.program_id(2)
    t = ATTN_BLOCK
    qi = lax.broadcasted_iota(jnp.int32, (t, 2 * t), 0)
    kj = lax.broadcasted_iota(jnp.int32, (t, 2 * t), 1)
    valid = jnp.logical_or(
        jnp.logical_and(jnp.logical_and(kj < t, kj >= qi), n > 0),
        jnp.logical_and(kj >= t, (kj - t) <= qi))
    lane = lax.broadcasted_iota(jnp.int32, (1, LANES), 1)
    st = jnp.zeros((t, LANES), F32)
    for hp in range(N_HEADS // 2):
        sl = slice(hp * LANES, (hp + 1) * LANES)
        q2 = q_ref[:, sl]
        k2 = jnp.concatenate([kp_ref[:, sl], kc_ref[:, sl]], axis=0)
        v2 = jnp.concatenate([vp_ref[:, sl], vc_ref[:, sl]], axis=0)
        o2 = jnp.zeros((t, LANES), F32)
        for e in range(2):
            qk_lanes = ((lane // (HEAD_DIM // 2)) % 2) == e
            v_lanes = (lane // HEAD_DIM) == e
            qe = jnp.where(qk_lanes, q2, jnp.zeros_like(q2))
            s = lax.dot_general(qe, k2, (((1,), (1,)), ((), ())), preferred_element_type=F32)
            s = jnp.where(valid, s, NEG_BIG)
            m = jnp.max(s, axis=-1, keepdims=True)
            p = jnp.exp(s - m)
            l = jnp.sum(p, axis=-1, keepdims=True)
            pv = jnp.dot(p.astype(BF16), v2, preferred_element_type=F32)
            o2 = jnp.where(v_lanes, pv / l, o2)
            st = jnp.where(lane == 2 * hp + e, m + jnp.log(l), st)
        o_ref[:, sl] = o2.astype(BF16)
    st_ref[...] = st


def _attn_group(proj2, g, dilation, batch, seq_len):
    n_tok = batch * seq_len
    t = ATTN_BLOCK
    n_blocks = seq_len // dilation // t
    tiles_per_seq = seq_len // TILE
    blocks_per_tile = TILE // t
    bpc = blocks_per_tile // dilation
    first = GROUP_FIRST_SLAB[g]

    def row_block(b, r, n):
        return (b * tiles_per_seq + n // bpc) * blocks_per_tile + r * bpc + n % bpc

    def spec(c, prev):
        if prev:
            return pl.BlockSpec((t, ATTN_WIDTH),
                                lambda b, r, n: (row_block(b, r, jnp.maximum(n - 1, 0)), first + c))
        return pl.BlockSpec((t, ATTN_WIDTH), lambda b, r, n: (row_block(b, r, n), first + c))

    return pl.pallas_call(
        _attn_kernel,
        grid=(batch, dilation, n_blocks),
        in_specs=[spec(0, False), spec(1, True), spec(1, False), spec(2, True), spec(2, False)],
        out_specs=[
            pl.BlockSpec((t, ATTN_WIDTH), lambda b, r, n: (row_block(b, r, n), 0)),
            pl.BlockSpec((t, LANES), lambda b, r, n: (row_block(b, r, n), 0)),
        ],
        out_shape=[
            jax.ShapeDtypeStruct((n_tok, ATTN_WIDTH), BF16),
            jax.ShapeDtypeStruct((n_tok, LANES), F32),
        ],
        compiler_params=pltpu.CompilerParams(
            dimension_semantics=("arbitrary", "arbitrary", "arbitrary"), vmem_limit_bytes=VMEM_LIMIT),
        name=f"dilated_attn_g{g}",
    )(proj2, proj2, proj2, proj2, proj2)


def _combine_kernel(x_ref, o0_ref, o1_ref, o2_ref, s0_ref, s1_ref, s2_ref, z_ref,
                    expand_ref, wout_ref, gpost_ref, out_ref, on1_sc, on2_sc, st1_sc, st2_sc, y_sc):
    n_slabs = ATTN_WIDTH // LANES
    for o_ref, s_ref, on_sc, st_sc, (_, dilation) in (
            (o1_ref, s1_ref, on1_sc, st1_sc, DILATED_GROUPS[1]),
            (o2_ref, s2_ref, on2_sc, st2_sc, DILATED_GROUPS[2])):
        rows = COMBINE_TOKENS // dilation
        for r in range(dilation):
            dst = pl.ds(r, rows, stride=dilation)
            st_sc[dst, :] = s_ref[0, r]
            for c in range(n_slabs):
                on_sc[c, dst, :] = o_ref[0, r, :, c * LANES:(c + 1) * LANES].astype(F32)

    lses = [s0_ref[...], st1_sc[...], st2_sc[...]]
    m = jnp.maximum(jnp.maximum(lses[0], lses[1]), lses[2])
    es = [jnp.exp(s - m) for s in lses]
    inv = 1.0 / (es[0] + es[1] + es[2])
    alphas = []
    for e in es:
        alpha = e * inv
        hi = alpha.astype(BF16)
        lo = (alpha - hi.astype(F32)).astype(BF16)
        alphas.append(jnp.dot(hi, expand_ref[...], preferred_element_type=F32)
                      + jnp.dot(lo, expand_ref[...], preferred_element_type=F32))
    for c in range(n_slabs):
        sl = slice(c * LANES, (c + 1) * LANES)
        o = (alphas[0][:, sl] * o0_ref[:, sl].astype(F32)
             + alphas[1][:, sl] * on1_sc[c] + alphas[2][:, sl] * on2_sc[c])
        y_sc[:, sl] = (o * _silu(z_ref[:, sl].astype(F32))).astype(BF16)
    out = jnp.dot(y_sc[...], wout_ref[...], preferred_element_type=F32)
    out_ref[...] = x_ref[...] + _rms_normalize(out, gpost_ref[...])


def _combine_layer(x2, outs, stats, proj2, w_out, gpost):
    n_tok = x2.shape[0]
    tm = COMBINE_TOKENS
    n_tiles = n_tok // TILE
    per_tile = TILE // tm
    head_of_lane = np.arange(ATTN_WIDTH) // HEAD_DIM
    expand = jnp.asarray(np.arange(LANES)[:, None] == head_of_lane[None, :], dtype=BF16)
    tok = lambda i: (i, 0)
    const = lambda i: (0, 0)
    in_tile = lambda i: (i // per_tile, 0, i % per_tile, 0)
    d1, d2 = DILATED_GROUPS[1][1], DILATED_GROUPS[2][1]
    return pl.pallas_call(
        _combine_kernel,
        grid=(n_tok // tm,),
        in_specs=[
            pl.BlockSpec((tm, D_MODEL), tok),
            pl.BlockSpec((tm, ATTN_WIDTH), tok),
            pl.BlockSpec((1, d1, tm // d1, ATTN_WIDTH), in_tile),
            pl.BlockSpec((1, d2, tm // d2, ATTN_WIDTH), in_tile),
            pl.BlockSpec((tm, LANES), tok),
            pl.BlockSpec((1, d1, tm // d1, LANES), in_tile),
            pl.BlockSpec((1, d2, tm // d2, LANES), in_tile),
            pl.BlockSpec((tm, ATTN_WIDTH), lambda i: (i, GATE_OUT_SLAB)),
            pl.BlockSpec((LANES, ATTN_WIDTH), const),
            pl.BlockSpec((ATTN_WIDTH, D_MODEL), const),
            pl.BlockSpec((1, D_MODEL), const),
        ],
        out_specs=pl.BlockSpec((tm, D_MODEL), tok),
        out_shape=jax.ShapeDtypeStruct(x2.shape, F32),
        scratch_shapes=[
            pltpu.VMEM((ATTN_WIDTH // LANES, tm, LANES), F32),
            pltpu.VMEM((ATTN_WIDTH // LANES, tm, LANES), F32),
            pltpu.VMEM((tm, LANES), F32),
            pltpu.VMEM((tm, LANES), F32),
            pltpu.VMEM((tm, ATTN_WIDTH), BF16),
        ],
        compiler_params=pltpu.CompilerParams(
            dimension_semantics=("arbitrary",), vmem_limit_bytes=VMEM_LIMIT),
        name="attn_combine",
    )(x2,
      outs[0],
      outs[1].reshape(n_tiles, d1, TILE // d1, ATTN_WIDTH),
      outs[2].reshape(n_tiles, d2, TILE // d2, ATTN_WIDTH),
      stats[0],
      stats[1].reshape(n_tiles, d1, TILE // d1, LANES),
      stats[2].reshape(n_tiles, d2, TILE // d2, LANES),
      proj2, expand, w_out.astype(BF16), gpost.reshape(1, -1))


def _rope_tables(seq_len):
    half = HEAD_DIM // 2
    inv_freq = 1.0 / (ROPE_THETA ** (jnp.arange(0, HEAD_DIM, 2, dtype=F32) / HEAD_DIM))
    ang = jnp.arange(seq_len, dtype=F32)[:, None] * inv_freq[None, :]
    cos, sin = jnp.cos(ang), jnp.sin(ang)
    cos2 = jnp.tile(cos, (1, LANES // half))
    sin2 = jnp.concatenate([-sin, -sin, sin, sin], axis=1)
    return cos2, sin2


def kernel(x, norm_pre, norm_post, a_w_in, a_ln_g, a_ln_b, a_w_s, a_b_s, a_w_out, b_w_in, b_w_out):
    batch, seq_len, d = x.shape
    depth = norm_pre.shape[0]
    cos2, sin2 = _rope_tables(seq_len)
    x2 = x.reshape(batch * seq_len, d)
    for i in range(depth):
        j = i // 2
        if i % 2 == 0:
            x2 = _gmlp_layer(x2, norm_pre[i], norm_post[i], a_w_in[j], a_ln_g[j], a_ln_b[j],
                             a_w_s[j], a_b_s[j], a_w_out[j])
        else:
            proj2 = _proj_layer(x2, norm_pre[i], b_w_in[j], cos2, sin2, seq_len)
            outs, stats = [], []
            for g, (_, dilation) in enumerate(DILATED_GROUPS):
                o, st = _attn_group(proj2, g, dilation, batch, seq_len)
                outs.append(o)
                stats.append(st)
            x2 = _combine_layer(x2, outs, stats, proj2, b_w_out[j], norm_post[i])
    return x2.reshape(batch, seq_len, d)
```

```python
import functools

import numpy as np
import jax
import jax.numpy as jnp
from jax import lax
from jax.experimental import pallas as pl
from jax.experimental.pallas import tpu as pltpu

F32 = jnp.float32
BF16 = jnp.bfloat16

D_MODEL = 1024
RMS_EPS = 1e-6
LN_EPS = 1e-5

GMLP_WIDTH = 2048
GMLP_CHUNK = 128
GMLP_GROUPS = 8
GMLP_GROUP_WIDTH = GMLP_WIDTH // GMLP_GROUPS

HEAD_DIM = 64
N_HEADS = 16
ATTN_WIDTH = N_HEADS * HEAD_DIM
DILATED_GROUPS = ((128, 1), (512, 4), (2048, 16))
N_DIL_GROUPS = len(DILATED_GROUPS)
ATTN_BLOCK = 128
ATTN_QB = 4
ROPE_THETA = 10000.0
N_PROJ_SLABS = 3 * N_DIL_GROUPS + 1
Z_SLAB = 3 * N_DIL_GROUPS
GROUP_FIRST_SLAB = (0, 4, 7)
GATE_OUT_SLAB = 3
TILE = 2048

LANES = 128
NEG_BIG = -1e30

GMLP_TOKENS = 512
PROJ_ROW_CHUNK = 512
COMBINE_TOKENS = 512
VMEM_LIMIT = 56 * 1024 * 1024


def _gelu(x):
    return 0.5 * x * (1.0 + lax.erf(x * np.float32(np.sqrt(0.5))))


def _silu(x):
    return x * jax.nn.sigmoid(x)


def _rms_normalize(x, gain):
    ms = jnp.mean(x * x, axis=-1, keepdims=True)
    return x * lax.rsqrt(ms + RMS_EPS) * gain


def _gmlp_kernel(x_ref, gpre_ref, gpost_ref, win_ref, lng_ref, lnb_ref, ws_ref,
                 bs_ref, wout_ref, o_ref, h_sc, v_sc, y_sc):
    tm = x_ref.shape[0]
    gw = GMLP_GROUP_WIDTH
    x = x_ref[...]
    h_sc[...] = _rms_normalize(x, gpre_ref[...]).astype(BF16)

    s1 = jnp.zeros((tm, 1), F32)
    for g in range(GMLP_GROUPS):
        c0 = GMLP_WIDTH + g * gw
        v = _gelu(jnp.dot(h_sc[...], win_ref[:, c0:c0 + gw], preferred_element_type=F32))
        v_sc[:, g * gw:(g + 1) * gw] = v
        s1 = s1 + jnp.sum(v, axis=-1, keepdims=True)
    mu = s1 * (1.0 / GMLP_WIDTH)
    s2 = jnp.zeros((tm, 1), F32)
    for g in range(GMLP_GROUPS):
        dv = v_sc[:, g * gw:(g + 1) * gw] - mu
        s2 = s2 + jnp.sum(dv * dv, axis=-1, keepdims=True)
    rstd = lax.rsqrt(s2 * (1.0 / GMLP_WIDTH) + LN_EPS)

    row = lax.broadcasted_iota(jnp.int32, (GMLP_CHUNK, GMLP_CHUNK), 0)
    col = lax.broadcasted_iota(jnp.int32, (GMLP_CHUNK, GMLP_CHUNK), 1)
    causal = col <= row
    for g in range(GMLP_GROUPS):
        sl = slice(g * gw, (g + 1) * gw)
        vn = ((v_sc[:, sl] - mu) * rstd * lng_ref[:, sl] + lnb_ref[:, sl]).astype(BF16)
        wm = jnp.where(causal, ws_ref[g], 0.0).astype(BF16)
        bias = bs_ref[:, g:g + 1]
        sv = jnp.concatenate(
            [jnp.dot(wm, vn[c * GMLP_CHUNK:(c + 1) * GMLP_CHUNK], preferred_element_type=F32) + bias
             for c in range(tm // GMLP_CHUNK)], axis=0)
        u = _gelu(jnp.dot(h_sc[...], win_ref[:, sl], preferred_element_type=F32))
        z0 = 2 * GMLP_WIDTH + g * gw
        z = jnp.dot(h_sc[...], win_ref[:, z0:z0 + gw], preferred_element_type=F32)
        y_sc[:, sl] = (u * sv * _silu(z)).astype(BF16)

    out = jnp.dot(y_sc[...], wout_ref[...], preferred_element_type=F32)
    o_ref[...] = x_ref[...] + _rms_normalize(out, gpost_ref[...])


def _gmlp_layer(x2, gpre, gpost, w_in, ln_g, ln_b, w_s, b_s, w_out):
    n_tok = x2.shape[0]
    tm = GMLP_TOKENS
    const = lambda i: (0, 0)
    return pl.pallas_call(
        _gmlp_kernel,
        grid=(n_tok // tm,),
        in_specs=[
            pl.BlockSpec((tm, D_MODEL), lambda i: (i, 0)),
            pl.BlockSpec((1, D_MODEL), const),
            pl.BlockSpec((1, D_MODEL), const),
            pl.BlockSpec((D_MODEL, 3 * GMLP_WIDTH), const, pipeline_mode=pl.Buffered(1)),
            pl.BlockSpec((1, GMLP_WIDTH), const),
            pl.BlockSpec((1, GMLP_WIDTH), const),
            pl.BlockSpec((GMLP_GROUPS, GMLP_CHUNK, GMLP_CHUNK), lambda i: (0, 0, 0)),
            pl.BlockSpec((GMLP_CHUNK, GMLP_GROUPS), const),
            pl.BlockSpec((GMLP_WIDTH, D_MODEL), const, pipeline_mode=pl.Buffered(1)),
        ],
        out_specs=pl.BlockSpec((tm, D_MODEL), lambda i: (i, 0)),
        out_shape=jax.ShapeDtypeStruct(x2.shape, F32),
        scratch_shapes=[
            pltpu.VMEM((tm, D_MODEL), BF16),
            pltpu.VMEM((tm, GMLP_WIDTH), F32),
            pltpu.VMEM((tm, GMLP_WIDTH), BF16),
        ],
        compiler_params=pltpu.CompilerParams(
            dimension_semantics=("arbitrary",), vmem_limit_bytes=VMEM_LIMIT),
        name="gmlp_layer",
    )(x2, gpre.reshape(1, -1), gpost.reshape(1, -1), w_in.astype(BF16),
      ln_g.reshape(1, -1), ln_b.reshape(1, -1), w_s, jnp.transpose(b_s), w_out.astype(BF16))


def _fill_permuted(hs_sc, cos_ref, sin_ref, h_sc, cos_sc, sin_sc, dilation):
    rows = TILE // dilation
    for r in range(dilation):
        dst = slice(r * rows, (r + 1) * rows)
        src = slice(None) if dilation == 1 else pl.ds(r, rows, stride=dilation)
        for c in range(D_MODEL // LANES):
            h_sc[dst, c * LANES:(c + 1) * LANES] = hs_sc[c, src, :].astype(BF16)
        cos_sc[dst, :] = cos_ref[src, :]
        sin_sc[dst, :] = sin_ref[src, :]


def _proj_kernel(x_ref, gpre_ref, w_ref, cos_ref, sin_ref, o_ref, hs_sc, h_sc, cos_sc, sin_sc):
    j = pl.program_id(1)
    n_chunks = TILE // PROJ_ROW_CHUNK

    @pl.when(j == 0)
    def _():
        for rc in range(n_chunks):
            rows = slice(rc * PROJ_ROW_CHUNK, (rc + 1) * PROJ_ROW_CHUNK)
            hs = _rms_normalize(x_ref[rows, :], gpre_ref[...])
            for c in range(D_MODEL // LANES):
                hs_sc[c, rows, :] = hs[:, c * LANES:(c + 1) * LANES]

    for g, (_, dilation) in enumerate(DILATED_GROUPS):
        @pl.when(j == GROUP_FIRST_SLAB[g])
        def _(dilation=dilation):
            _fill_permuted(hs_sc, cos_ref, sin_ref, h_sc, cos_sc, sin_sc, dilation)

    is_q = functools.reduce(jnp.logical_or, [j == s for s in GROUP_FIRST_SLAB])
    is_k = functools.reduce(jnp.logical_or, [j == s + 1 for s in GROUP_FIRST_SLAB])
    is_rope = jnp.logical_or(is_q, is_k)

    @pl.when(is_rope)
    def _():
        scale = jnp.where(is_q, np.float32(np.log2(np.e) / np.sqrt(HEAD_DIM)), np.float32(1.0))
        for rc in range(n_chunks):
            rows = slice(rc * PROJ_ROW_CHUNK, (rc + 1) * PROJ_ROW_CHUNK)
            p = jnp.dot(h_sc[rows, :], w_ref[...], preferred_element_type=F32)
            cos = cos_sc[rows, :] * scale
            sin = sin_sc[rows, :] * scale
            for s in range(ATTN_WIDTH // LANES):
                t = p[:, s * LANES:(s + 1) * LANES]
                o_ref[rows, s * LANES:(s + 1) * LANES] = (
                    t * cos + pltpu.roll(t, LANES // 2, 1) * sin).astype(BF16)

    @pl.when(jnp.logical_not(is_rope))
    def _():
        for rc in range(n_chunks):
            rows = slice(rc * PROJ_ROW_CHUNK, (rc + 1) * PROJ_ROW_CHUNK)
            o_ref[rows, :] = jnp.dot(h_sc[rows, :], w_ref[...], preferred_element_type=F32).astype(BF16)


def _proj_weight(w_in):
    half = HEAD_DIM // 2
    w6 = w_in.reshape(D_MODEL, N_PROJ_SLABS, N_HEADS // 2, 2, 2, half)
    swapped = jnp.transpose(w6, (0, 1, 2, 4, 3, 5))
    is_qk = np.array([s != Z_SLAB and s % 3 != 2 for s in range(N_PROJ_SLABS)])
    w6 = jnp.where(is_qk[None, :, None, None, None, None], swapped, w6)
    w3 = w6.reshape(D_MODEL, N_PROJ_SLABS, ATTN_WIDTH)
    order = np.array([0, 1, 2, Z_SLAB, 3, 4, 5, 6, 7, 8])
    return w3[:, order, :].reshape(D_MODEL, N_PROJ_SLABS * ATTN_WIDTH).astype(BF16)


def _proj_layer(x2, gpre, w_in, cos2, sin2, seq_len):
    n_tok = x2.shape[0]
    tiles_per_seq = seq_len // TILE
    return pl.pallas_call(
        _proj_kernel,
        grid=(n_tok // TILE, N_PROJ_SLABS),
        in_specs=[
            pl.BlockSpec((TILE, D_MODEL), lambda i, j: (i, 0)),
            pl.BlockSpec((1, D_MODEL), lambda i, j: (0, 0)),
            pl.BlockSpec((D_MODEL, ATTN_WIDTH), lambda i, j: (0, j)),
            pl.BlockSpec((TILE, LANES), lambda i, j: (i % tiles_per_seq, 0)),
            pl.BlockSpec((TILE, LANES), lambda i, j: (i % tiles_per_seq, 0)),
        ],
        out_specs=pl.BlockSpec((TILE, ATTN_WIDTH), lambda i, j: (i, j)),
        out_shape=jax.ShapeDtypeStruct((n_tok, N_PROJ_SLABS * ATTN_WIDTH), BF16),
        scratch_shapes=[
            pltpu.VMEM((D_MODEL // LANES, TILE, LANES), F32),
            pltpu.VMEM((TILE, D_MODEL), BF16),
            pltpu.VMEM((TILE, LANES), F32),
            pltpu.VMEM((TILE, LANES), F32),
        ],
        compiler_params=pltpu.CompilerParams(
            dimension_semantics=("arbitrary", "arbitrary"), vmem_limit_bytes=VMEM_LIMIT),
        name="attn_proj",
    )(x2, gpre.reshape(1, -1), _proj_weight(w_in), cos2, sin2)


def _attn_masks():
    t = ATTN_BLOCK
    kj = np.arange(2 * t)[:, None]
    qc = np.arange(t)[None, :]
    cur = (kj >= t) & (kj - t <= qc)
    prev = (kj < t) & (kj >= qc)
    valid = np.stack([cur, cur | prev])
    return jnp.asarray(np.where(valid, 0.0, NEG_BIG), dtype=BF16)


def _attn_kernel(q_ref, kp_ref, kc_ref, vp_ref, vc_ref, mask_ref, eye_ref, o_ref, st_ref, *, chained):
    t = ATTN_BLOCK
    first_step_mask = mask_ref[jnp.minimum(pl.program_id(2), 1)]
    eye = eye_ref[...]
    ones = jnp.ones((2 * t, LANES), BF16)
    lane = lax.broadcasted_iota(jnp.int32, (1, LANES), 1)
    head_a_qk = ((lane // (HEAD_DIM // 2)) % 2) == 0
    head_a_v = (lane // HEAD_DIM) == 0
    for jb in range(ATTN_QB):
        rows = slice(jb * t, (jb + 1) * t)
        if chained and jb > 0:
            prev_k, prev_v, prev_rows, mask_t = kc_ref, vc_ref, slice((jb - 1) * t, jb * t), mask_ref[1]
        else:
            prev_k, prev_v, mask_t = kp_ref, vp_ref, first_step_mask
            prev_rows = slice(0, t) if chained else rows
        st = jnp.zeros((t, LANES), F32)
        for hp in range(N_HEADS // 2):
            sl = slice(hp * LANES, (hp + 1) * LANES)
            q2 = q_ref[rows, sl]
            zero = jnp.zeros_like(q2)
            lhs = jnp.concatenate([
                jnp.concatenate([jnp.where(head_a_qk, q2, zero), eye], axis=1),
                jnp.concatenate([jnp.where(head_a_qk, zero, q2), eye], axis=1)], axis=0)
            k_ext = jnp.concatenate(
                [jnp.concatenate([prev_k[prev_rows, sl], kc_ref[rows, sl]], axis=0), mask_t], axis=1)
            v_ext = jnp.concatenate(
                [jnp.concatenate([prev_v[prev_rows, sl], vc_ref[rows, sl]], axis=0), ones], axis=1)
            s = lax.dot_general(lhs, k_ext, (((1,), (1,)), ((), ())), preferred_element_type=F32)
            m = jnp.max(s, axis=-1, keepdims=True)
            p = jnp.exp2(s - m).astype(BF16)
            pv = jnp.dot(p, v_ext, preferred_element_type=F32)
            o_ref[rows, sl] = jnp.where(head_a_v, pv[:t, :LANES], pv[t:, :LANES]).astype(BF16)
            denom = pv[:, LANES:]
            st = jnp.where(lane == 2 * hp, m[:t], st)
            st = jnp.where(lane == 2 * hp + 1, m[t:], st)
            st = jnp.where(lane == N_HEADS + 2 * hp, denom[:t], st)
            st = jnp.where(lane == N_HEADS + 2 * hp + 1, denom[t:], st)
        st_ref[rows, :] = st


def _attn_group(proj2, g, dilation, batch, seq_len):
    n_tok = batch * seq_len
    t, qb = ATTN_BLOCK, ATTN_QB
    n_blocks = seq_len // dilation // t
    tiles_per_seq = seq_len // TILE
    blocks_per_tile = TILE // t
    bpc = blocks_per_tile // dilation
    first = GROUP_FIRST_SLAB[g]
    chained = bpc >= qb
    if chained:
        assert bpc % qb == 0
        grid = (batch, dilation, n_blocks // qb)

        def row_block(b, r, n):
            return (b * tiles_per_seq + n // bpc) * blocks_per_tile + r * bpc + n % bpc

        cur = lambda b, r, n: row_block(b, r, qb * n) // qb
        prev = lambda b, r, n: row_block(b, r, jnp.maximum(qb * n - 1, 0))
        prev_rows = t
    else:
        assert bpc == 1 and dilation % qb == 0
        grid = (batch, dilation // qb, n_blocks)
        cur = lambda b, r, n: (b * tiles_per_seq + n) * (blocks_per_tile // qb) + r
        prev = lambda b, r, n: cur(b, r, jnp.maximum(n - 1, 0))
        prev_rows = qb * t

    def spec(c, is_prev):
        if is_prev:
            return pl.BlockSpec((prev_rows, ATTN_WIDTH), lambda b, r, n: (prev(b, r, n), first + c))
        return pl.BlockSpec((qb * t, ATTN_WIDTH), lambda b, r, n: (cur(b, r, n), first + c))

    return pl.pallas_call(
        functools.partial(_attn_kernel, chained=chained),
        grid=grid,
        in_specs=[spec(0, False), spec(1, True), spec(1, False), spec(2, True), spec(2, False),
                  pl.BlockSpec((2, 2 * t, t), lambda b, r, n: (0, 0, 0)),
                  pl.BlockSpec((t, t), lambda b, r, n: (0, 0))],
        out_specs=[
            pl.BlockSpec((qb * t, ATTN_WIDTH), lambda b, r, n: (cur(b, r, n), 0)),
            pl.BlockSpec((qb * t, LANES), lambda b, r, n: (cur(b, r, n), 0)),
        ],
        out_shape=[
            jax.ShapeDtypeStruct((n_tok, ATTN_WIDTH), BF16),
            jax.ShapeDtypeStruct((n_tok, LANES), F32),
        ],
        compiler_params=pltpu.CompilerParams(
            dimension_semantics=("arbitrary", "arbitrary", "arbitrary"), vmem_limit_bytes=VMEM_LIMIT),
        name=f"dilated_attn_g{g}",
    )(proj2, proj2, proj2, proj2, proj2, _attn_masks(), jnp.eye(t, dtype=BF16))


def _combine_kernel(x_ref, o0_ref, o1_ref, o2_ref, s0_ref, s1_ref, s2_ref, z_ref,
                    expand_ref, wout_ref, gpost_ref, out_ref, on1_sc, on2_sc, st1_sc, st2_sc, y_sc):
    n_slabs = ATTN_WIDTH // LANES
    for o_ref, s_ref, on_sc, st_sc, (_, dilation) in (
            (o1_ref, s1_ref, on1_sc, st1_sc, DILATED_GROUPS[1]),
            (o2_ref, s2_ref, on2_sc, st2_sc, DILATED_GROUPS[2])):
        rows = COMBINE_TOKENS // dilation
        for r in range(dilation):
            dst = pl.ds(r, rows, stride=dilation)
            st_sc[dst, :] = s_ref[0, r]
            for c in range(n_slabs):
                on_sc[c, dst, :] = o_ref[0, r, :, c * LANES:(c + 1) * LANES].astype(F32)

    sts = [s0_ref[...], st1_sc[...], st2_sc[...]]
    m = jnp.maximum(jnp.maximum(sts[0], sts[1]), sts[2])
    es = [jnp.exp2(s - m) for s in sts]
    total = None
    for e, s in zip(es, sts):
        term = e * pltpu.roll(s, LANES - N_HEADS, 1)
        total = term if total is None else total + term
    is_head_lane = lax.broadcasted_iota(jnp.int32, (1, LANES), 1) < N_HEADS
    inv = 1.0 / jnp.where(is_head_lane, total, 1.0)
    alphas = []
    for e in es:
        alpha = jnp.where(is_head_lane, e * inv, 0.0)
        hi = alpha.astype(BF16)
        lo = (alpha - hi.astype(F32)).astype(BF16)
        alphas.append(jnp.dot(hi, expand_ref[...], preferred_element_type=F32)
                      + jnp.dot(lo, expand_ref[...], preferred_element_type=F32))
    for c in range(n_slabs):
        sl = slice(c * LANES, (c + 1) * LANES)
        o = (alphas[0][:, sl] * o0_ref[:, sl].astype(F32)
             + alphas[1][:, sl] * on1_sc[c] + alphas[2][:, sl] * on2_sc[c])
        y_sc[:, sl] = (o * _silu(z_ref[:, sl].astype(F32))).astype(BF16)
    out = jnp.dot(y_sc[...], wout_ref[...], preferred_element_type=F32)
    out_ref[...] = x_ref[...] + _rms_normalize(out, gpost_ref[...])


def _combine_layer(x2, outs, stats, proj2, w_out, gpost):
    n_tok = x2.shape[0]
    tm = COMBINE_TOKENS
    n_tiles = n_tok // TILE
    per_tile = TILE // tm
    head_of_lane = np.arange(ATTN_WIDTH) // HEAD_DIM
    expand = jnp.asarray(np.arange(LANES)[:, None] == head_of_lane[None, :], dtype=BF16)
    tok = lambda i: (i, 0)
    const = lambda i: (0, 0)
    in_tile = lambda i: (i // per_tile, 0, i % per_tile, 0)
    d1, d2 = DILATED_GROUPS[1][1], DILATED_GROUPS[2][1]
    return pl.pallas_call(
        _combine_kernel,
        grid=(n_tok // tm,),
        in_specs=[
            pl.BlockSpec((tm, D_MODEL), tok),
            pl.BlockSpec((tm, ATTN_WIDTH), tok),
            pl.BlockSpec((1, d1, tm // d1, ATTN_WIDTH), in_tile),
            pl.BlockSpec((1, d2, tm // d2, ATTN_WIDTH), in_tile),
            pl.BlockSpec((tm, LANES), tok),
            pl.BlockSpec((1, d1, tm // d1, LANES), in_tile),
            pl.BlockSpec((1, d2, tm // d2, LANES), in_tile),
            pl.BlockSpec((tm, ATTN_WIDTH), lambda i: (i, GATE_OUT_SLAB)),
            pl.BlockSpec((LANES, ATTN_WIDTH), const),
            pl.BlockSpec((ATTN_WIDTH, D_MODEL), const),
            pl.BlockSpec((1, D_MODEL), const),
        ],
        out_specs=pl.BlockSpec((tm, D_MODEL), tok),
        out_shape=jax.ShapeDtypeStruct(x2.shape, F32),
        scratch_shapes=[
            pltpu.VMEM((ATTN_WIDTH // LANES, tm, LANES), F32),
            pltpu.VMEM((ATTN_WIDTH // LANES, tm, LANES), F32),
            pltpu.VMEM((tm, LANES), F32),
            pltpu.VMEM((tm, LANES), F32),
            pltpu.VMEM((tm, ATTN_WIDTH), BF16),
        ],
        compiler_params=pltpu.CompilerParams(
            dimension_semantics=("arbitrary",), vmem_limit_bytes=VMEM_LIMIT),
        name="attn_combine",
    )(x2,
      outs[0],
      outs[1].reshape(n_tiles, d1, TILE // d1, ATTN_WIDTH),
      outs[2].reshape(n_tiles, d2, TILE // d2, ATTN_WIDTH),
      stats[0],
      stats[1].reshape(n_tiles, d1, TILE // d1, LANES),
      stats[2].reshape(n_tiles, d2, TILE // d2, LANES),
      proj2, expand, w_out.astype(BF16), gpost.reshape(1, -1))


def _rope_tables(seq_len):
    half = HEAD_DIM // 2
    inv_freq = 1.0 / (ROPE_THETA ** (jnp.arange(0, HEAD_DIM, 2, dtype=F32) / HEAD_DIM))
    ang = jnp.arange(seq_len, dtype=F32)[:, None] * inv_freq[None, :]
    cos, sin = jnp.cos(ang), jnp.sin(ang)
    cos2 = jnp.tile(cos, (1, LANES // half))
    sin2 = jnp.concatenate([-sin, -sin, sin, sin], axis=1)
    return cos2, sin2


def kernel(x, norm_pre, norm_post, a_w_in, a_ln_g, a_ln_b, a_w_s, a_b_s, a_w_out, b_w_in, b_w_out):
    batch, seq_len, d = x.shape
    depth = norm_pre.shape[0]
    cos2, sin2 = _rope_tables(seq_len)
    x2 = x.reshape(batch * seq_len, d)
    for i in range(depth):
        j = i // 2
        if i % 2 == 0:
            x2 = _gmlp_layer(x2, norm_pre[i], norm_post[i], a_w_in[j], a_ln_g[j], a_ln_b[j],
                             a_w_s[j], a_b_s[j], a_w_out[j])
        else:
            proj2 = _proj_layer(x2, norm_pre[i], b_w_in[j], cos2, sin2, seq_len)
            outs, stats = [], []
            for g, (_, dilation) in enumerate(DILATED_GROUPS):
                o, st = _attn_group(proj2, g, dilation, batch, seq_len)
                outs.append(o)
                stats.append(st)
            x2 = _combine_layer(x2, outs, stats, proj2, b_w_out[j], norm_post[i])
    return x2.reshape(batch, seq_len, d)
```

```python
import functools

import numpy as np
import jax
import jax.numpy as jnp
from jax import lax
from jax.experimental import pallas as pl
from jax.experimental.pallas import tpu as pltpu

F32 = jnp.float32
BF16 = jnp.bfloat16

D_MODEL = 1024
RMS_EPS = 1e-6
LN_EPS = 1e-5

GMLP_WIDTH = 2048
GMLP_CHUNK = 128
GMLP_GROUPS = 8
GMLP_GROUP_WIDTH = GMLP_WIDTH // GMLP_GROUPS

HEAD_DIM = 64
N_HEADS = 16
ATTN_WIDTH = N_HEADS * HEAD_DIM
DILATED_GROUPS = ((128, 1), (512, 4), (2048, 16))
N_DIL_GROUPS = len(DILATED_GROUPS)
ATTN_BLOCK = 128
ATTN_QB = 4
ROPE_THETA = 10000.0
N_PROJ_SLABS = 3 * N_DIL_GROUPS + 1
Z_SLAB = 3 * N_DIL_GROUPS
GROUP_FIRST_SLAB = (0, 4, 7)
GATE_OUT_SLAB = 3
TILE = 2048

LANES = 128
NEG_BIG = -1e30

GMLP_TOKENS = 512
PROJ_ROW_CHUNK = 512
COMBINE_TOKENS = 512
VMEM_LIMIT = 56 * 1024 * 1024
WEIGHT_CAST_BLOCK_BYTES = 4 * 1024 * 1024


def _gelu(x):
    return 0.5 * x * (1.0 + lax.erf(x * np.float32(np.sqrt(0.5))))


def _silu(x):
    return x * jax.nn.sigmoid(x)


def _rms_normalize(x, gain):
    ms = jnp.mean(x * x, axis=-1, keepdims=True)
    return x * lax.rsqrt(ms + RMS_EPS) * gain


def _gmlp_kernel(x_ref, gpre_ref, gpost_ref, win_ref, lng_ref, lnb_ref, ws_ref,
                 bs_ref, wout_ref, o_ref, h_sc, v_sc, y_sc):
    tm = x_ref.shape[0]
    gw = GMLP_GROUP_WIDTH
    x = x_ref[...]
    h_sc[...] = _rms_normalize(x, gpre_ref[...]).astype(BF16)

    s1 = jnp.zeros((tm, 1), F32)
    for g in range(GMLP_GROUPS):
        c0 = GMLP_WIDTH + g * gw
        v = _gelu(jnp.dot(h_sc[...], win_ref[:, c0:c0 + gw], preferred_element_type=F32))
        v_sc[:, g * gw:(g + 1) * gw] = v
        s1 = s1 + jnp.sum(v, axis=-1, keepdims=True)
    mu = s1 * (1.0 / GMLP_WIDTH)
    s2 = jnp.zeros((tm, 1), F32)
    for g in range(GMLP_GROUPS):
        dv = v_sc[:, g * gw:(g + 1) * gw] - mu
        s2 = s2 + jnp.sum(dv * dv, axis=-1, keepdims=True)
    rstd = lax.rsqrt(s2 * (1.0 / GMLP_WIDTH) + LN_EPS)

    row = lax.broadcasted_iota(jnp.int32, (GMLP_CHUNK, GMLP_CHUNK), 0)
    col = lax.broadcasted_iota(jnp.int32, (GMLP_CHUNK, GMLP_CHUNK), 1)
    causal = col <= row
    for g in range(GMLP_GROUPS):
        sl = slice(g * gw, (g + 1) * gw)
        vn = ((v_sc[:, sl] - mu) * rstd * lng_ref[:, sl] + lnb_ref[:, sl]).astype(BF16)
        wm = jnp.where(causal, ws_ref[g], 0.0).astype(BF16)
        bias = bs_ref[:, g:g + 1]
        sv = jnp.concatenate(
            [jnp.dot(wm, vn[c * GMLP_CHUNK:(c + 1) * GMLP_CHUNK], preferred_element_type=F32) + bias
             for c in range(tm // GMLP_CHUNK)], axis=0)
        u = _gelu(jnp.dot(h_sc[...], win_ref[:, sl], preferred_element_type=F32))
        z0 = 2 * GMLP_WIDTH + g * gw
        z = jnp.dot(h_sc[...], win_ref[:, z0:z0 + gw], preferred_element_type=F32)
        y_sc[:, sl] = (u * sv * _silu(z)).astype(BF16)

    out = jnp.dot(y_sc[...], wout_ref[...], preferred_element_type=F32)
    o_ref[...] = x_ref[...] + _rms_normalize(out, gpost_ref[...])


def _gmlp_layer(x2, gpre, gpost, w_in, ln_g, ln_b, w_s, b_s, w_out):
    n_tok = x2.shape[0]
    tm = GMLP_TOKENS
    const = lambda i: (0, 0)
    return pl.pallas_call(
        _gmlp_kernel,
        grid=(n_tok // tm,),
        in_specs=[
            pl.BlockSpec((tm, D_MODEL), lambda i: (i, 0)),
            pl.BlockSpec((1, D_MODEL), const),
            pl.BlockSpec((1, D_MODEL), const),
            pl.BlockSpec((D_MODEL, 3 * GMLP_WIDTH), const, pipeline_mode=pl.Buffered(1)),
            pl.BlockSpec((1, GMLP_WIDTH), const),
            pl.BlockSpec((1, GMLP_WIDTH), const),
            pl.BlockSpec((GMLP_GROUPS, GMLP_CHUNK, GMLP_CHUNK), lambda i: (0, 0, 0)),
            pl.BlockSpec((GMLP_CHUNK, GMLP_GROUPS), const),
            pl.BlockSpec((GMLP_WIDTH, D_MODEL), const, pipeline_mode=pl.Buffered(1)),
        ],
        out_specs=pl.BlockSpec((tm, D_MODEL), lambda i: (i, 0)),
        out_shape=jax.ShapeDtypeStruct(x2.shape, F32),
        scratch_shapes=[
            pltpu.VMEM((tm, D_MODEL), BF16),
            pltpu.VMEM((tm, GMLP_WIDTH), F32),
            pltpu.VMEM((tm, GMLP_WIDTH), BF16),
        ],
        compiler_params=pltpu.CompilerParams(
            dimension_semantics=("arbitrary",), vmem_limit_bytes=VMEM_LIMIT),
        name="gmlp_layer",
    )(x2, gpre.reshape(1, -1), gpost.reshape(1, -1), w_in,
      ln_g.reshape(1, -1), ln_b.reshape(1, -1), w_s, jnp.transpose(b_s), w_out)


def _fill_permuted(hs_sc, cos_ref, sin_ref, h_sc, cos_sc, sin_sc, dilation):
    rows = TILE // dilation
    for r in range(dilation):
        dst = slice(r * rows, (r + 1) * rows)
        src = slice(None) if dilation == 1 else pl.ds(r, rows, stride=dilation)
        for c in range(D_MODEL // LANES):
            h_sc[dst, c * LANES:(c + 1) * LANES] = hs_sc[c, src, :].astype(BF16)
        cos_sc[dst, :] = cos_ref[src, :]
        sin_sc[dst, :] = sin_ref[src, :]


def _proj_kernel(x_ref, gpre_ref, w_ref, cos_ref, sin_ref, o_ref, hs_sc, h_sc, cos_sc, sin_sc):
    j = pl.program_id(1)
    n_chunks = TILE // PROJ_ROW_CHUNK

    @pl.when(j == 0)
    def _():
        for rc in range(n_chunks):
            rows = slice(rc * PROJ_ROW_CHUNK, (rc + 1) * PROJ_ROW_CHUNK)
            hs = _rms_normalize(x_ref[rows, :], gpre_ref[...])
            for c in range(D_MODEL // LANES):
                hs_sc[c, rows, :] = hs[:, c * LANES:(c + 1) * LANES]

    for g, (_, dilation) in enumerate(DILATED_GROUPS):
        @pl.when(j == GROUP_FIRST_SLAB[g])
        def _(dilation=dilation):
            _fill_permuted(hs_sc, cos_ref, sin_ref, h_sc, cos_sc, sin_sc, dilation)

    is_q = functools.reduce(jnp.logical_or, [j == s for s in GROUP_FIRST_SLAB])
    is_k = functools.reduce(jnp.logical_or, [j == s + 1 for s in GROUP_FIRST_SLAB])
    is_rope = jnp.logical_or(is_q, is_k)

    @pl.when(is_rope)
    def _():
        scale = jnp.where(is_q, np.float32(np.log2(np.e) / np.sqrt(HEAD_DIM)), np.float32(1.0))
        for rc in range(n_chunks):
            rows = slice(rc * PROJ_ROW_CHUNK, (rc + 1) * PROJ_ROW_CHUNK)
            p = jnp.dot(h_sc[rows, :], w_ref[...], preferred_element_type=F32)
            cos = cos_sc[rows, :] * scale
            sin = sin_sc[rows, :] * scale
            for s in range(ATTN_WIDTH // LANES):
                t = p[:, s * LANES:(s + 1) * LANES]
                o_ref[rows, s * LANES:(s + 1) * LANES] = (
                    t * cos + pltpu.roll(t, LANES // 2, 1) * sin).astype(BF16)

    @pl.when(jnp.logical_not(is_rope))
    def _():
        for rc in range(n_chunks):
            rows = slice(rc * PROJ_ROW_CHUNK, (rc + 1) * PROJ_ROW_CHUNK)
            o_ref[rows, :] = jnp.dot(h_sc[rows, :], w_ref[...], preferred_element_type=F32).astype(BF16)


def _proj_weight_kernel(w_ref, o_ref):
    s = pl.program_id(0)
    is_qk = functools.reduce(jnp.logical_or, [jnp.logical_or(s == f, s == f + 1) for f in GROUP_FIRST_SLAB])

    @pl.when(is_qk)
    def _():
        half = HEAD_DIM // 2
        quarter = lax.broadcasted_iota(jnp.int32, (1, LANES), 1) // half
        for c in range(ATTN_WIDTH // LANES):
            sl = slice(c * LANES, (c + 1) * LANES)
            w = w_ref[0, :, sl]
            w = jnp.where(quarter == 1, pltpu.roll(w, LANES - half, 1),
                          jnp.where(quarter == 2, pltpu.roll(w, half, 1), w))
            o_ref[:, sl] = w.astype(BF16)

    @pl.when(jnp.logical_not(is_qk))
    def _():
        o_ref[...] = w_ref[0].astype(BF16)


def _proj_weight(w_in_all, layer):
    def source_slab(s):
        return jnp.where(s == GATE_OUT_SLAB, Z_SLAB, jnp.where(s < GATE_OUT_SLAB, s, s - 1))

    return pl.pallas_call(
        _proj_weight_kernel,
        grid=(N_PROJ_SLABS,),
        in_specs=[pl.BlockSpec((1, D_MODEL, ATTN_WIDTH), lambda s: (layer, 0, source_slab(s)))],
        out_specs=pl.BlockSpec((D_MODEL, ATTN_WIDTH), lambda s: (0, s)),
        out_shape=jax.ShapeDtypeStruct((D_MODEL, N_PROJ_SLABS * ATTN_WIDTH), BF16),
        compiler_params=pltpu.CompilerParams(
            dimension_semantics=("arbitrary",), vmem_limit_bytes=VMEM_LIMIT),
        name="proj_weight_prep",
    )(w_in_all)


def _cast_kernel(w_ref, o_ref):
    o_ref[...] = w_ref[0].astype(BF16)


def _bf16_weight(w_all, layer):
    _, rows, cols = w_all.shape
    block_rows = rows
    while block_rows * cols * 4 > WEIGHT_CAST_BLOCK_BYTES:
        block_rows //= 2
    assert rows % block_rows == 0 and block_rows % 16 == 0
    return pl.pallas_call(
        _cast_kernel,
        grid=(rows // block_rows,),
        in_specs=[pl.BlockSpec((1, block_rows, cols), lambda i: (layer, i, 0))],
        out_specs=pl.BlockSpec((block_rows, cols), lambda i: (i, 0)),
        out_shape=jax.ShapeDtypeStruct((rows, cols), BF16),
        compiler_params=pltpu.CompilerParams(
            dimension_semantics=("arbitrary",), vmem_limit_bytes=VMEM_LIMIT),
        name="weight_cast",
    )(w_all)


def _proj_layer(x2, gpre, w_in, cos2, sin2, seq_len):
    n_tok = x2.shape[0]
    tiles_per_seq = seq_len // TILE
    return pl.pallas_call(
        _proj_kernel,
        grid=(n_tok // TILE, N_PROJ_SLABS),
        in_specs=[
            pl.BlockSpec((TILE, D_MODEL), lambda i, j: (i, 0)),
            pl.BlockSpec((1, D_MODEL), lambda i, j: (0, 0)),
            pl.BlockSpec((D_MODEL, ATTN_WIDTH), lambda i, j: (0, j)),
            pl.BlockSpec((TILE, LANES), lambda i, j: (i % tiles_per_seq, 0)),
            pl.BlockSpec((TILE, LANES), lambda i, j: (i % tiles_per_seq, 0)),
        ],
        out_specs=pl.BlockSpec((TILE, ATTN_WIDTH), lambda i, j: (i, j)),
        out_shape=jax.ShapeDtypeStruct((n_tok, N_PROJ_SLABS * ATTN_WIDTH), BF16),
        scratch_shapes=[
            pltpu.VMEM((D_MODEL // LANES, TILE, LANES), F32),
            pltpu.VMEM((TILE, D_MODEL), BF16),
            pltpu.VMEM((TILE, LANES), F32),
            pltpu.VMEM((TILE, LANES), F32),
        ],
        compiler_params=pltpu.CompilerParams(
            dimension_semantics=("arbitrary", "arbitrary"), vmem_limit_bytes=VMEM_LIMIT),
        name="attn_proj",
    )(x2, gpre.reshape(1, -1), w_in, cos2, sin2)


def _attn_masks():
    t = ATTN_BLOCK
    kj = np.arange(2 * t)[:, None]
    qc = np.arange(t)[None, :]
    cur = (kj >= t) & (kj - t <= qc)
    prev = (kj < t) & (kj >= qc)
    valid = np.stack([cur, cur | prev])
    return jnp.asarray(np.where(valid, 0.0, NEG_BIG), dtype=BF16)


def _attn_kernel(q_ref, kp_ref, kc_ref, vp_ref, vc_ref, mask_ref, eye_ref, o_ref, st_ref, *, chained):
    t = ATTN_BLOCK
    first_step_mask = mask_ref[jnp.minimum(pl.program_id(2), 1)]
    eye = eye_ref[...]
    ones = jnp.ones((2 * t, LANES), BF16)
    lane = lax.broadcasted_iota(jnp.int32, (1, LANES), 1)
    head_a_qk = ((lane // (HEAD_DIM // 2)) % 2) == 0
    head_a_v = (lane // HEAD_DIM) == 0
    for jb in range(ATTN_QB):
        rows = slice(jb * t, (jb + 1) * t)
        if chained and jb > 0:
            prev_k, prev_v, prev_rows, mask_t = kc_ref, vc_ref, slice((jb - 1) * t, jb * t), mask_ref[1]
        else:
            prev_k, prev_v, mask_t = kp_ref, vp_ref, first_step_mask
            prev_rows = slice(0, t) if chained else rows
        st = jnp.zeros((t, LANES), F32)
        for hp in range(N_HEADS // 2):
            sl = slice(hp * LANES, (hp + 1) * LANES)
            q2 = q_ref[rows, sl]
            zero = jnp.zeros_like(q2)
            lhs = jnp.concatenate([
                jnp.concatenate([jnp.where(head_a_qk, q2, zero), eye], axis=1),
                jnp.concatenate([jnp.where(head_a_qk, zero, q2), eye], axis=1)], axis=0)
            k_ext = jnp.concatenate(
                [jnp.concatenate([prev_k[prev_rows, sl], kc_ref[rows, sl]], axis=0), mask_t], axis=1)
            v_ext = jnp.concatenate(
                [jnp.concatenate([prev_v[prev_rows, sl], vc_ref[rows, sl]], axis=0), ones], axis=1)
            s = lax.dot_general(lhs, k_ext, (((1,), (1,)), ((), ())), preferred_element_type=F32)
            m = jnp.max(s, axis=-1, keepdims=True)
            p = jnp.exp2(s - m).astype(BF16)
            pv = jnp.dot(p, v_ext, preferred_element_type=F32)
            o_ref[rows, sl] = jnp.where(head_a_v, pv[:t, :LANES], pv[t:, :LANES]).astype(BF16)
            denom = pv[:, LANES:]
            st = jnp.where(lane == 2 * hp, m[:t], st)
            st = jnp.where(lane == 2 * hp + 1, m[t:], st)
            st = jnp.where(lane == N_HEADS + 2 * hp, denom[:t], st)
            st = jnp.where(lane == N_HEADS + 2 * hp + 1, denom[t:], st)
        st_ref[rows, :] = st


def _attn_group(proj2, g, dilation, batch, seq_len):
    n_tok = batch * seq_len
    t, qb = ATTN_BLOCK, ATTN_QB
    n_blocks = seq_len // dilation // t
    tiles_per_seq = seq_len // TILE
    blocks_per_tile = TILE // t
    bpc = blocks_per_tile // dilation
    first = GROUP_FIRST_SLAB[g]
    chained = bpc >= qb
    if chained:
        assert bpc % qb == 0
        grid = (batch, dilation, n_blocks // qb)

        def row_block(b, r, n):
            return (b * tiles_per_seq + n // bpc) * blocks_per_tile + r * bpc + n % bpc

        cur = lambda b, r, n: row_block(b, r, qb * n) // qb
        prev = lambda b, r, n: row_block(b, r, jnp.maximum(qb * n - 1, 0))
        prev_rows = t
    else:
        assert bpc == 1 and dilation % qb == 0
        grid = (batch, dilation // qb, n_blocks)
        cur = lambda b, r, n: (b * tiles_per_seq + n) * (blocks_per_tile // qb) + r
        prev = lambda b, r, n: cur(b, r, jnp.maximum(n - 1, 0))
        prev_rows = qb * t

    def spec(c, is_prev):
        if is_prev:
            return pl.BlockSpec((prev_rows, ATTN_WIDTH), lambda b, r, n: (prev(b, r, n), first + c))
        return pl.BlockSpec((qb * t, ATTN_WIDTH), lambda b, r, n: (cur(b, r, n), first + c))

    return pl.pallas_call(
        functools.partial(_attn_kernel, chained=chained),
        grid=grid,
        in_specs=[spec(0, False), spec(1, True), spec(1, False), spec(2, True), spec(2, False),
                  pl.BlockSpec((2, 2 * t, t), lambda b, r, n: (0, 0, 0)),
                  pl.BlockSpec((t, t), lambda b, r, n: (0, 0))],
        out_specs=[
            pl.BlockSpec((qb * t, ATTN_WIDTH), lambda b, r, n: (cur(b, r, n), 0)),
            pl.BlockSpec((qb * t, LANES), lambda b, r, n: (cur(b, r, n), 0)),
        ],
        out_shape=[
            jax.ShapeDtypeStruct((n_tok, ATTN_WIDTH), BF16),
            jax.ShapeDtypeStruct((n_tok, LANES), F32),
        ],
        compiler_params=pltpu.CompilerParams(
            dimension_semantics=("arbitrary", "arbitrary", "arbitrary"), vmem_limit_bytes=VMEM_LIMIT),
        name=f"dilated_attn_g{g}",
    )(proj2, proj2, proj2, proj2, proj2, _attn_masks(), jnp.eye(t, dtype=BF16))


def _combine_kernel(x_ref, o0_ref, o1_ref, o2_ref, s0_ref, s1_ref, s2_ref, z_ref,
                    expand_ref, wout_ref, gpost_ref, out_ref, on1_sc, on2_sc, st1_sc, st2_sc, y_sc):
    n_slabs = ATTN_WIDTH // LANES
    for o_ref, s_ref, on_sc, st_sc, (_, dilation) in (
            (o1_ref, s1_ref, on1_sc, st1_sc, DILATED_GROUPS[1]),
            (o2_ref, s2_ref, on2_sc, st2_sc, DILATED_GROUPS[2])):
        rows = COMBINE_TOKENS // dilation
        for r in range(dilation):
            dst = pl.ds(r, rows, stride=dilation)
            st_sc[dst, :] = s_ref[0, r]
            for c in range(n_slabs):
                on_sc[c, dst, :] = o_ref[0, r, :, c * LANES:(c + 1) * LANES].astype(F32)

    sts = [s0_ref[...], st1_sc[...], st2_sc[...]]
    m = jnp.maximum(jnp.maximum(sts[0], sts[1]), sts[2])
    es = [jnp.exp2(s - m) for s in sts]
    total = None
    for e, s in zip(es, sts):
        term = e * pltpu.roll(s, LANES - N_HEADS, 1)
        total = term if total is None else total + term
    is_head_lane = lax.broadcasted_iota(jnp.int32, (1, LANES), 1) < N_HEADS
    inv = 1.0 / jnp.where(is_head_lane, total, 1.0)
    alphas = []
    for e in es:
        alpha = jnp.where(is_head_lane, e * inv, 0.0)
        hi = alpha.astype(BF16)
        lo = (alpha - hi.astype(F32)).astype(BF16)
        alphas.append(jnp.dot(jnp.concatenate([hi, lo], axis=1), expand_ref[...],
                              preferred_element_type=F32))
    for c in range(n_slabs):
        sl = slice(c * LANES, (c + 1) * LANES)
        o = (alphas[0][:, sl] * o0_ref[:, sl].astype(F32)
             + alphas[1][:, sl] * on1_sc[c] + alphas[2][:, sl] * on2_sc[c])
        y_sc[:, sl] = (o * _silu(z_ref[:, sl].astype(F32))).astype(BF16)
    out = jnp.dot(y_sc[...], wout_ref[...], preferred_element_type=F32)
    out_ref[...] = x_ref[...] + _rms_normalize(out, gpost_ref[...])


def _combine_layer(x2, outs, stats, proj2, w_out, gpost):
    n_tok = x2.shape[0]
    tm = COMBINE_TOKENS
    n_tiles = n_tok // TILE
    per_tile = TILE // tm
    head_of_lane = np.arange(ATTN_WIDTH) // HEAD_DIM
    expand = jnp.asarray(np.tile(np.arange(LANES)[:, None] == head_of_lane[None, :], (2, 1)), dtype=BF16)
    tok = lambda i: (i, 0)
    const = lambda i: (0, 0)
    in_tile = lambda i: (i // per_tile, 0, i % per_tile, 0)
    d1, d2 = DILATED_GROUPS[1][1], DILATED_GROUPS[2][1]
    return pl.pallas_call(
        _combine_kernel,
        grid=(n_tok // tm,),
        in_specs=[
            pl.BlockSpec((tm, D_MODEL), tok),
            pl.BlockSpec((tm, ATTN_WIDTH), tok),
            pl.BlockSpec((1, d1, tm // d1, ATTN_WIDTH), in_tile),
            pl.BlockSpec((1, d2, tm // d2, ATTN_WIDTH), in_tile),
            pl.BlockSpec((tm, LANES), tok),
            pl.BlockSpec((1, d1, tm // d1, LANES), in_tile),
            pl.BlockSpec((1, d2, tm // d2, LANES), in_tile),
            pl.BlockSpec((tm, ATTN_WIDTH), lambda i: (i, GATE_OUT_SLAB)),
            pl.BlockSpec((2 * LANES, ATTN_WIDTH), const),
            pl.BlockSpec((ATTN_WIDTH, D_MODEL), const),
            pl.BlockSpec((1, D_MODEL), const),
        ],
        out_specs=pl.BlockSpec((tm, D_MODEL), tok),
        out_shape=jax.ShapeDtypeStruct(x2.shape, F32),
        scratch_shapes=[
            pltpu.VMEM((ATTN_WIDTH // LANES, tm, LANES), F32),
            pltpu.VMEM((ATTN_WIDTH // LANES, tm, LANES), F32),
            pltpu.VMEM((tm, LANES), F32),
            pltpu.VMEM((tm, LANES), F32),
            pltpu.VMEM((tm, ATTN_WIDTH), BF16),
        ],
        compiler_params=pltpu.CompilerParams(
            dimension_semantics=("arbitrary",), vmem_limit_bytes=VMEM_LIMIT),
        name="attn_combine",
    )(x2,
      outs[0],
      outs[1].reshape(n_tiles, d1, TILE // d1, ATTN_WIDTH),
      outs[2].reshape(n_tiles, d2, TILE // d2, ATTN_WIDTH),
      stats[0],
      stats[1].reshape(n_tiles, d1, TILE // d1, LANES),
      stats[2].reshape(n_tiles, d2, TILE // d2, LANES),
      proj2, expand, w_out, gpost.reshape(1, -1))


def _rope_tables(seq_len):
    half = HEAD_DIM // 2
    exponent = np.arange(0, HEAD_DIM, 2, dtype=np.float32) / np.float32(HEAD_DIM)
    power = np.power(np.float64(ROPE_THETA), exponent.astype(np.float64)).astype(np.float32)
    inv_freq = np.float32(1.0) / power
    ang = np.arange(seq_len, dtype=np.float32)[:, None] * inv_freq[None, :]
    cos = np.cos(ang.astype(np.float64)).astype(np.float32)
    sin = np.sin(ang.astype(np.float64)).astype(np.float32)
    cos2 = np.tile(cos, (1, LANES // half))
    sin2 = np.concatenate([-sin, -sin, sin, sin], axis=1)
    return jnp.asarray(cos2), jnp.asarray(sin2)


def kernel(x, norm_pre, norm_post, a_w_in, a_ln_g, a_ln_b, a_w_s, a_b_s, a_w_out, b_w_in, b_w_out):
    batch, seq_len, d = x.shape
    depth = norm_pre.shape[0]
    cos2, sin2 = _rope_tables(seq_len)
    x2 = x.reshape(batch * seq_len, d)
    for i in range(depth):
        j = i // 2
        if i % 2 == 0:
            x2 = _gmlp_layer(x2, norm_pre[i], norm_post[i], _bf16_weight(a_w_in, j), a_ln_g[j], a_ln_b[j],
                             a_w_s[j], a_b_s[j], _bf16_weight(a_w_out, j))
        else:
            proj2 = _proj_layer(x2, norm_pre[i], _proj_weight(b_w_in, j), cos2, sin2, seq_len)
            outs, stats = [], []
            for g, (_, dilation) in enumerate(DILATED_GROUPS):
                o, st = _attn_group(proj2, g, dilation, batch, seq_len)
                outs.append(o)
                stats.append(st)
            x2 = _combine_layer(x2, outs, stats, proj2, _bf16_weight(b_w_out, j), norm_post[i])
    return x2.reshape(batch, seq_len, d)
```

```python
import functools

import numpy as np
import jax
import jax.numpy as jnp
from jax import lax
from jax.experimental import pallas as pl
from jax.experimental.pallas import tpu as pltpu

F32 = jnp.float32
BF16 = jnp.bfloat16

D_MODEL = 1024
RMS_EPS = 1e-6
LN_EPS = 1e-5

GMLP_WIDTH = 2048
GMLP_CHUNK = 128
GMLP_GROUPS = 8
GMLP_GROUP_WIDTH = GMLP_WIDTH // GMLP_GROUPS

HEAD_DIM = 64
N_HEADS = 16
ATTN_WIDTH = N_HEADS * HEAD_DIM
DILATED_GROUPS = ((128, 1), (512, 4), (2048, 16))
N_DIL_GROUPS = len(DILATED_GROUPS)
ATTN_BLOCK = 128
ATTN_QB = (8, 4, 8)
ROPE_THETA = 10000.0
N_PROJ_SLABS = 3 * N_DIL_GROUPS + 1
Z_SLAB = 3 * N_DIL_GROUPS
GROUP_FIRST_SLAB = (0, 4, 7)
GATE_OUT_SLAB = 3
TILE = 2048

LANES = 128
NEG_BIG = -1e30

GMLP_TOKENS = 512
PROJ_ROW_CHUNK = 512
COMBINE_TOKENS = 512
VMEM_LIMIT = 56 * 1024 * 1024
WEIGHT_CAST_BLOCK_BYTES = 4 * 1024 * 1024


def _gelu(x):
    return 0.5 * x * (1.0 + lax.erf(x * np.float32(np.sqrt(0.5))))


def _silu(x):
    return x * jax.nn.sigmoid(x)


def _rms_normalize(x, gain):
    ms = jnp.mean(x * x, axis=-1, keepdims=True)
    return x * lax.rsqrt(ms + RMS_EPS) * gain


def _gmlp_kernel(x_ref, gpre_ref, gpost_ref, win_ref, lng_ref, lnb_ref, ws_ref,
                 bs_ref, wout_ref, o_ref, h_sc, v_sc, y_sc):
    tm = x_ref.shape[0]
    gw = GMLP_GROUP_WIDTH
    x = x_ref[...]
    h_sc[...] = _rms_normalize(x, gpre_ref[...]).astype(BF16)

    s1 = jnp.zeros((tm, 1), F32)
    for g in range(GMLP_GROUPS):
        c0 = GMLP_WIDTH + g * gw
        v = _gelu(jnp.dot(h_sc[...], win_ref[:, c0:c0 + gw], preferred_element_type=F32))
        v_sc[:, g * gw:(g + 1) * gw] = v
        s1 = s1 + jnp.sum(v, axis=-1, keepdims=True)
    mu = s1 * (1.0 / GMLP_WIDTH)
    s2 = jnp.zeros((tm, 1), F32)
    for g in range(GMLP_GROUPS):
        dv = v_sc[:, g * gw:(g + 1) * gw] - mu
        s2 = s2 + jnp.sum(dv * dv, axis=-1, keepdims=True)
    rstd = lax.rsqrt(s2 * (1.0 / GMLP_WIDTH) + LN_EPS)

    row = lax.broadcasted_iota(jnp.int32, (GMLP_CHUNK, GMLP_CHUNK), 0)
    col = lax.broadcasted_iota(jnp.int32, (GMLP_CHUNK, GMLP_CHUNK), 1)
    causal = col <= row
    for g in range(GMLP_GROUPS):
        sl = slice(g * gw, (g + 1) * gw)
        vn = ((v_sc[:, sl] - mu) * rstd * lng_ref[:, sl] + lnb_ref[:, sl]).astype(BF16)
        wm = jnp.where(causal, ws_ref[g], 0.0).astype(BF16)
        bias = bs_ref[:, g:g + 1]
        sv = jnp.concatenate(
            [jnp.dot(wm, vn[c * GMLP_CHUNK:(c + 1) * GMLP_CHUNK], preferred_element_type=F32) + bias
             for c in range(tm // GMLP_CHUNK)], axis=0)
        u = _gelu(jnp.dot(h_sc[...], win_ref[:, sl], preferred_element_type=F32))
        z0 = 2 * GMLP_WIDTH + g * gw
        z = jnp.dot(h_sc[...], win_ref[:, z0:z0 + gw], preferred_element_type=F32)
        y_sc[:, sl] = (u * sv * _silu(z)).astype(BF16)

    out = jnp.dot(y_sc[...], wout_ref[...], preferred_element_type=F32)
    o_ref[...] = x_ref[...] + _rms_normalize(out, gpost_ref[...])


def _gmlp_layer(x2, gpre, gpost, w_in, ln_g, ln_b, w_s, b_s, w_out):
    n_tok = x2.shape[0]
    tm = GMLP_TOKENS
    const = lambda i: (0, 0)
    return pl.pallas_call(
        _gmlp_kernel,
        grid=(n_tok // tm,),
        in_specs=[
            pl.BlockSpec((tm, D_MODEL), lambda i: (i, 0)),
            pl.BlockSpec((1, D_MODEL), const),
            pl.BlockSpec((1, D_MODEL), const),
            pl.BlockSpec((D_MODEL, 3 * GMLP_WIDTH), const, pipeline_mode=pl.Buffered(1)),
            pl.BlockSpec((1, GMLP_WIDTH), const),
            pl.BlockSpec((1, GMLP_WIDTH), const),
            pl.BlockSpec((GMLP_GROUPS, GMLP_CHUNK, GMLP_CHUNK), lambda i: (0, 0, 0)),
            pl.BlockSpec((GMLP_CHUNK, GMLP_GROUPS), const),
            pl.BlockSpec((GMLP_WIDTH, D_MODEL), const, pipeline_mode=pl.Buffered(1)),
        ],
        out_specs=pl.BlockSpec((tm, D_MODEL), lambda i: (i, 0)),
        out_shape=jax.ShapeDtypeStruct(x2.shape, F32),
        scratch_shapes=[
            pltpu.VMEM((tm, D_MODEL), BF16),
            pltpu.VMEM((tm, GMLP_WIDTH), F32),
            pltpu.VMEM((tm, GMLP_WIDTH), BF16),
        ],
        compiler_params=pltpu.CompilerParams(
            dimension_semantics=("arbitrary",), vmem_limit_bytes=VMEM_LIMIT),
        name="gmlp_layer",
    )(x2, gpre.reshape(1, -1), gpost.reshape(1, -1), w_in,
      ln_g.reshape(1, -1), ln_b.reshape(1, -1), w_s, jnp.transpose(b_s), w_out)


def _fill_permuted(hs_sc, cos_ref, sin_ref, dst, dilation, row_classes):
    h_sc, cos_sc, sin_sc = dst
    rows = TILE // dilation
    for r in row_classes:
        out = slice(r * rows, (r + 1) * rows)
        src = slice(None) if dilation == 1 else pl.ds(r, rows, stride=dilation)
        for c in range(D_MODEL // LANES):
            h_sc[out, c * LANES:(c + 1) * LANES] = hs_sc[c, src, :].astype(BF16)
        cos_sc[out, :] = cos_ref[src, :]
        sin_sc[out, :] = sin_ref[src, :]


def _proj_kernel(x_ref, gpre_ref, w_ref, cos_ref, sin_ref, o_ref, hs_sc,
                 h0_sc, cos0_sc, sin0_sc, h1_sc, cos1_sc, sin1_sc):
    j = pl.program_id(1)
    n_chunks = TILE // PROJ_ROW_CHUNK
    first1, first2 = GROUP_FIRST_SLAB[1], GROUP_FIRST_SLAB[2]
    bufs = ((h0_sc, cos0_sc, sin0_sc), (h1_sc, cos1_sc, sin1_sc))
    in_buf1 = jnp.logical_and(j >= first1, j < first2)

    @pl.when(j == 0)
    def _():
        for rc in range(n_chunks):
            rows = slice(rc * PROJ_ROW_CHUNK, (rc + 1) * PROJ_ROW_CHUNK)
            hs = _rms_normalize(x_ref[rows, :], gpre_ref[...])
            for c in range(D_MODEL // LANES):
                hs_sc[c, rows, :] = hs[:, c * LANES:(c + 1) * LANES]
        _fill_permuted(hs_sc, cos_ref, sin_ref, bufs[0], 1, range(1))

    is_q = functools.reduce(jnp.logical_or, [j == s for s in GROUP_FIRST_SLAB])
    is_k = functools.reduce(jnp.logical_or, [j == s + 1 for s in GROUP_FIRST_SLAB])
    is_rope = jnp.logical_or(is_q, is_k)

    def rope_slab(src):
        h_sc, cos_sc, sin_sc = src
        scale = jnp.where(is_q, np.float32(np.log2(np.e) / np.sqrt(HEAD_DIM)), np.float32(1.0))
        for rc in range(n_chunks):
            rows = slice(rc * PROJ_ROW_CHUNK, (rc + 1) * PROJ_ROW_CHUNK)
            p = jnp.dot(h_sc[rows, :], w_ref[...], preferred_element_type=F32)
            cos = cos_sc[rows, :] * scale
            sin = sin_sc[rows, :] * scale
            for s in range(ATTN_WIDTH // LANES):
                t = p[:, s * LANES:(s + 1) * LANES]
                o_ref[rows, s * LANES:(s + 1) * LANES] = (
                    t * cos + pltpu.roll(t, LANES // 2, 1) * sin).astype(BF16)

    def plain_slab(src, fill=None):
        for rc in range(n_chunks):
            rows = slice(rc * PROJ_ROW_CHUNK, (rc + 1) * PROJ_ROW_CHUNK)
            o_ref[rows, :] = jnp.dot(src[0][rows, :], w_ref[...], preferred_element_type=F32).astype(BF16)
            if fill is not None:
                dst, dilation = fill
                per_chunk = dilation // n_chunks
                _fill_permuted(hs_sc, cos_ref, sin_ref, dst, dilation,
                               range(rc * per_chunk, (rc + 1) * per_chunk))

    pl.when(jnp.logical_and(is_rope, jnp.logical_not(in_buf1)))(functools.partial(rope_slab, bufs[0]))
    pl.when(jnp.logical_and(is_rope, in_buf1))(functools.partial(rope_slab, bufs[1]))
    fill_steps = {first1 - 1: (bufs[0], (bufs[1], DILATED_GROUPS[1][1])),
                  first2 - 1: (bufs[1], (bufs[0], DILATED_GROUPS[2][1]))}
    for step, (src, fill) in fill_steps.items():
        pl.when(j == step)(functools.partial(plain_slab, src, fill))
    is_fill_step = functools.reduce(jnp.logical_or, [j == step for step in fill_steps])
    pl.when(jnp.logical_not(jnp.logical_or(is_rope, is_fill_step)))(functools.partial(plain_slab, bufs[0]))


def _proj_weight_kernel(w_ref, o_ref):
    s = pl.program_id(0)
    is_qk = functools.reduce(jnp.logical_or, [jnp.logical_or(s == f, s == f + 1) for f in GROUP_FIRST_SLAB])

    @pl.when(is_qk)
    def _():
        half = HEAD_DIM // 2
        quarter = lax.broadcasted_iota(jnp.int32, (1, LANES), 1) // half
        for c in range(ATTN_WIDTH // LANES):
            sl = slice(c * LANES, (c + 1) * LANES)
            w = w_ref[0, :, sl]
            w = jnp.where(quarter == 1, pltpu.roll(w, LANES - half, 1),
                          jnp.where(quarter == 2, pltpu.roll(w, half, 1), w))
            o_ref[:, sl] = w.astype(BF16)

    @pl.when(jnp.logical_not(is_qk))
    def _():
        o_ref[...] = w_ref[0].astype(BF16)


def _proj_weight(w_in_all, layer):
    def source_slab(s):
        return jnp.where(s == GATE_OUT_SLAB, Z_SLAB, jnp.where(s < GATE_OUT_SLAB, s, s - 1))

    return pl.pallas_call(
        _proj_weight_kernel,
        grid=(N_PROJ_SLABS,),
        in_specs=[pl.BlockSpec((1, D_MODEL, ATTN_WIDTH), lambda s: (layer, 0, source_slab(s)))],
        out_specs=pl.BlockSpec((D_MODEL, ATTN_WIDTH), lambda s: (0, s)),
        out_shape=jax.ShapeDtypeStruct((D_MODEL, N_PROJ_SLABS * ATTN_WIDTH), BF16),
        compiler_params=pltpu.CompilerParams(
            dimension_semantics=("arbitrary",), vmem_limit_bytes=VMEM_LIMIT),
        name="proj_weight_prep",
    )(w_in_all)


def _cast_kernel(w_ref, o_ref):
    o_ref[...] = w_ref[0].astype(BF16)


def _bf16_weight(w_all, layer):
    _, rows, cols = w_all.shape
    block_rows = rows
    while block_rows * cols * 4 > WEIGHT_CAST_BLOCK_BYTES:
        block_rows //= 2
    assert rows % block_rows == 0 and block_rows % 16 == 0
    return pl.pallas_call(
        _cast_kernel,
        grid=(rows // block_rows,),
        in_specs=[pl.BlockSpec((1, block_rows, cols), lambda i: (layer, i, 0))],
        out_specs=pl.BlockSpec((block_rows, cols), lambda i: (i, 0)),
        out_shape=jax.ShapeDtypeStruct((rows, cols), BF16),
        compiler_params=pltpu.CompilerParams(
            dimension_semantics=("arbitrary",), vmem_limit_bytes=VMEM_LIMIT),
        name="weight_cast",
    )(w_all)


def _proj_layer(x2, gpre, w_in, cos2, sin2, seq_len):
    n_tok = x2.shape[0]
    tiles_per_seq = seq_len // TILE
    return pl.pallas_call(
        _proj_kernel,
        grid=(n_tok // TILE, N_PROJ_SLABS),
        in_specs=[
            pl.BlockSpec((TILE, D_MODEL), lambda i, j: (i, 0)),
            pl.BlockSpec((1, D_MODEL), lambda i, j: (0, 0)),
            pl.BlockSpec((D_MODEL, ATTN_WIDTH), lambda i, j: (0, j)),
            pl.BlockSpec((TILE, LANES), lambda i, j: (i % tiles_per_seq, 0)),
            pl.BlockSpec((TILE, LANES), lambda i, j: (i % tiles_per_seq, 0)),
        ],
        out_specs=pl.BlockSpec((TILE, ATTN_WIDTH), lambda i, j: (i, j)),
        out_shape=jax.ShapeDtypeStruct((n_tok, N_PROJ_SLABS * ATTN_WIDTH), BF16),
        scratch_shapes=[
            pltpu.VMEM((D_MODEL // LANES, TILE, LANES), F32),
        ] + 2 * [
            pltpu.VMEM((TILE, D_MODEL), BF16),
            pltpu.VMEM((TILE, LANES), F32),
            pltpu.VMEM((TILE, LANES), F32),
        ],
        compiler_params=pltpu.CompilerParams(
            dimension_semantics=("arbitrary", "arbitrary"), vmem_limit_bytes=VMEM_LIMIT),
        name="attn_proj",
    )(x2, gpre.reshape(1, -1), w_in, cos2, sin2)


def _attn_masks():
    t = ATTN_BLOCK
    kj = np.arange(2 * t)[:, None]
    qc = np.arange(t)[None, :]
    cur = (kj >= t) & (kj - t <= qc)
    prev = (kj < t) & (kj >= qc)
    valid = np.stack([cur, cur | prev])
    return jnp.asarray(np.where(valid, 0.0, NEG_BIG), dtype=BF16)


def _attn_kernel(q_ref, kp_ref, kc_ref, vp_ref, vc_ref, mask_ref, eye_ref, o_ref, st_ref, *, chained):
    t = ATTN_BLOCK
    first_step_mask = mask_ref[jnp.minimum(pl.program_id(2), 1)]
    eye = eye_ref[...]
    ones = jnp.ones((2 * t, LANES), BF16)
    lane = lax.broadcasted_iota(jnp.int32, (1, LANES), 1)
    head_a_qk = ((lane // (HEAD_DIM // 2)) % 2) == 0
    head_a_v = (lane // HEAD_DIM) == 0
    for jb in range(q_ref.shape[0] // t):
        rows = slice(jb * t, (jb + 1) * t)
        if chained and jb > 0:
            prev_k, prev_v, prev_rows, mask_t = kc_ref, vc_ref, slice((jb - 1) * t, jb * t), mask_ref[1]
        else:
            prev_k, prev_v, mask_t = kp_ref, vp_ref, first_step_mask
            prev_rows = slice(0, t) if chained else rows
        st = jnp.zeros((t, LANES), F32)
        for hp in range(N_HEADS // 2):
            sl = slice(hp * LANES, (hp + 1) * LANES)
            q2 = q_ref[rows, sl]
            zero = jnp.zeros_like(q2)
            lhs = jnp.concatenate([
                jnp.concatenate([jnp.where(head_a_qk, q2, zero), eye], axis=1),
                jnp.concatenate([jnp.where(head_a_qk, zero, q2), eye], axis=1)], axis=0)
            k_ext = jnp.concatenate(
                [jnp.concatenate([prev_k[prev_rows, sl], kc_ref[rows, sl]], axis=0), mask_t], axis=1)
            v_ext = jnp.concatenate(
                [jnp.concatenate([prev_v[prev_rows, sl], vc_ref[rows, sl]], axis=0), ones], axis=1)
            s = lax.dot_general(lhs, k_ext, (((1,), (1,)), ((), ())), preferred_element_type=F32)
            m = jnp.max(s, axis=-1, keepdims=True)
            p = jnp.exp2(s - m).astype(BF16)
            pv = jnp.dot(p, v_ext, preferred_element_type=F32)
            o_ref[rows, sl] = jnp.where(head_a_v, pv[:t, :LANES], pv[t:, :LANES]).astype(BF16)
            denom = pv[:, LANES:]
            st = jnp.where(lane == 2 * hp, m[:t], st)
            st = jnp.where(lane == 2 * hp + 1, m[t:], st)
            st = jnp.where(lane == N_HEADS + 2 * hp, denom[:t], st)
            st = jnp.where(lane == N_HEADS + 2 * hp + 1, denom[t:], st)
        st_ref[rows, :] = st


def _attn_group(proj2, g, dilation, batch, seq_len):
    n_tok = batch * seq_len
    t, qb = ATTN_BLOCK, ATTN_QB[g]
    n_blocks = seq_len // dilation // t
    tiles_per_seq = seq_len // TILE
    blocks_per_tile = TILE // t
    bpc = blocks_per_tile // dilation
    first = GROUP_FIRST_SLAB[g]
    chained = bpc >= qb
    if chained:
        assert bpc % qb == 0
        grid = (batch, dilation, n_blocks // qb)

        def row_block(b, r, n):
            return (b * tiles_per_seq + n // bpc) * blocks_per_tile + r * bpc + n % bpc

        cur = lambda b, r, n: row_block(b, r, qb * n) // qb
        prev = lambda b, r, n: row_block(b, r, jnp.maximum(qb * n - 1, 0))
        prev_rows = t
    else:
        assert bpc == 1 and dilation % qb == 0
        grid = (batch, dilation // qb, n_blocks)
        cur = lambda b, r, n: (b * tiles_per_seq + n) * (blocks_per_tile // qb) + r
        prev = lambda b, r, n: cur(b, r, jnp.maximum(n - 1, 0))
        prev_rows = qb * t

    def spec(c, is_prev):
        if is_prev:
            return pl.BlockSpec((prev_rows, ATTN_WIDTH), lambda b, r, n: (prev(b, r, n), first + c))
        return pl.BlockSpec((qb * t, ATTN_WIDTH), lambda b, r, n: (cur(b, r, n), first + c))

    return pl.pallas_call(
        functools.partial(_attn_kernel, chained=chained),
        grid=grid,
        in_specs=[spec(0, False), spec(1, True), spec(1, False), spec(2, True), spec(2, False),
                  pl.BlockSpec((2, 2 * t, t), lambda b, r, n: (0, 0, 0)),
                  pl.BlockSpec((t, t), lambda b, r, n: (0, 0))],
        out_specs=[
            pl.BlockSpec((qb * t, ATTN_WIDTH), lambda b, r, n: (cur(b, r, n), 0)),
            pl.BlockSpec((qb * t, LANES), lambda b, r, n: (cur(b, r, n), 0)),
        ],
        out_shape=[
            jax.ShapeDtypeStruct((n_tok, ATTN_WIDTH), BF16),
            jax.ShapeDtypeStruct((n_tok, LANES), F32),
        ],
        compiler_params=pltpu.CompilerParams(
            dimension_semantics=("arbitrary", "arbitrary", "arbitrary"), vmem_limit_bytes=VMEM_LIMIT),
        name=f"dilated_attn_g{g}",
    )(proj2, proj2, proj2, proj2, proj2, _attn_masks(), jnp.eye(t, dtype=BF16))


def _combine_kernel(x_ref, o0_ref, o1_ref, o2_ref, s0_ref, s1_ref, s2_ref, z_ref,
                    expand_ref, wout_ref, gpost_ref, out_ref, on1_sc, on2_sc, st1_sc, st2_sc, y_sc):
    n_slabs = ATTN_WIDTH // LANES
    for o_ref, s_ref, on_sc, st_sc, (_, dilation) in (
            (o1_ref, s1_ref, on1_sc, st1_sc, DILATED_GROUPS[1]),
            (o2_ref, s2_ref, on2_sc, st2_sc, DILATED_GROUPS[2])):
        rows = COMBINE_TOKENS // dilation
        for r in range(dilation):
            dst = pl.ds(r, rows, stride=dilation)
            st_sc[dst, :] = s_ref[0, r]
            for c in range(n_slabs):
                on_sc[c, dst, :] = o_ref[0, r, :, c * LANES:(c + 1) * LANES].astype(F32)

    sts = [s0_ref[...], st1_sc[...], st2_sc[...]]
    m = jnp.maximum(jnp.maximum(sts[0], sts[1]), sts[2])
    es = [jnp.exp2(s - m) for s in sts]
    total = None
    for e, s in zip(es, sts):
        term = e * pltpu.roll(s, LANES - N_HEADS, 1)
        total = term if total is None else total + term
    is_head_lane = lax.broadcasted_iota(jnp.int32, (1, LANES), 1) < N_HEADS
    inv = 1.0 / jnp.where(is_head_lane, total, 1.0)
    alphas = []
    for e in es:
        alpha = jnp.where(is_head_lane, e * inv, 0.0)
        hi = alpha.astype(BF16)
        lo = (alpha - hi.astype(F32)).astype(BF16)
        alphas.append(jnp.dot(jnp.concatenate([hi, lo], axis=1), expand_ref[...],
                              preferred_element_type=F32))
    for c in range(n_slabs):
        sl = slice(c * LANES, (c + 1) * LANES)
        o = (alphas[0][:, sl] * o0_ref[:, sl].astype(F32)
             + alphas[1][:, sl] * on1_sc[c] + alphas[2][:, sl] * on2_sc[c])
        y_sc[:, sl] = (o * _silu(z_ref[:, sl].astype(F32))).astype(BF16)
    out = jnp.dot(y_sc[...], wout_ref[...], preferred_element_type=F32)
    out_ref[...] = x_ref[...] + _rms_normalize(out, gpost_ref[...])


def _combine_layer(x2, outs, stats, proj2, w_out, gpost):
    n_tok = x2.shape[0]
    tm = COMBINE_TOKENS
    n_tiles = n_tok // TILE
    per_tile = TILE // tm
    head_of_lane = np.arange(ATTN_WIDTH) // HEAD_DIM
    expand = jnp.asarray(np.tile(np.arange(LANES)[:, None] == head_of_lane[None, :], (2, 1)), dtype=BF16)
    tok = lambda i: (i, 0)
    const = lambda i: (0, 0)
    in_tile = lambda i: (i // per_tile, 0, i % per_tile, 0)
    d1, d2 = DILATED_GROUPS[1][1], DILATED_GROUPS[2][1]
    return pl.pallas_call(
        _combine_kernel,
        grid=(n_tok // tm,),
        in_specs=[
            pl.BlockSpec((tm, D_MODEL), tok),
            pl.BlockSpec((tm, ATTN_WIDTH), tok),
            pl.BlockSpec((1, d1, tm // d1, ATTN_WIDTH), in_tile),
            pl.BlockSpec((1, d2, tm // d2, ATTN_WIDTH), in_tile),
            pl.BlockSpec((tm, LANES), tok),
            pl.BlockSpec((1, d1, tm // d1, LANES), in_tile),
            pl.BlockSpec((1, d2, tm // d2, LANES), in_tile),
            pl.BlockSpec((tm, ATTN_WIDTH), lambda i: (i, GATE_OUT_SLAB)),
            pl.BlockSpec((2 * LANES, ATTN_WIDTH), const),
            pl.BlockSpec((ATTN_WIDTH, D_MODEL), const),
            pl.BlockSpec((1, D_MODEL), const),
        ],
        out_specs=pl.BlockSpec((tm, D_MODEL), tok),
        out_shape=jax.ShapeDtypeStruct(x2.shape, F32),
        scratch_shapes=[
            pltpu.VMEM((ATTN_WIDTH // LANES, tm, LANES), F32),
            pltpu.VMEM((ATTN_WIDTH // LANES, tm, LANES), F32),
            pltpu.VMEM((tm, LANES), F32),
            pltpu.VMEM((tm, LANES), F32),
            pltpu.VMEM((tm, ATTN_WIDTH), BF16),
        ],
        compiler_params=pltpu.CompilerParams(
            dimension_semantics=("arbitrary",), vmem_limit_bytes=VMEM_LIMIT),
        name="attn_combine",
    )(x2,
      outs[0],
      outs[1].reshape(n_tiles, d1, TILE // d1, ATTN_WIDTH),
      outs[2].reshape(n_tiles, d2, TILE // d2, ATTN_WIDTH),
      stats[0],
      stats[1].reshape(n_tiles, d1, TILE // d1, LANES),
      stats[2].reshape(n_tiles, d2, TILE // d2, LANES),
      proj2, expand, w_out, gpost.reshape(1, -1))


def _rope_tables(seq_len):
    half = HEAD_DIM // 2
    exponent = np.arange(0, HEAD_DIM, 2, dtype=np.float32) / np.float32(HEAD_DIM)
    power = np.power(np.float64(ROPE_THETA), exponent.astype(np.float64)).astype(np.float32)
    inv_freq = np.float32(1.0) / power
    ang = np.arange(seq_len, dtype=np.float32)[:, None] * inv_freq[None, :]
    cos = np.cos(ang.astype(np.float64)).astype(np.float32)
    sin = np.sin(ang.astype(np.float64)).astype(np.float32)
    cos2 = np.tile(cos, (1, LANES // half))
    sin2 = np.concatenate([-sin, -sin, sin, sin], axis=1)
    return jnp.asarray(cos2), jnp.asarray(sin2)


def kernel(x, norm_pre, norm_post, a_w_in, a_ln_g, a_ln_b, a_w_s, a_b_s, a_w_out, b_w_in, b_w_out):
    batch, seq_len, d = x.shape
    depth = norm_pre.shape[0]
    cos2, sin2 = _rope_tables(seq_len)
    x2 = x.reshape(batch * seq_len, d)
    for i in range(depth):
        j = i // 2
        if i % 2 == 0:
            x2 = _gmlp_layer(x2, norm_pre[i], norm_post[i], _bf16_weight(a_w_in, j), a_ln_g[j], a_ln_b[j],
                             a_w_s[j], a_b_s[j], _bf16_weight(a_w_out, j))
        else:
            proj2 = _proj_layer(x2, norm_pre[i], _proj_weight(b_w_in, j), cos2, sin2, seq_len)
            outs, stats = [], []
            for g, (_, dilation) in enumerate(DILATED_GROUPS):
                o, st = _attn_group(proj2, g, dilation, batch, seq_len)
                outs.append(o)
                stats.append(st)
            x2 = _combine_layer(x2, outs, stats, proj2, _bf16_weight(b_w_out, j), norm_post[i])
    return x2.reshape(batch, seq_len, d)
```

```python
import functools

import numpy as np
import jax
import jax.numpy as jnp
from jax import lax
from jax.experimental import pallas as pl
from jax.experimental.pallas import tpu as pltpu

F32 = jnp.float32
BF16 = jnp.bfloat16

D_MODEL = 1024
RMS_EPS = 1e-6
LN_EPS = 1e-5

GMLP_WIDTH = 2048
GMLP_CHUNK = 128
GMLP_GROUPS = 8
GMLP_GROUP_WIDTH = GMLP_WIDTH // GMLP_GROUPS
GMLP_DOT_GROUPS = 4

HEAD_DIM = 64
N_HEADS = 16
ATTN_WIDTH = N_HEADS * HEAD_DIM
DILATED_GROUPS = ((128, 1), (512, 4), (2048, 16))
N_DIL_GROUPS = len(DILATED_GROUPS)
ATTN_BLOCK = 128
ATTN_QB = (8, 4, 8)
ROPE_THETA = 10000.0
N_PROJ_SLABS = 3 * N_DIL_GROUPS + 1
Z_SLAB = 3 * N_DIL_GROUPS
GROUP_FIRST_SLAB = (0, 4, 7)
GATE_OUT_SLAB = 3
TILE = 2048

LANES = 128
NEG_BIG = -1e30

GMLP_TOKENS = 512
PROJ_ROW_CHUNK = 512
COMBINE_TOKENS = 512
VMEM_LIMIT = 56 * 1024 * 1024
WEIGHT_CAST_BLOCK_BYTES = 4 * 1024 * 1024


def _gelu(x):
    return 0.5 * x * (1.0 + lax.erf(x * np.float32(np.sqrt(0.5))))


def _silu(x):
    return x * jax.nn.sigmoid(x)


def _rms_normalize(x, gain):
    ms = jnp.mean(x * x, axis=-1, keepdims=True)
    return x * lax.rsqrt(ms + RMS_EPS) * gain


def _gmlp_kernel(x_ref, gpre_ref, gpost_ref, win_ref, lng_ref, lnb_ref, ws_ref,
                 bs_ref, wout_ref, o_ref, h_sc, v_sc, t_sc, y_sc):
    tm = x_ref.shape[0]
    gw = GMLP_GROUP_WIDTH
    x = x_ref[...]
    h_sc[...] = _rms_normalize(x, gpre_ref[...]).astype(BF16)

    pw = GMLP_DOT_GROUPS * gw
    s1 = jnp.zeros((tm, 1), F32)
    for gp in range(GMLP_WIDTH // pw):
        c0 = GMLP_WIDTH + gp * pw
        v = _gelu(jnp.dot(h_sc[...], win_ref[:, c0:c0 + pw], preferred_element_type=F32))
        v_sc[:, gp * pw:(gp + 1) * pw] = v
        s1 = s1 + jnp.sum(v, axis=-1, keepdims=True)
    for gp in range(GMLP_WIDTH // pw):
        psl = slice(gp * pw, (gp + 1) * pw)
        u = _gelu(jnp.dot(h_sc[...], win_ref[:, psl], preferred_element_type=F32))
        z0 = 2 * GMLP_WIDTH + gp * pw
        z = jnp.dot(h_sc[...], win_ref[:, z0:z0 + pw], preferred_element_type=F32)
        t_sc[:, psl] = u * _silu(z)

    mu = s1 * (1.0 / GMLP_WIDTH)
    s2 = jnp.zeros((tm, 1), F32)
    for g in range(GMLP_GROUPS):
        dv = v_sc[:, g * gw:(g + 1) * gw] - mu
        s2 = s2 + jnp.sum(dv * dv, axis=-1, keepdims=True)
    rstd = lax.rsqrt(s2 * (1.0 / GMLP_WIDTH) + LN_EPS)

    row = lax.broadcasted_iota(jnp.int32, (GMLP_CHUNK, GMLP_CHUNK), 0)
    col = lax.broadcasted_iota(jnp.int32, (GMLP_CHUNK, GMLP_CHUNK), 1)
    causal = col <= row
    for g in range(GMLP_GROUPS):
        sl = slice(g * gw, (g + 1) * gw)
        vn = ((v_sc[:, sl] - mu) * rstd * lng_ref[:, sl] + lnb_ref[:, sl]).astype(BF16)
        wm = jnp.where(causal, ws_ref[g], 0.0).astype(BF16)
        bias = bs_ref[:, g:g + 1]
        sv = jnp.concatenate(
            [jnp.dot(wm, vn[c * GMLP_CHUNK:(c + 1) * GMLP_CHUNK], preferred_element_type=F32) + bias
             for c in range(tm // GMLP_CHUNK)], axis=0)
        y_sc[:, sl] = (t_sc[:, sl] * sv).astype(BF16)

    out = jnp.dot(y_sc[...], wout_ref[...], preferred_element_type=F32)
    o_ref[...] = x_ref[...] + _rms_normalize(out, gpost_ref[...])


def _gmlp_layer(x2, gpre, gpost, w_in, ln_g, ln_b, w_s, b_s, w_out):
    n_tok = x2.shape[0]
    tm = GMLP_TOKENS
    const = lambda i: (0, 0)
    return pl.pallas_call(
        _gmlp_kernel,
        grid=(n_tok // tm,),
        in_specs=[
            pl.BlockSpec((tm, D_MODEL), lambda i: (i, 0)),
            pl.BlockSpec((1, D_MODEL), const),
            pl.BlockSpec((1, D_MODEL), const),
            pl.BlockSpec((D_MODEL, 3 * GMLP_WIDTH), const, pipeline_mode=pl.Buffered(1)),
            pl.BlockSpec((1, GMLP_WIDTH), const),
            pl.BlockSpec((1, GMLP_WIDTH), const),
            pl.BlockSpec((GMLP_GROUPS, GMLP_CHUNK, GMLP_CHUNK), lambda i: (0, 0, 0)),
            pl.BlockSpec((GMLP_CHUNK, GMLP_GROUPS), const),
            pl.BlockSpec((GMLP_WIDTH, D_MODEL), const, pipeline_mode=pl.Buffered(1)),
        ],
        out_specs=pl.BlockSpec((tm, D_MODEL), lambda i: (i, 0)),
        out_shape=jax.ShapeDtypeStruct(x2.shape, F32),
        scratch_shapes=[
            pltpu.VMEM((tm, D_MODEL), BF16),
            pltpu.VMEM((tm, GMLP_WIDTH), F32),
            pltpu.VMEM((tm, GMLP_WIDTH), F32),
            pltpu.VMEM((tm, GMLP_WIDTH), BF16),
        ],
        compiler_params=pltpu.CompilerParams(
            dimension_semantics=("arbitrary",), vmem_limit_bytes=VMEM_LIMIT),
        name="gmlp_layer",
    )(x2, gpre.reshape(1, -1), gpost.reshape(1, -1), w_in,
      ln_g.reshape(1, -1), ln_b.reshape(1, -1), w_s, jnp.transpose(b_s), w_out)


def _fill_permuted(hs_sc, cos_ref, sin_ref, dst, dilation, row_classes):
    h_sc, cos_sc, sin_sc = dst
    rows = TILE // dilation
    for r in row_classes:
        out = slice(r * rows, (r + 1) * rows)
        src = slice(None) if dilation == 1 else pl.ds(r, rows, stride=dilation)
        for c in range(D_MODEL // LANES):
            h_sc[out, c * LANES:(c + 1) * LANES] = hs_sc[c, src, :].astype(BF16)
        cos_sc[out, :] = cos_ref[src, :]
        sin_sc[out, :] = sin_ref[src, :]


def _proj_kernel(x_ref, gpre_ref, w_ref, cos_ref, sin_ref, o_ref, hs_sc,
                 h0_sc, cos0_sc, sin0_sc, h1_sc, cos1_sc, sin1_sc):
    j = pl.program_id(1)
    n_chunks = TILE // PROJ_ROW_CHUNK
    first1, first2 = GROUP_FIRST_SLAB[1], GROUP_FIRST_SLAB[2]
    bufs = ((h0_sc, cos0_sc, sin0_sc), (h1_sc, cos1_sc, sin1_sc))
    in_buf1 = jnp.logical_and(j >= first1, j < first2)

    @pl.when(j == 0)
    def _():
        for rc in range(n_chunks):
            rows = slice(rc * PROJ_ROW_CHUNK, (rc + 1) * PROJ_ROW_CHUNK)
            hs = _rms_normalize(x_ref[rows, :], gpre_ref[...])
            for c in range(D_MODEL // LANES):
                hs_sc[c, rows, :] = hs[:, c * LANES:(c + 1) * LANES]
        _fill_permuted(hs_sc, cos_ref, sin_ref, bufs[0], 1, range(1))

    is_q = functools.reduce(jnp.logical_or, [j == s for s in GROUP_FIRST_SLAB])
    is_k = functools.reduce(jnp.logical_or, [j == s + 1 for s in GROUP_FIRST_SLAB])
    is_rope = jnp.logical_or(is_q, is_k)

    def rope_slab(src):
        h_sc, cos_sc, sin_sc = src
        scale = jnp.where(is_q, np.float32(np.log2(np.e) / np.sqrt(HEAD_DIM)), np.float32(1.0))
        for rc in range(n_chunks):
            rows = slice(rc * PROJ_ROW_CHUNK, (rc + 1) * PROJ_ROW_CHUNK)
            p = jnp.dot(h_sc[rows, :], w_ref[...], preferred_element_type=F32)
            cos = cos_sc[rows, :] * scale
            sin = sin_sc[rows, :] * scale
            for s in range(ATTN_WIDTH // LANES):
                t = p[:, s * LANES:(s + 1) * LANES]
                o_ref[rows, s * LANES:(s + 1) * LANES] = (
                    t * cos + pltpu.roll(t, LANES // 2, 1) * sin).astype(BF16)

    def plain_slab(src, fill=None):
        for rc in range(n_chunks):
            rows = slice(rc * PROJ_ROW_CHUNK, (rc + 1) * PROJ_ROW_CHUNK)
            o_ref[rows, :] = jnp.dot(src[0][rows, :], w_ref[...], preferred_element_type=F32).astype(BF16)
            if fill is not None:
                dst, dilation = fill
                per_chunk = dilation // n_chunks
                _fill_permuted(hs_sc, cos_ref, sin_ref, dst, dilation,
                               range(rc * per_chunk, (rc + 1) * per_chunk))

    pl.when(jnp.logical_and(is_rope, jnp.logical_not(in_buf1)))(functools.partial(rope_slab, bufs[0]))
    pl.when(jnp.logical_and(is_rope, in_buf1))(functools.partial(rope_slab, bufs[1]))
    fill_steps = {first1 - 1: (bufs[0], (bufs[1], DILATED_GROUPS[1][1])),
                  first2 - 1: (bufs[1], (bufs[0], DILATED_GROUPS[2][1]))}
    for step, (src, fill) in fill_steps.items():
        pl.when(j == step)(functools.partial(plain_slab, src, fill))
    is_fill_step = functools.reduce(jnp.logical_or, [j == step for step in fill_steps])
    pl.when(jnp.logical_not(jnp.logical_or(is_rope, is_fill_step)))(functools.partial(plain_slab, bufs[0]))


def _proj_weight_kernel(w_ref, o_ref):
    s = pl.program_id(0)
    is_qk = functools.reduce(jnp.logical_or, [jnp.logical_or(s == f, s == f + 1) for f in GROUP_FIRST_SLAB])

    @pl.when(is_qk)
    def _():
        half = HEAD_DIM // 2
        quarter = lax.broadcasted_iota(jnp.int32, (1, LANES), 1) // half
        for c in range(ATTN_WIDTH // LANES):
            sl = slice(c * LANES, (c + 1) * LANES)
            w = w_ref[0, :, sl]
            w = jnp.where(quarter == 1, pltpu.roll(w, LANES - half, 1),
                          jnp.where(quarter == 2, pltpu.roll(w, half, 1), w))
            o_ref[:, sl] = w.astype(BF16)

    @pl.when(jnp.logical_not(is_qk))
    def _():
        o_ref[...] = w_ref[0].astype(BF16)


def _proj_weight(w_in_all, layer):
    def source_slab(s):
        return jnp.where(s == GATE_OUT_SLAB, Z_SLAB, jnp.where(s < GATE_OUT_SLAB, s, s - 1))

    return pl.pallas_call(
        _proj_weight_kernel,
        grid=(N_PROJ_SLABS,),
        in_specs=[pl.BlockSpec((1, D_MODEL, ATTN_WIDTH), lambda s: (layer, 0, source_slab(s)))],
        out_specs=pl.BlockSpec((D_MODEL, ATTN_WIDTH), lambda s: (0, s)),
        out_shape=jax.ShapeDtypeStruct((D_MODEL, N_PROJ_SLABS * ATTN_WIDTH), BF16),
        compiler_params=pltpu.CompilerParams(
            dimension_semantics=("arbitrary",), vmem_limit_bytes=VMEM_LIMIT),
        name="proj_weight_prep",
    )(w_in_all)


def _cast_kernel(w_ref, o_ref):
    o_ref[...] = w_ref[0].astype(BF16)


def _bf16_weight(w_all, layer):
    _, rows, cols = w_all.shape
    block_rows = rows
    while block_rows * cols * 4 > WEIGHT_CAST_BLOCK_BYTES:
        block_rows //= 2
    assert rows % block_rows == 0 and block_rows % 16 == 0
    return pl.pallas_call(
        _cast_kernel,
        grid=(rows // block_rows,),
        in_specs=[pl.BlockSpec((1, block_rows, cols), lambda i: (layer, i, 0))],
        out_specs=pl.BlockSpec((block_rows, cols), lambda i: (i, 0)),
        out_shape=jax.ShapeDtypeStruct((rows, cols), BF16),
        compiler_params=pltpu.CompilerParams(
            dimension_semantics=("arbitrary",), vmem_limit_bytes=VMEM_LIMIT),
        name="weight_cast",
    )(w_all)


def _proj_layer(x2, gpre, w_in, cos2, sin2, seq_len):
    n_tok = x2.shape[0]
    tiles_per_seq = seq_len // TILE
    return pl.pallas_call(
        _proj_kernel,
        grid=(n_tok // TILE, N_PROJ_SLABS),
        in_specs=[
            pl.BlockSpec((TILE, D_MODEL), lambda i, j: (i, 0)),
            pl.BlockSpec((1, D_MODEL), lambda i, j: (0, 0)),
            pl.BlockSpec((D_MODEL, ATTN_WIDTH), lambda i, j: (0, j)),
            pl.BlockSpec((TILE, LANES), lambda i, j: (i % tiles_per_seq, 0)),
            pl.BlockSpec((TILE, LANES), lambda i, j: (i % tiles_per_seq, 0)),
        ],
        out_specs=pl.BlockSpec((TILE, ATTN_WIDTH), lambda i, j: (i, j)),
        out_shape=jax.ShapeDtypeStruct((n_tok, N_PROJ_SLABS * ATTN_WIDTH), BF16),
        scratch_shapes=[
            pltpu.VMEM((D_MODEL // LANES, TILE, LANES), F32),
        ] + 2 * [
            pltpu.VMEM((TILE, D_MODEL), BF16),
            pltpu.VMEM((TILE, LANES), F32),
            pltpu.VMEM((TILE, LANES), F32),
        ],
        compiler_params=pltpu.CompilerParams(
            dimension_semantics=("arbitrary", "arbitrary"), vmem_limit_bytes=VMEM_LIMIT),
        name="attn_proj",
    )(x2, gpre.reshape(1, -1), w_in, cos2, sin2)


def _attn_masks():
    t = ATTN_BLOCK
    kj = np.arange(2 * t)[:, None]
    qc = np.arange(t)[None, :]
    cur = (kj >= t) & (kj - t <= qc)
    prev = (kj < t) & (kj >= qc)
    valid = np.stack([cur, cur | prev])
    return jnp.asarray(np.where(valid, 0.0, NEG_BIG), dtype=BF16)


def _attn_kernel(q_ref, kp_ref, kc_ref, vp_ref, vc_ref, mask_ref, eye_ref, o_ref, st_ref, *, chained):
    t = ATTN_BLOCK
    first_step_mask = mask_ref[jnp.minimum(pl.program_id(2), 1)]
    eye = eye_ref[...]
    ones = jnp.ones((2 * t, LANES), BF16)
    lane = lax.broadcasted_iota(jnp.int32, (1, LANES), 1)
    head_a_qk = ((lane // (HEAD_DIM // 2)) % 2) == 0
    head_a_v = (lane // HEAD_DIM) == 0
    for jb in range(q_ref.shape[0] // t):
        rows = slice(jb * t, (jb + 1) * t)
        if chained and jb > 0:
            prev_k, prev_v, prev_rows, mask_t = kc_ref, vc_ref, slice((jb - 1) * t, jb * t), mask_ref[1]
        else:
            prev_k, prev_v, mask_t = kp_ref, vp_ref, first_step_mask
            prev_rows = slice(0, t) if chained else rows
        st = jnp.zeros((t, LANES), F32)
        for hp in range(N_HEADS // 2):
            sl = slice(hp * LANES, (hp + 1) * LANES)
            q2 = q_ref[rows, sl]
            zero = jnp.zeros_like(q2)
            lhs = jnp.concatenate([
                jnp.concatenate([jnp.where(head_a_qk, q2, zero), eye], axis=1),
                jnp.concatenate([jnp.where(head_a_qk, zero, q2), eye], axis=1)], axis=0)
            k_ext = jnp.concatenate(
                [jnp.concatenate([prev_k[prev_rows, sl], kc_ref[rows, sl]], axis=0), mask_t], axis=1)
            v_ext = jnp.concatenate(
                [jnp.concatenate([prev_v[prev_rows, sl], vc_ref[rows, sl]], axis=0), ones], axis=1)
            s = lax.dot_general(lhs, k_ext, (((1,), (1,)), ((), ())), preferred_element_type=F32)
            m = jnp.max(s, axis=-1, keepdims=True)
            p = jnp.exp2(s - m).astype(BF16)
            pv = jnp.dot(p, v_ext, preferred_element_type=F32)
            o_ref[rows, sl] = jnp.where(head_a_v, pv[:t, :LANES], pv[t:, :LANES]).astype(BF16)
            denom = pv[:, LANES:]
            st = jnp.where(lane == 2 * hp, m[:t], st)
            st = jnp.where(lane == 2 * hp + 1, m[t:], st)
            st = jnp.where(lane == N_HEADS + 2 * hp, denom[:t], st)
            st = jnp.where(lane == N_HEADS + 2 * hp + 1, denom[t:], st)
        st_ref[rows, :] = st


def _attn_group(proj2, g, dilation, batch, seq_len):
    n_tok = batch * seq_len
    t, qb = ATTN_BLOCK, ATTN_QB[g]
    n_blocks = seq_len // dilation // t
    tiles_per_seq = seq_len // TILE
    blocks_per_tile = TILE // t
    bpc = blocks_per_tile // dilation
    first = GROUP_FIRST_SLAB[g]
    chained = bpc >= qb
    if chained:
        assert bpc % qb == 0
        grid = (batch, dilation, n_blocks // qb)

        def row_block(b, r, n):
            return (b * tiles_per_seq + n // bpc) * blocks_per_tile + r * bpc + n % bpc

        cur = lambda b, r, n: row_block(b, r, qb * n) // qb
        prev = lambda b, r, n: row_block(b, r, jnp.maximum(qb * n - 1, 0))
        prev_rows = t
    else:
        assert bpc == 1 and dilation % qb == 0
        grid = (batch, dilation // qb, n_blocks)
        cur = lambda b, r, n: (b * tiles_per_seq + n) * (blocks_per_tile // qb) + r
        prev = lambda b, r, n: cur(b, r, jnp.maximum(n - 1, 0))
        prev_rows = qb * t

    def spec(c, is_prev):
        if is_prev:
            return pl.BlockSpec((prev_rows, ATTN_WIDTH), lambda b, r, n: (prev(b, r, n), first + c))
        return pl.BlockSpec((qb * t, ATTN_WIDTH), lambda b, r, n: (cur(b, r, n), first + c))

    return pl.pallas_call(
        functools.partial(_attn_kernel, chained=chained),
        grid=grid,
        in_specs=[spec(0, False), spec(1, True), spec(1, False), spec(2, True), spec(2, False),
                  pl.BlockSpec((2, 2 * t, t), lambda b, r, n: (0, 0, 0)),
                  pl.BlockSpec((t, t), lambda b, r, n: (0, 0))],
        out_specs=[
            pl.BlockSpec((qb * t, ATTN_WIDTH), lambda b, r, n: (cur(b, r, n), 0)),
            pl.BlockSpec((qb * t, LANES), lambda b, r, n: (cur(b, r, n), 0)),
        ],
        out_shape=[
            jax.ShapeDtypeStruct((n_tok, ATTN_WIDTH), BF16),
            jax.ShapeDtypeStruct((n_tok, LANES), F32),
        ],
        compiler_params=pltpu.CompilerParams(
            dimension_semantics=("arbitrary", "arbitrary", "arbitrary"), vmem_limit_bytes=VMEM_LIMIT),
        name=f"dilated_attn_g{g}",
    )(proj2, proj2, proj2, proj2, proj2, _attn_masks(), jnp.eye(t, dtype=BF16))


def _combine_kernel(x_ref, o0_ref, o1_ref, o2_ref, s0_ref, s1_ref, s2_ref, z_ref,
                    expand_ref, wout_ref, gpost_ref, out_ref, on1_sc, on2_sc, st1_sc, st2_sc, y_sc):
    n_slabs = ATTN_WIDTH // LANES
    for o_ref, s_ref, on_sc, st_sc, (_, dilation) in (
            (o1_ref, s1_ref, on1_sc, st1_sc, DILATED_GROUPS[1]),
            (o2_ref, s2_ref, on2_sc, st2_sc, DILATED_GROUPS[2])):
        rows = COMBINE_TOKENS // dilation
        for r in range(dilation):
            dst = pl.ds(r, rows, stride=dilation)
            st_sc[dst, :] = s_ref[0, r]
            for c in range(n_slabs):
                on_sc[c, dst, :] = o_ref[0, r, :, c * LANES:(c + 1) * LANES].astype(F32)

    sts = [s0_ref[...], st1_sc[...], st2_sc[...]]
    m = jnp.maximum(jnp.maximum(sts[0], sts[1]), sts[2])
    es = [jnp.exp2(s - m) for s in sts]
    total = None
    for e, s in zip(es, sts):
        term = e * pltpu.roll(s, LANES - N_HEADS, 1)
        total = term if total is None else total + term
    is_head_lane = lax.broadcasted_iota(jnp.int32, (1, LANES), 1) < N_HEADS
    inv = 1.0 / jnp.where(is_head_lane, total, 1.0)
    alphas = []
    for e in es:
        alpha = jnp.where(is_head_lane, e * inv, 0.0)
        hi = alpha.astype(BF16)
        lo = (alpha - hi.astype(F32)).astype(BF16)
        alphas.append(jnp.dot(jnp.concatenate([hi, lo], axis=1), expand_ref[...],
                              preferred_element_type=F32))
    for c in range(n_slabs):
        sl = slice(c * LANES, (c + 1) * LANES)
        o = (alphas[0][:, sl] * o0_ref[:, sl].astype(F32)
             + alphas[1][:, sl] * on1_sc[c] + alphas[2][:, sl] * on2_sc[c])
        y_sc[:, sl] = (o * _silu(z_ref[:, sl].astype(F32))).astype(BF16)
    out = jnp.dot(y_sc[...], wout_ref[...], preferred_element_type=F32)
    out_ref[...] = x_ref[...] + _rms_normalize(out, gpost_ref[...])


def _combine_layer(x2, outs, stats, proj2, w_out, gpost):
    n_tok = x2.shape[0]
    tm = COMBINE_TOKENS
    n_tiles = n_tok // TILE
    per_tile = TILE // tm
    head_of_lane = np.arange(ATTN_WIDTH) // HEAD_DIM
    expand = jnp.asarray(np.tile(np.arange(LANES)[:, None] == head_of_lane[None, :], (2, 1)), dtype=BF16)
    tok = lambda i: (i, 0)
    const = lambda i: (0, 0)
    in_tile = lambda i: (i // per_tile, 0, i % per_tile, 0)
    d1, d2 = DILATED_GROUPS[1][1], DILATED_GROUPS[2][1]
    return pl.pallas_call(
        _combine_kernel,
        grid=(n_tok // tm,),
        in_specs=[
            pl.BlockSpec((tm, D_MODEL), tok),
            pl.BlockSpec((tm, ATTN_WIDTH), tok),
            pl.BlockSpec((1, d1, tm // d1, ATTN_WIDTH), in_tile),
            pl.BlockSpec((1, d2, tm // d2, ATTN_WIDTH), in_tile),
            pl.BlockSpec((tm, LANES), tok),
            pl.BlockSpec((1, d1, tm // d1, LANES), in_tile),
            pl.BlockSpec((1, d2, tm // d2, LANES), in_tile),
            pl.BlockSpec((tm, ATTN_WIDTH), lambda i: (i, GATE_OUT_SLAB)),
            pl.BlockSpec((2 * LANES, ATTN_WIDTH), const),
            pl.BlockSpec((ATTN_WIDTH, D_MODEL), const),
            pl.BlockSpec((1, D_MODEL), const),
        ],
        out_specs=pl.BlockSpec((tm, D_MODEL), tok),
        out_shape=jax.ShapeDtypeStruct(x2.shape, F32),
        scratch_shapes=[
            pltpu.VMEM((ATTN_WIDTH // LANES, tm, LANES), F32),
            pltpu.VMEM((ATTN_WIDTH // LANES, tm, LANES), F32),
            pltpu.VMEM((tm, LANES), F32),
            pltpu.VMEM((tm, LANES), F32),
            pltpu.VMEM((tm, ATTN_WIDTH), BF16),
        ],
        compiler_params=pltpu.CompilerParams(
            dimension_semantics=("arbitrary",), vmem_limit_bytes=VMEM_LIMIT),
        name="attn_combine",
    )(x2,
      outs[0],
      outs[1].reshape(n_tiles, d1, TILE // d1, ATTN_WIDTH),
      outs[2].reshape(n_tiles, d2, TILE // d2, ATTN_WIDTH),
      stats[0],
      stats[1].reshape(n_tiles, d1, TILE // d1, LANES),
      stats[2].reshape(n_tiles, d2, TILE // d2, LANES),
      proj2, expand, w_out, gpost.reshape(1, -1))


def _rope_tables(seq_len):
    half = HEAD_DIM // 2
    exponent = np.arange(0, HEAD_DIM, 2, dtype=np.float32) / np.float32(HEAD_DIM)
    power = np.power(np.float64(ROPE_THETA), exponent.astype(np.float64)).astype(np.float32)
    inv_freq = np.float32(1.0) / power
    ang = np.arange(seq_len, dtype=np.float32)[:, None] * inv_freq[None, :]
    cos = np.cos(ang.astype(np.float64)).astype(np.float32)
    sin = np.sin(ang.astype(np.float64)).astype(np.float32)
    cos2 = np.tile(cos, (1, LANES // half))
    sin2 = np.concatenate([-sin, -sin, sin, sin], axis=1)
    return jnp.asarray(cos2), jnp.asarray(sin2)


def kernel(x, norm_pre, norm_post, a_w_in, a_ln_g, a_ln_b, a_w_s, a_b_s, a_w_out, b_w_in, b_w_out):
    batch, seq_len, d = x.shape
    depth = norm_pre.shape[0]
    cos2, sin2 = _rope_tables(seq_len)
    x2 = x.reshape(batch * seq_len, d)
    for i in range(depth):
        j = i // 2
        if i % 2 == 0:
            x2 = _gmlp_layer(x2, norm_pre[i], norm_post[i], _bf16_weight(a_w_in, j), a_ln_g[j], a_ln_b[j],
                             a_w_s[j], a_b_s[j], _bf16_weight(a_w_out, j))
        else:
            proj2 = _proj_layer(x2, norm_pre[i], _proj_weight(b_w_in, j), cos2, sin2, seq_len)
            outs, stats = [], []
            for g, (_, dilation) in enumerate(DILATED_GROUPS):
                o, st = _attn_group(proj2, g, dilation, batch, seq_len)
                outs.append(o)
                stats.append(st)
            x2 = _combine_layer(x2, outs, stats, proj2, _bf16_weight(b_w_out, j), norm_post[i])
    return x2.reshape(batch, seq_len, d)
```

```python
import functools

import numpy as np
import jax
import jax.numpy as jnp
from jax import lax
from jax.experimental import pallas as pl
from jax.experimental.pallas import tpu as pltpu

F32 = jnp.float32
BF16 = jnp.bfloat16

D_MODEL = 1024
RMS_EPS = 1e-6
LN_EPS = 1e-5

GMLP_WIDTH = 2048
GMLP_CHUNK = 128
GMLP_GROUPS = 8
GMLP_GROUP_WIDTH = GMLP_WIDTH // GMLP_GROUPS
GMLP_DOT_GROUPS = 4

HEAD_DIM = 64
N_HEADS = 16
ATTN_WIDTH = N_HEADS * HEAD_DIM
DILATED_GROUPS = ((128, 1), (512, 4), (2048, 16))
N_DIL_GROUPS = len(DILATED_GROUPS)
ATTN_BLOCK = 128
ATTN_QB = (8, 4, 8)
ROPE_THETA = 10000.0
N_PROJ_SLABS = 3 * N_DIL_GROUPS + 1
Z_SLAB = 3 * N_DIL_GROUPS
GROUP_FIRST_SLAB = (0, 4, 7)
GATE_OUT_SLAB = 3
TILE = 2048

LANES = 128
NEG_BIG = -1e30

GMLP_TOKENS = 512
PROJ_ROW_CHUNK = 512
COMBINE_TOKENS = 512
VMEM_LIMIT = 60 * 1024 * 1024
WEIGHT_CAST_BLOCK_BYTES = 4 * 1024 * 1024


def _gelu(x):
    return 0.5 * x * (1.0 + lax.erf(x * np.float32(np.sqrt(0.5))))


def _silu(x):
    return x * jax.nn.sigmoid(x)


def _rms_normalize(x, gain):
    ms = jnp.mean(x * x, axis=-1, keepdims=True)
    return x * lax.rsqrt(ms + RMS_EPS) * gain


def _gmlp_kernel(x_ref, gpre_ref, gpost_ref, win_ref, lng_ref, lnb_ref, ws_ref,
                 bs_ref, wout_ref, o_ref, h_sc, v_sc, t_sc, y_sc):
    tm = x_ref.shape[0]
    gw = GMLP_GROUP_WIDTH
    x = x_ref[...]
    h_sc[...] = _rms_normalize(x, gpre_ref[...]).astype(BF16)

    pw = GMLP_DOT_GROUPS * gw
    s1 = jnp.zeros((tm, 1), F32)
    for gp in range(GMLP_WIDTH // pw):
        c0 = GMLP_WIDTH + gp * pw
        v = _gelu(jnp.dot(h_sc[...], win_ref[:, c0:c0 + pw], preferred_element_type=F32))
        v_sc[:, gp * pw:(gp + 1) * pw] = v
        s1 = s1 + jnp.sum(v, axis=-1, keepdims=True)
    for gp in range(GMLP_WIDTH // pw):
        psl = slice(gp * pw, (gp + 1) * pw)
        u = _gelu(jnp.dot(h_sc[...], win_ref[:, psl], preferred_element_type=F32))
        z0 = 2 * GMLP_WIDTH + gp * pw
        z = jnp.dot(h_sc[...], win_ref[:, z0:z0 + pw], preferred_element_type=F32)
        t_sc[:, psl] = u * _silu(z)

    mu = s1 * (1.0 / GMLP_WIDTH)
    s2 = jnp.zeros((tm, 1), F32)
    for g in range(GMLP_GROUPS):
        dv = v_sc[:, g * gw:(g + 1) * gw] - mu
        s2 = s2 + jnp.sum(dv * dv, axis=-1, keepdims=True)
    rstd = lax.rsqrt(s2 * (1.0 / GMLP_WIDTH) + LN_EPS)

    row = lax.broadcasted_iota(jnp.int32, (GMLP_CHUNK, GMLP_CHUNK), 0)
    col = lax.broadcasted_iota(jnp.int32, (GMLP_CHUNK, GMLP_CHUNK), 1)
    causal = col <= row
    for g in range(GMLP_GROUPS):
        sl = slice(g * gw, (g + 1) * gw)
        vn = ((v_sc[:, sl] - mu) * rstd * lng_ref[:, sl] + lnb_ref[:, sl]).astype(BF16)
        wm = jnp.where(causal, ws_ref[g], 0.0).astype(BF16)
        bias = bs_ref[:, g:g + 1]
        sv = jnp.concatenate(
            [jnp.dot(wm, vn[c * GMLP_CHUNK:(c + 1) * GMLP_CHUNK], preferred_element_type=F32) + bias
             for c in range(tm // GMLP_CHUNK)], axis=0)
        y_sc[:, sl] = (t_sc[:, sl] * sv).astype(BF16)

    out = jnp.dot(y_sc[...], wout_ref[...], preferred_element_type=F32)
    o_ref[...] = x_ref[...] + _rms_normalize(out, gpost_ref[...])


def _gmlp_layer(x2, gpre, gpost, w_in, ln_g, ln_b, w_s, b_s, w_out):
    n_tok = x2.shape[0]
    tm = GMLP_TOKENS
    const = lambda i: (0, 0)
    return pl.pallas_call(
        _gmlp_kernel,
        grid=(n_tok // tm,),
        in_specs=[
            pl.BlockSpec((tm, D_MODEL), lambda i: (i, 0)),
            pl.BlockSpec((1, D_MODEL), const),
            pl.BlockSpec((1, D_MODEL), const),
            pl.BlockSpec((D_MODEL, 3 * GMLP_WIDTH), const, pipeline_mode=pl.Buffered(1)),
            pl.BlockSpec((1, GMLP_WIDTH), const),
            pl.BlockSpec((1, GMLP_WIDTH), const),
            pl.BlockSpec((GMLP_GROUPS, GMLP_CHUNK, GMLP_CHUNK), lambda i: (0, 0, 0)),
            pl.BlockSpec((GMLP_CHUNK, GMLP_GROUPS), const),
            pl.BlockSpec((GMLP_WIDTH, D_MODEL), const, pipeline_mode=pl.Buffered(1)),
        ],
        out_specs=pl.BlockSpec((tm, D_MODEL), lambda i: (i, 0)),
        out_shape=jax.ShapeDtypeStruct(x2.shape, F32),
        scratch_shapes=[
            pltpu.VMEM((tm, D_MODEL), BF16),
            pltpu.VMEM((tm, GMLP_WIDTH), F32),
            pltpu.VMEM((tm, GMLP_WIDTH), F32),
            pltpu.VMEM((tm, GMLP_WIDTH), BF16),
        ],
        compiler_params=pltpu.CompilerParams(
            dimension_semantics=("arbitrary",), vmem_limit_bytes=VMEM_LIMIT),
        name="gmlp_layer",
    )(x2, gpre.reshape(1, -1), gpost.reshape(1, -1), w_in,
      ln_g.reshape(1, -1), ln_b.reshape(1, -1), w_s, jnp.transpose(b_s), w_out)


def _proj_kernel(x_ref, gpre_ref, w_ref, cos_ref, sin_ref, o_ref,
                 hs_sc, hs1_sc, h_sc, cos1_sc, sin1_sc, cos2_sc, sin2_sc):
    j = pl.program_id(1)
    n_chunks = TILE // PROJ_ROW_CHUNK
    n_slabs = D_MODEL // LANES
    first1, first2 = GROUP_FIRST_SLAB[1], GROUP_FIRST_SLAB[2]
    d1, d2 = DILATED_GROUPS[1][1], DILATED_GROUPS[2][1]
    ratio = d2 // d1

    @pl.when(j == 0)
    def _():
        for rc in range(n_chunks):
            rows = slice(rc * PROJ_ROW_CHUNK, (rc + 1) * PROJ_ROW_CHUNK)
            hs = _rms_normalize(x_ref[rows, :], gpre_ref[...])
            for c in range(n_slabs):
                hs_sc[c, rows, :] = hs[:, c * LANES:(c + 1) * LANES]

    @pl.when(j == first1)
    def _():
        rows = TILE // d1
        for b in range(d1):
            out, src = slice(b * rows, (b + 1) * rows), pl.ds(b, rows, stride=d1)
            for c in range(n_slabs):
                v = hs_sc[c, src, :]
                hs1_sc[c, out, :] = v
                h_sc[out, c * LANES:(c + 1) * LANES] = v.astype(BF16)
            cos1_sc[out, :] = cos_ref[src, :]
            sin1_sc[out, :] = sin_ref[src, :]

    @pl.when(j == first2)
    def _():
        rows = TILE // d2
        for b in range(d1):
            for a in range(ratio):
                r = a * d1 + b
                out, src = slice(r * rows, (r + 1) * rows), pl.ds(b * (TILE // d1) + a, rows, stride=ratio)
                for c in range(n_slabs):
                    h_sc[out, c * LANES:(c + 1) * LANES] = hs1_sc[c, src, :].astype(BF16)
                cos2_sc[out, :] = cos1_sc[src, :]
                sin2_sc[out, :] = sin1_sc[src, :]

    is_q = functools.reduce(jnp.logical_or, [j == s for s in GROUP_FIRST_SLAB])
    is_k = functools.reduce(jnp.logical_or, [j == s + 1 for s in GROUP_FIRST_SLAB])
    is_rope = jnp.logical_or(is_q, is_k)
    in_group0 = j < first1
    in_group1 = jnp.logical_and(j >= first1, j < first2)
    in_group2 = j >= first2

    def lhs_rows(from_token_order, rows):
        if from_token_order:
            return jnp.concatenate([hs_sc[c, rows, :] for c in range(n_slabs)], axis=1).astype(BF16)
        return h_sc[rows, :]

    def rope_slab(from_token_order, cos_tab, sin_tab):
        scale = jnp.where(is_q, np.float32(np.log2(np.e) / np.sqrt(HEAD_DIM)), np.float32(1.0))
        for rc in range(n_chunks):
            rows = slice(rc * PROJ_ROW_CHUNK, (rc + 1) * PROJ_ROW_CHUNK)
            p = jnp.dot(lhs_rows(from_token_order, rows), w_ref[...], preferred_element_type=F32)
            cos = cos_tab[rows, :] * scale
            sin = sin_tab[rows, :] * scale
            for s in range(ATTN_WIDTH // LANES):
                t = p[:, s * LANES:(s + 1) * LANES]
                o_ref[rows, s * LANES:(s + 1) * LANES] = (
                    t * cos + pltpu.roll(t, LANES // 2, 1) * sin).astype(BF16)

    def plain_slab(from_token_order):
        for rc in range(n_chunks):
            rows = slice(rc * PROJ_ROW_CHUNK, (rc + 1) * PROJ_ROW_CHUNK)
            o_ref[rows, :] = jnp.dot(lhs_rows(from_token_order, rows), w_ref[...],
                                     preferred_element_type=F32).astype(BF16)

    pl.when(jnp.logical_and(is_rope, in_group0))(functools.partial(rope_slab, True, cos_ref, sin_ref))
    pl.when(jnp.logical_and(is_rope, in_group1))(functools.partial(rope_slab, False, cos1_sc, sin1_sc))
    pl.when(jnp.logical_and(is_rope, in_group2))(functools.partial(rope_slab, False, cos2_sc, sin2_sc))
    is_plain = jnp.logical_not(is_rope)
    pl.when(jnp.logical_and(is_plain, in_group0))(functools.partial(plain_slab, True))
    pl.when(jnp.logical_and(is_plain, jnp.logical_not(in_group0)))(functools.partial(plain_slab, False))


def _proj_weight_kernel(w_ref, o_ref):
    s = pl.program_id(0)
    is_qk = functools.reduce(jnp.logical_or, [jnp.logical_or(s == f, s == f + 1) for f in GROUP_FIRST_SLAB])

    @pl.when(is_qk)
    def _():
        half = HEAD_DIM // 2
        quarter = lax.broadcasted_iota(jnp.int32, (1, LANES), 1) // half
        for c in range(ATTN_WIDTH // LANES):
            sl = slice(c * LANES, (c + 1) * LANES)
            w = w_ref[0, :, sl]
            w = jnp.where(quarter == 1, pltpu.roll(w, LANES - half, 1),
                          jnp.where(quarter == 2, pltpu.roll(w, half, 1), w))
            o_ref[:, sl] = w.astype(BF16)

    @pl.when(jnp.logical_not(is_qk))
    def _():
        o_ref[...] = w_ref[0].astype(BF16)


def _proj_weight(w_in_all, layer):
    def source_slab(s):
        return jnp.where(s == GATE_OUT_SLAB, Z_SLAB, jnp.where(s < GATE_OUT_SLAB, s, s - 1))

    return pl.pallas_call(
        _proj_weight_kernel,
        grid=(N_PROJ_SLABS,),
        in_specs=[pl.BlockSpec((1, D_MODEL, ATTN_WIDTH), lambda s: (layer, 0, source_slab(s)))],
        out_specs=pl.BlockSpec((D_MODEL, ATTN_WIDTH), lambda s: (0, s)),
        out_shape=jax.ShapeDtypeStruct((D_MODEL, N_PROJ_SLABS * ATTN_WIDTH), BF16),
        compiler_params=pltpu.CompilerParams(
            dimension_semantics=("arbitrary",), vmem_limit_bytes=VMEM_LIMIT),
        name="proj_weight_prep",
    )(w_in_all)


def _cast_kernel(w_ref, o_ref):
    o_ref[...] = w_ref[0].astype(BF16)


def _bf16_weight(w_all, layer):
    _, rows, cols = w_all.shape
    block_rows = rows
    while block_rows * cols * 4 > WEIGHT_CAST_BLOCK_BYTES:
        block_rows //= 2
    assert rows % block_rows == 0 and block_rows % 16 == 0
    return pl.pallas_call(
        _cast_kernel,
        grid=(rows // block_rows,),
        in_specs=[pl.BlockSpec((1, block_rows, cols), lambda i: (layer, i, 0))],
        out_specs=pl.BlockSpec((block_rows, cols), lambda i: (i, 0)),
        out_shape=jax.ShapeDtypeStruct((rows, cols), BF16),
        compiler_params=pltpu.CompilerParams(
            dimension_semantics=("arbitrary",), vmem_limit_bytes=VMEM_LIMIT),
        name="weight_cast",
    )(w_all)


def _proj_layer(x2, gpre, w_in, cos2, sin2, seq_len):
    n_tok = x2.shape[0]
    tiles_per_seq = seq_len // TILE
    return pl.pallas_call(
        _proj_kernel,
        grid=(n_tok // TILE, N_PROJ_SLABS),
        in_specs=[
            pl.BlockSpec((TILE, D_MODEL), lambda i, j: (i, 0)),
            pl.BlockSpec((1, D_MODEL), lambda i, j: (0, 0)),
            pl.BlockSpec((D_MODEL, ATTN_WIDTH), lambda i, j: (0, j)),
            pl.BlockSpec((TILE, LANES), lambda i, j: (i % tiles_per_seq, 0)),
            pl.BlockSpec((TILE, LANES), lambda i, j: (i % tiles_per_seq, 0)),
        ],
        out_specs=pl.BlockSpec((TILE, ATTN_WIDTH), lambda i, j: (i, j)),
        out_shape=jax.ShapeDtypeStruct((n_tok, N_PROJ_SLABS * ATTN_WIDTH), BF16),
        scratch_shapes=[
            pltpu.VMEM((D_MODEL // LANES, TILE, LANES), F32),
            pltpu.VMEM((D_MODEL // LANES, TILE, LANES), F32),
            pltpu.VMEM((TILE, D_MODEL), BF16),
        ] + 4 * [pltpu.VMEM((TILE, LANES), F32)],
        compiler_params=pltpu.CompilerParams(
            dimension_semantics=("arbitrary", "arbitrary"), vmem_limit_bytes=VMEM_LIMIT),
        name="attn_proj",
    )(x2, gpre.reshape(1, -1), w_in, cos2, sin2)


def _attn_masks():
    t = ATTN_BLOCK
    kj = np.arange(2 * t)[:, None]
    qc = np.arange(t)[None, :]
    cur = (kj >= t) & (kj - t <= qc)
    prev = (kj < t) & (kj >= qc)
    valid = np.stack([cur, cur | prev])
    return jnp.asarray(np.where(valid, 0.0, NEG_BIG), dtype=BF16)


def _attn_kernel(q_ref, kp_ref, kc_ref, vp_ref, vc_ref, mask_ref, eye_ref, o_ref, st_ref, *, chained):
    t = ATTN_BLOCK
    first_step_mask = mask_ref[jnp.minimum(pl.program_id(2), 1)]
    eye = eye_ref[...]
    ones = jnp.ones((2 * t, LANES), BF16)
    lane = lax.broadcasted_iota(jnp.int32, (1, LANES), 1)
    head_a_qk = ((lane // (HEAD_DIM // 2)) % 2) == 0
    head_a_v = (lane // HEAD_DIM) == 0
    for jb in range(q_ref.shape[0] // t):
        rows = slice(jb * t, (jb + 1) * t)
        if chained and jb > 0:
            prev_k, prev_v, prev_rows, mask_t = kc_ref, vc_ref, slice((jb - 1) * t, jb * t), mask_ref[1]
        else:
            prev_k, prev_v, mask_t = kp_ref, vp_ref, first_step_mask
            prev_rows = slice(0, t) if chained else rows
        st = jnp.zeros((t, LANES), F32)
        for hp in range(N_HEADS // 2):
            sl = slice(hp * LANES, (hp + 1) * LANES)
            q2 = q_ref[rows, sl]
            zero = jnp.zeros_like(q2)
            lhs = jnp.concatenate([
                jnp.concatenate([jnp.where(head_a_qk, q2, zero), eye], axis=1),
                jnp.concatenate([jnp.where(head_a_qk, zero, q2), eye], axis=1)], axis=0)
            k_ext = jnp.concatenate(
                [jnp.concatenate([prev_k[prev_rows, sl], kc_ref[rows, sl]], axis=0), mask_t], axis=1)
            v_ext = jnp.concatenate(
                [jnp.concatenate([prev_v[prev_rows, sl], vc_ref[rows, sl]], axis=0), ones], axis=1)
            s = lax.dot_general(lhs, k_ext, (((1,), (1,)), ((), ())), preferred_element_type=F32)
            m = jnp.max(s, axis=-1, keepdims=True)
            p = jnp.exp2(s - m).astype(BF16)
            pv = jnp.dot(p, v_ext, preferred_element_type=F32)
            o_ref[rows, sl] = jnp.where(head_a_v, pv[:t, :LANES], pv[t:, :LANES]).astype(BF16)
            denom = pv[:, LANES:]
            st = jnp.where(lane == 2 * hp, m[:t], st)
            st = jnp.where(lane == 2 * hp + 1, m[t:], st)
            st = jnp.where(lane == N_HEADS + 2 * hp, denom[:t], st)
            st = jnp.where(lane == N_HEADS + 2 * hp + 1, denom[t:], st)
        st_ref[rows, :] = st


def _attn_group(proj2, g, dilation, batch, seq_len):
    n_tok = batch * seq_len
    t, qb = ATTN_BLOCK, ATTN_QB[g]
    n_blocks = seq_len // dilation // t
    tiles_per_seq = seq_len // TILE
    blocks_per_tile = TILE // t
    bpc = blocks_per_tile // dilation
    first = GROUP_FIRST_SLAB[g]
    chained = bpc >= qb
    if chained:
        assert bpc % qb == 0
        grid = (batch, dilation, n_blocks // qb)

        def row_block(b, r, n):
            return (b * tiles_per_seq + n // bpc) * blocks_per_tile + r * bpc + n % bpc

        cur = lambda b, r, n: row_block(b, r, qb * n) // qb
        prev = lambda b, r, n: row_block(b, r, jnp.maximum(qb * n - 1, 0))
        prev_rows = t
    else:
        assert bpc == 1 and dilation % qb == 0
        grid = (batch, dilation // qb, n_blocks)
        cur = lambda b, r, n: (b * tiles_per_seq + n) * (blocks_per_tile // qb) + r
        prev = lambda b, r, n: cur(b, r, jnp.maximum(n - 1, 0))
        prev_rows = qb * t

    def spec(c, is_prev):
        if is_prev:
            return pl.BlockSpec((prev_rows, ATTN_WIDTH), lambda b, r, n: (prev(b, r, n), first + c))
        return pl.BlockSpec((qb * t, ATTN_WIDTH), lambda b, r, n: (cur(b, r, n), first + c))

    return pl.pallas_call(
        functools.partial(_attn_kernel, chained=chained),
        grid=grid,
        in_specs=[spec(0, False), spec(1, True), spec(1, False), spec(2, True), spec(2, False),
                  pl.BlockSpec((2, 2 * t, t), lambda b, r, n: (0, 0, 0)),
                  pl.BlockSpec((t, t), lambda b, r, n: (0, 0))],
        out_specs=[
            pl.BlockSpec((qb * t, ATTN_WIDTH), lambda b, r, n: (cur(b, r, n), 0)),
            pl.BlockSpec((qb * t, LANES), lambda b, r, n: (cur(b, r, n), 0)),
        ],
        out_shape=[
            jax.ShapeDtypeStruct((n_tok, ATTN_WIDTH), BF16),
            jax.ShapeDtypeStruct((n_tok, LANES), F32),
        ],
        compiler_params=pltpu.CompilerParams(
            dimension_semantics=("arbitrary", "arbitrary", "arbitrary"), vmem_limit_bytes=VMEM_LIMIT),
        name=f"dilated_attn_g{g}",
    )(proj2, proj2, proj2, proj2, proj2, _attn_masks(), jnp.eye(t, dtype=BF16))


def _combine_kernel(x_ref, o0_ref, o1_ref, o2_ref, s0_ref, s1_ref, s2_ref, z_ref,
                    expand_ref, wout_ref, gpost_ref, out_ref, on1_sc, on2_sc, st1_sc, st2_sc, y_sc):
    n_slabs = ATTN_WIDTH // LANES
    for o_ref, s_ref, on_sc, st_sc, (_, dilation) in (
            (o1_ref, s1_ref, on1_sc, st1_sc, DILATED_GROUPS[1]),
            (o2_ref, s2_ref, on2_sc, st2_sc, DILATED_GROUPS[2])):
        rows = COMBINE_TOKENS // dilation
        for r in range(dilation):
            dst = pl.ds(r, rows, stride=dilation)
            st_sc[dst, :] = s_ref[0, r]
            for c in range(n_slabs):
                on_sc[c, dst, :] = o_ref[0, r, :, c * LANES:(c + 1) * LANES].astype(F32)

    sts = [s0_ref[...], st1_sc[...], st2_sc[...]]
    m = jnp.maximum(jnp.maximum(sts[0], sts[1]), sts[2])
    es = [jnp.exp2(s - m) for s in sts]
    total = None
    for e, s in zip(es, sts):
        term = e * pltpu.roll(s, LANES - N_HEADS, 1)
        total = term if total is None else total + term
    is_head_lane = lax.broadcasted_iota(jnp.int32, (1, LANES), 1) < N_HEADS
    inv = 1.0 / jnp.where(is_head_lane, total, 1.0)
    alphas = []
    for e in es:
        alpha = jnp.where(is_head_lane, e * inv, 0.0)
        hi = alpha.astype(BF16)
        lo = (alpha - hi.astype(F32)).astype(BF16)
        alphas.append(jnp.dot(jnp.concatenate([hi, lo], axis=1), expand_ref[...],
                              preferred_element_type=F32))
    for c in range(n_slabs):
        sl = slice(c * LANES, (c + 1) * LANES)
        o = (alphas[0][:, sl] * o0_ref[:, sl].astype(F32)
             + alphas[1][:, sl] * on1_sc[c] + alphas[2][:, sl] * on2_sc[c])
        y_sc[:, sl] = (o * _silu(z_ref[:, sl].astype(F32))).astype(BF16)
    out = jnp.dot(y_sc[...], wout_ref[...], preferred_element_type=F32)
    out_ref[...] = x_ref[...] + _rms_normalize(out, gpost_ref[...])


def _combine_layer(x2, outs, stats, proj2, w_out, gpost):
    n_tok = x2.shape[0]
    tm = COMBINE_TOKENS
    n_tiles = n_tok // TILE
    per_tile = TILE // tm
    head_of_lane = np.arange(ATTN_WIDTH) // HEAD_DIM
    expand = jnp.asarray(np.tile(np.arange(LANES)[:, None] == head_of_lane[None, :], (2, 1)), dtype=BF16)
    tok = lambda i: (i, 0)
    const = lambda i: (0, 0)
    in_tile = lambda i: (i // per_tile, 0, i % per_tile, 0)
    d1, d2 = DILATED_GROUPS[1][1], DILATED_GROUPS[2][1]
    return pl.pallas_call(
        _combine_kernel,
        grid=(n_tok // tm,),
        in_specs=[
            pl.BlockSpec((tm, D_MODEL), tok),
            pl.BlockSpec((tm, ATTN_WIDTH), tok),
            pl.BlockSpec((1, d1, tm // d1, ATTN_WIDTH), in_tile),
            pl.BlockSpec((1, d2, tm // d2, ATTN_WIDTH), in_tile),
            pl.BlockSpec((tm, LANES), tok),
            pl.BlockSpec((1, d1, tm // d1, LANES), in_tile),
            pl.BlockSpec((1, d2, tm // d2, LANES), in_tile),
            pl.BlockSpec((tm, ATTN_WIDTH), lambda i: (i, GATE_OUT_SLAB)),
            pl.BlockSpec((2 * LANES, ATTN_WIDTH), const),
            pl.BlockSpec((ATTN_WIDTH, D_MODEL), const),
            pl.BlockSpec((1, D_MODEL), const),
        ],
        out_specs=pl.BlockSpec((tm, D_MODEL), tok),
        out_shape=jax.ShapeDtypeStruct(x2.shape, F32),
        scratch_shapes=[
            pltpu.VMEM((ATTN_WIDTH // LANES, tm, LANES), F32),
            pltpu.VMEM((ATTN_WIDTH // LANES, tm, LANES), F32),
            pltpu.VMEM((tm, LANES), F32),
            pltpu.VMEM((tm, LANES), F32),
            pltpu.VMEM((tm, ATTN_WIDTH), BF16),
        ],
        compiler_params=pltpu.CompilerParams(
            dimension_semantics=("arbitrary",), vmem_limit_bytes=VMEM_LIMIT),
        name="attn_combine",
    )(x2,
      outs[0],
      outs[1].reshape(n_tiles, d1, TILE // d1, ATTN_WIDTH),
      outs[2].reshape(n_tiles, d2, TILE // d2, ATTN_WIDTH),
      stats[0],
      stats[1].reshape(n_tiles, d1, TILE // d1, LANES),
      stats[2].reshape(n_tiles, d2, TILE // d2, LANES),
      proj2, expand, w_out, gpost.reshape(1, -1))


def _rope_tables(seq_len):
    half = HEAD_DIM // 2
    exponent = np.arange(0, HEAD_DIM, 2, dtype=np.float32) / np.float32(HEAD_DIM)
    power = np.power(np.float64(ROPE_THETA), exponent.astype(np.float64)).astype(np.float32)
    inv_freq = np.float32(1.0) / power
    ang = np.arange(seq_len, dtype=np.float32)[:, None] * inv_freq[None, :]
    cos = np.cos(ang.astype(np.float64)).astype(np.float32)
    sin = np.sin(ang.astype(np.float64)).astype(np.float32)
    cos2 = np.tile(cos, (1, LANES // half))
    sin2 = np.concatenate([-sin, -sin, sin, sin], axis=1)
    return jnp.asarray(cos2), jnp.asarray(sin2)


def kernel(x, norm_pre, norm_post, a_w_in, a_ln_g, a_ln_b, a_w_s, a_b_s, a_w_out, b_w_in, b_w_out):
    batch, seq_len, d = x.shape
    depth = norm_pre.shape[0]
    cos2, sin2 = _rope_tables(seq_len)
    x2 = x.reshape(batch * seq_len, d)
    for i in range(depth):
        j = i // 2
        if i % 2 == 0:
            x2 = _gmlp_layer(x2, norm_pre[i], norm_post[i], _bf16_weight(a_w_in, j), a_ln_g[j], a_ln_b[j],
                             a_w_s[j], a_b_s[j], _bf16_weight(a_w_out, j))
        else:
            proj2 = _proj_layer(x2, norm_pre[i], _proj_weight(b_w_in, j), cos2, sin2, seq_len)
            outs, stats = [], []
            for g, (_, dilation) in enumerate(DILATED_GROUPS):
                o, st = _attn_group(proj2, g, dilation, batch, seq_len)
                outs.append(o)
                stats.append(st)
            x2 = _combine_layer(x2, outs, stats, proj2, _bf16_weight(b_w_out, j), norm_post[i])
    return x2.reshape(batch, seq_len, d)
```

```python
import functools

import numpy as np
import jax
import jax.numpy as jnp
from jax import lax
from jax.experimental import pallas as pl
from jax.experimental.pallas import tpu as pltpu

F32 = jnp.float32
BF16 = jnp.bfloat16

D_MODEL = 1024
RMS_EPS = 1e-6
LN_EPS = 1e-5

GMLP_WIDTH = 2048
GMLP_CHUNK = 128
GMLP_GROUPS = 8
GMLP_GROUP_WIDTH = GMLP_WIDTH // GMLP_GROUPS
GMLP_DOT_GROUPS = 4

HEAD_DIM = 64
N_HEADS = 16
ATTN_WIDTH = N_HEADS * HEAD_DIM
DILATED_GROUPS = ((128, 1), (512, 4), (2048, 16))
N_DIL_GROUPS = len(DILATED_GROUPS)
ATTN_BLOCK = 128
ATTN_QB = (16, 4, 16)
ROPE_THETA = 10000.0
N_PROJ_SLABS = 3 * N_DIL_GROUPS + 1
Z_SLAB = 3 * N_DIL_GROUPS
GROUP_FIRST_SLAB = (0, 4, 7)
GATE_OUT_SLAB = 3
TILE = 2048

LANES = 128
NEG_BIG = -1e30

GMLP_TOKENS = 512
PROJ_ROW_CHUNK = 512
COMBINE_TOKENS = 1024
COMBINE_SUB = 512
VMEM_LIMIT = 60 * 1024 * 1024
WEIGHT_CAST_BLOCK_BYTES = 4 * 1024 * 1024


def _gelu(x):
    return 0.5 * x * (1.0 + lax.erf(x * np.float32(np.sqrt(0.5))))


def _silu(x):
    return x * jax.nn.sigmoid(x)


def _rms_normalize(x, gain):
    ms = jnp.mean(x * x, axis=-1, keepdims=True)
    return x * lax.rsqrt(ms + RMS_EPS) * gain


def _gmlp_kernel(x_ref, gpre_ref, gpost_ref, win_ref, lng_ref, lnb_ref, ws_ref,
                 bs_ref, wout_ref, o_ref, h_sc, v_sc, t_sc, y_sc):
    tm = x_ref.shape[0]
    gw = GMLP_GROUP_WIDTH
    x = x_ref[...]
    h_sc[...] = _rms_normalize(x, gpre_ref[...]).astype(BF16)

    pw = GMLP_DOT_GROUPS * gw
    s1 = jnp.zeros((tm, 1), F32)
    for gp in range(GMLP_WIDTH // pw):
        c0 = GMLP_WIDTH + gp * pw
        v = _gelu(jnp.dot(h_sc[...], win_ref[:, c0:c0 + pw], preferred_element_type=F32))
        v_sc[:, gp * pw:(gp + 1) * pw] = v
        s1 = s1 + jnp.sum(v, axis=-1, keepdims=True)
    for gp in range(GMLP_WIDTH // pw):
        psl = slice(gp * pw, (gp + 1) * pw)
        u = _gelu(jnp.dot(h_sc[...], win_ref[:, psl], preferred_element_type=F32))
        z0 = 2 * GMLP_WIDTH + gp * pw
        z = jnp.dot(h_sc[...], win_ref[:, z0:z0 + pw], preferred_element_type=F32)
        t_sc[:, psl] = u * _silu(z)

    mu = s1 * (1.0 / GMLP_WIDTH)
    s2 = jnp.zeros((tm, 1), F32)
    for g in range(GMLP_GROUPS):
        dv = v_sc[:, g * gw:(g + 1) * gw] - mu
        s2 = s2 + jnp.sum(dv * dv, axis=-1, keepdims=True)
    rstd = lax.rsqrt(s2 * (1.0 / GMLP_WIDTH) + LN_EPS)

    row = lax.broadcasted_iota(jnp.int32, (GMLP_CHUNK, GMLP_CHUNK), 0)
    col = lax.broadcasted_iota(jnp.int32, (GMLP_CHUNK, GMLP_CHUNK), 1)
    causal = col <= row
    for g in range(GMLP_GROUPS):
        sl = slice(g * gw, (g + 1) * gw)
        vn = ((v_sc[:, sl] - mu) * rstd * lng_ref[:, sl] + lnb_ref[:, sl]).astype(BF16)
        wm = jnp.where(causal, ws_ref[g], 0.0).astype(BF16)
        bias = bs_ref[:, g:g + 1]
        sv = jnp.concatenate(
            [jnp.dot(wm, vn[c * GMLP_CHUNK:(c + 1) * GMLP_CHUNK], preferred_element_type=F32) + bias
             for c in range(tm // GMLP_CHUNK)], axis=0)
        y_sc[:, sl] = (t_sc[:, sl] * sv).astype(BF16)

    out = jnp.dot(y_sc[...], wout_ref[...], preferred_element_type=F32)
    o_ref[...] = x_ref[...] + _rms_normalize(out, gpost_ref[...])


def _gmlp_layer(x2, gpre, gpost, w_in, ln_g, ln_b, w_s, b_s, w_out):
    n_tok = x2.shape[0]
    tm = GMLP_TOKENS
    const = lambda i: (0, 0)
    return pl.pallas_call(
        _gmlp_kernel,
        grid=(n_tok // tm,),
        in_specs=[
            pl.BlockSpec((tm, D_MODEL), lambda i: (i, 0)),
            pl.BlockSpec((1, D_MODEL), const),
            pl.BlockSpec((1, D_MODEL), const),
            pl.BlockSpec((D_MODEL, 3 * GMLP_WIDTH), const, pipeline_mode=pl.Buffered(1)),
            pl.BlockSpec((1, GMLP_WIDTH), const),
            pl.BlockSpec((1, GMLP_WIDTH), const),
            pl.BlockSpec((GMLP_GROUPS, GMLP_CHUNK, GMLP_CHUNK), lambda i: (0, 0, 0)),
            pl.BlockSpec((GMLP_CHUNK, GMLP_GROUPS), const),
            pl.BlockSpec((GMLP_WIDTH, D_MODEL), const, pipeline_mode=pl.Buffered(1)),
        ],
        out_specs=pl.BlockSpec((tm, D_MODEL), lambda i: (i, 0)),
        out_shape=jax.ShapeDtypeStruct(x2.shape, F32),
        scratch_shapes=[
            pltpu.VMEM((tm, D_MODEL), BF16),
            pltpu.VMEM((tm, GMLP_WIDTH), F32),
            pltpu.VMEM((tm, GMLP_WIDTH), F32),
            pltpu.VMEM((tm, GMLP_WIDTH), BF16),
        ],
        compiler_params=pltpu.CompilerParams(
            dimension_semantics=("arbitrary",), vmem_limit_bytes=VMEM_LIMIT),
        name="gmlp_layer",
    )(x2, gpre.reshape(1, -1), gpost.reshape(1, -1), w_in,
      ln_g.reshape(1, -1), ln_b.reshape(1, -1), w_s, jnp.transpose(b_s), w_out)


def _proj_kernel(x_ref, gpre_ref, w_ref, cos_ref, sin_ref, o_ref,
                 hs_sc, hs1_sc, h_sc, cos1_sc, sin1_sc, cos2_sc, sin2_sc):
    j = pl.program_id(1)
    n_chunks = TILE // PROJ_ROW_CHUNK
    n_slabs = D_MODEL // LANES
    first1, first2 = GROUP_FIRST_SLAB[1], GROUP_FIRST_SLAB[2]
    d1, d2 = DILATED_GROUPS[1][1], DILATED_GROUPS[2][1]
    ratio = d2 // d1

    @pl.when(j == 0)
    def _():
        for rc in range(n_chunks):
            rows = slice(rc * PROJ_ROW_CHUNK, (rc + 1) * PROJ_ROW_CHUNK)
            hs = _rms_normalize(x_ref[rows, :], gpre_ref[...])
            for c in range(n_slabs):
                hs_sc[c, rows, :] = hs[:, c * LANES:(c + 1) * LANES]

    @pl.when(j == first1)
    def _():
        rows = TILE // d1
        for b in range(d1):
            out, src = slice(b * rows, (b + 1) * rows), pl.ds(b, rows, stride=d1)
            for c in range(n_slabs):
                v = hs_sc[c, src, :]
                hs1_sc[c, out, :] = v
                h_sc[out, c * LANES:(c + 1) * LANES] = v.astype(BF16)
            cos1_sc[out, :] = cos_ref[src, :]
            sin1_sc[out, :] = sin_ref[src, :]

    @pl.when(j == first2)
    def _():
        rows = TILE // d2
        for b in range(d1):
            for a in range(ratio):
                r = a * d1 + b
                out, src = slice(r * rows, (r + 1) * rows), pl.ds(b * (TILE // d1) + a, rows, stride=ratio)
                for c in range(n_slabs):
                    h_sc[out, c * LANES:(c + 1) * LANES] = hs1_sc[c, src, :].astype(BF16)
                cos2_sc[out, :] = cos1_sc[src, :]
                sin2_sc[out, :] = sin1_sc[src, :]

    is_q = functools.reduce(jnp.logical_or, [j == s for s in GROUP_FIRST_SLAB])
    is_k = functools.reduce(jnp.logical_or, [j == s + 1 for s in GROUP_FIRST_SLAB])
    is_rope = jnp.logical_or(is_q, is_k)
    in_group0 = j < first1
    in_group1 = jnp.logical_and(j >= first1, j < first2)
    in_group2 = j >= first2

    def lhs_rows(from_token_order, rows):
        if from_token_order:
            return jnp.concatenate([hs_sc[c, rows, :] for c in range(n_slabs)], axis=1).astype(BF16)
        return h_sc[rows, :]

    def rope_slab(from_token_order, cos_tab, sin_tab):
        scale = jnp.where(is_q, np.float32(np.log2(np.e) / np.sqrt(HEAD_DIM)), np.float32(1.0))
        for rc in range(n_chunks):
            rows = slice(rc * PROJ_ROW_CHUNK, (rc + 1) * PROJ_ROW_CHUNK)
            p = jnp.dot(lhs_rows(from_token_order, rows), w_ref[...], preferred_element_type=F32)
            cos = cos_tab[rows, :] * scale
            sin = sin_tab[rows, :] * scale
            for s in range(ATTN_WIDTH // LANES):
                t = p[:, s * LANES:(s + 1) * LANES]
                o_ref[rows, s * LANES:(s + 1) * LANES] = (
                    t * cos + pltpu.roll(t, LANES // 2, 1) * sin).astype(BF16)

    def plain_slab(from_token_order):
        for rc in range(n_chunks):
            rows = slice(rc * PROJ_ROW_CHUNK, (rc + 1) * PROJ_ROW_CHUNK)
            o_ref[rows, :] = jnp.dot(lhs_rows(from_token_order, rows), w_ref[...],
                                     preferred_element_type=F32).astype(BF16)

    pl.when(jnp.logical_and(is_rope, in_group0))(functools.partial(rope_slab, True, cos_ref, sin_ref))
    pl.when(jnp.logical_and(is_rope, in_group1))(functools.partial(rope_slab, False, cos1_sc, sin1_sc))
    pl.when(jnp.logical_and(is_rope, in_group2))(functools.partial(rope_slab, False, cos2_sc, sin2_sc))
    is_plain = jnp.logical_not(is_rope)
    pl.when(jnp.logical_and(is_plain, in_group0))(functools.partial(plain_slab, True))
    pl.when(jnp.logical_and(is_plain, jnp.logical_not(in_group0)))(functools.partial(plain_slab, False))


def _proj_weight_kernel(w_ref, o_ref):
    s = pl.program_id(0)
    is_qk = functools.reduce(jnp.logical_or, [jnp.logical_or(s == f, s == f + 1) for f in GROUP_FIRST_SLAB])

    @pl.when(is_qk)
    def _():
        half = HEAD_DIM // 2
        quarter = lax.broadcasted_iota(jnp.int32, (1, LANES), 1) // half
        for c in range(ATTN_WIDTH // LANES):
            sl = slice(c * LANES, (c + 1) * LANES)
            w = w_ref[0, :, sl]
            w = jnp.where(quarter == 1, pltpu.roll(w, LANES - half, 1),
                          jnp.where(quarter == 2, pltpu.roll(w, half, 1), w))
            o_ref[:, sl] = w.astype(BF16)

    @pl.when(jnp.logical_not(is_qk))
    def _():
        o_ref[...] = w_ref[0].astype(BF16)


def _proj_weight(w_in_all, layer):
    def source_slab(s):
        return jnp.where(s == GATE_OUT_SLAB, Z_SLAB, jnp.where(s < GATE_OUT_SLAB, s, s - 1))

    return pl.pallas_call(
        _proj_weight_kernel,
        grid=(N_PROJ_SLABS,),
        in_specs=[pl.BlockSpec((1, D_MODEL, ATTN_WIDTH), lambda s: (layer, 0, source_slab(s)))],
        out_specs=pl.BlockSpec((D_MODEL, ATTN_WIDTH), lambda s: (0, s)),
        out_shape=jax.ShapeDtypeStruct((D_MODEL, N_PROJ_SLABS * ATTN_WIDTH), BF16),
        compiler_params=pltpu.CompilerParams(
            dimension_semantics=("arbitrary",), vmem_limit_bytes=VMEM_LIMIT),
        name="proj_weight_prep",
    )(w_in_all)


def _cast_kernel(w_ref, o_ref):
    o_ref[...] = w_ref[0].astype(BF16)


def _bf16_weight(w_all, layer):
    _, rows, cols = w_all.shape
    block_rows = rows
    while block_rows * cols * 4 > WEIGHT_CAST_BLOCK_BYTES:
        block_rows //= 2
    assert rows % block_rows == 0 and block_rows % 16 == 0
    return pl.pallas_call(
        _cast_kernel,
        grid=(rows // block_rows,),
        in_specs=[pl.BlockSpec((1, block_rows, cols), lambda i: (layer, i, 0))],
        out_specs=pl.BlockSpec((block_rows, cols), lambda i: (i, 0)),
        out_shape=jax.ShapeDtypeStruct((rows, cols), BF16),
        compiler_params=pltpu.CompilerParams(
            dimension_semantics=("arbitrary",), vmem_limit_bytes=VMEM_LIMIT),
        name="weight_cast",
    )(w_all)


def _proj_layer(x2, gpre, w_in, cos2, sin2, seq_len):
    n_tok = x2.shape[0]
    tiles_per_seq = seq_len // TILE
    return pl.pallas_call(
        _proj_kernel,
        grid=(n_tok // TILE, N_PROJ_SLABS),
        in_specs=[
            pl.BlockSpec((TILE, D_MODEL), lambda i, j: (i, 0)),
            pl.BlockSpec((1, D_MODEL), lambda i, j: (0, 0)),
            pl.BlockSpec((D_MODEL, ATTN_WIDTH), lambda i, j: (0, j)),
            pl.BlockSpec((TILE, LANES), lambda i, j: (i % tiles_per_seq, 0)),
            pl.BlockSpec((TILE, LANES), lambda i, j: (i % tiles_per_seq, 0)),
        ],
        out_specs=pl.BlockSpec((TILE, ATTN_WIDTH), lambda i, j: (i, j)),
        out_shape=jax.ShapeDtypeStruct((n_tok, N_PROJ_SLABS * ATTN_WIDTH), BF16),
        scratch_shapes=[
            pltpu.VMEM((D_MODEL // LANES, TILE, LANES), F32),
            pltpu.VMEM((D_MODEL // LANES, TILE, LANES), F32),
            pltpu.VMEM((TILE, D_MODEL), BF16),
        ] + 4 * [pltpu.VMEM((TILE, LANES), F32)],
        compiler_params=pltpu.CompilerParams(
            dimension_semantics=("arbitrary", "arbitrary"), vmem_limit_bytes=VMEM_LIMIT),
        name="attn_proj",
    )(x2, gpre.reshape(1, -1), w_in, cos2, sin2)


def _attn_masks():
    t = ATTN_BLOCK
    kj = np.arange(2 * t)[:, None]
    qc = np.arange(t)[None, :]
    cur = (kj >= t) & (kj - t <= qc)
    prev = (kj < t) & (kj >= qc)
    valid = np.stack([cur, cur | prev])
    return jnp.asarray(np.where(valid, 0.0, NEG_BIG), dtype=BF16)


def _attn_kernel(q_ref, kp_ref, kc_ref, vp_ref, vc_ref, mask_ref, eye_ref, o_ref, st_ref, *, chained):
    t = ATTN_BLOCK
    first_step_mask = mask_ref[jnp.minimum(pl.program_id(2), 1)]
    eye = eye_ref[...]
    ones = jnp.ones((2 * t, LANES), BF16)
    lane = lax.broadcasted_iota(jnp.int32, (1, LANES), 1)
    head_a_qk = ((lane // (HEAD_DIM // 2)) % 2) == 0
    head_a_v = (lane // HEAD_DIM) == 0
    for jb in range(q_ref.shape[0] // t):
        rows = slice(jb * t, (jb + 1) * t)
        if chained and jb > 0:
            prev_k, prev_v, prev_rows, mask_t = kc_ref, vc_ref, slice((jb - 1) * t, jb * t), mask_ref[1]
        else:
            prev_k, prev_v, mask_t = kp_ref, vp_ref, first_step_mask
            prev_rows = slice(0, t) if chained else rows
        st = jnp.zeros((t, LANES), F32)
        for hp in range(N_HEADS // 2):
            sl = slice(hp * LANES, (hp + 1) * LANES)
            q2 = q_ref[rows, sl]
            zero = jnp.zeros_like(q2)
            lhs = jnp.concatenate([
                jnp.concatenate([jnp.where(head_a_qk, q2, zero), eye], axis=1),
                jnp.concatenate([jnp.where(head_a_qk, zero, q2), eye], axis=1)], axis=0)
            k_ext = jnp.concatenate(
                [jnp.concatenate([prev_k[prev_rows, sl], kc_ref[rows, sl]], axis=0), mask_t], axis=1)
            v_ext = jnp.concatenate(
                [jnp.concatenate([prev_v[prev_rows, sl], vc_ref[rows, sl]], axis=0), ones], axis=1)
            s = lax.dot_general(lhs, k_ext, (((1,), (1,)), ((), ())), preferred_element_type=F32)
            m = jnp.max(s, axis=-1, keepdims=True)
            p = jnp.exp2(s - m).astype(BF16)
            pv = jnp.dot(p, v_ext, preferred_element_type=F32)
            o_ref[rows, sl] = jnp.where(head_a_v, pv[:t, :LANES], pv[t:, :LANES]).astype(BF16)
            denom = pv[:, LANES:]
            st = jnp.where(lane == 2 * hp, m[:t], st)
            st = jnp.where(lane == 2 * hp + 1, m[t:], st)
            st = jnp.where(lane == N_HEADS + 2 * hp, denom[:t], st)
            st = jnp.where(lane == N_HEADS + 2 * hp + 1, denom[t:], st)
        st_ref[rows, :] = st


def _attn_group(proj2, g, dilation, batch, seq_len):
    n_tok = batch * seq_len
    t, qb = ATTN_BLOCK, ATTN_QB[g]
    n_blocks = seq_len // dilation // t
    tiles_per_seq = seq_len // TILE
    blocks_per_tile = TILE // t
    bpc = blocks_per_tile // dilation
    first = GROUP_FIRST_SLAB[g]
    chained = bpc >= qb
    if chained:
        assert bpc % qb == 0
        grid = (batch, dilation, n_blocks // qb)

        def row_block(b, r, n):
            return (b * tiles_per_seq + n // bpc) * blocks_per_tile + r * bpc + n % bpc

        cur = lambda b, r, n: row_block(b, r, qb * n) // qb
        prev = lambda b, r, n: row_block(b, r, jnp.maximum(qb * n - 1, 0))
        prev_rows = t
    else:
        assert bpc == 1 and dilation % qb == 0
        grid = (batch, dilation // qb, n_blocks)
        cur = lambda b, r, n: (b * tiles_per_seq + n) * (blocks_per_tile // qb) + r
        prev = lambda b, r, n: cur(b, r, jnp.maximum(n - 1, 0))
        prev_rows = qb * t

    def spec(c, is_prev):
        if is_prev:
            return pl.BlockSpec((prev_rows, ATTN_WIDTH), lambda b, r, n: (prev(b, r, n), first + c))
        return pl.BlockSpec((qb * t, ATTN_WIDTH), lambda b, r, n: (cur(b, r, n), first + c))

    return pl.pallas_call(
        functools.partial(_attn_kernel, chained=chained),
        grid=grid,
        in_specs=[spec(0, False), spec(1, True), spec(1, False), spec(2, True), spec(2, False),
                  pl.BlockSpec((2, 2 * t, t), lambda b, r, n: (0, 0, 0)),
                  pl.BlockSpec((t, t), lambda b, r, n: (0, 0))],
        out_specs=[
            pl.BlockSpec((qb * t, ATTN_WIDTH), lambda b, r, n: (cur(b, r, n), 0)),
            pl.BlockSpec((qb * t, LANES), lambda b, r, n: (cur(b, r, n), 0)),
        ],
        out_shape=[
            jax.ShapeDtypeStruct((n_tok, ATTN_WIDTH), BF16),
            jax.ShapeDtypeStruct((n_tok, LANES), F32),
        ],
        compiler_params=pltpu.CompilerParams(
            dimension_semantics=("arbitrary", "arbitrary", "arbitrary"), vmem_limit_bytes=VMEM_LIMIT),
        name=f"dilated_attn_g{g}",
    )(proj2, proj2, proj2, proj2, proj2, _attn_masks(), jnp.eye(t, dtype=BF16))


def _combine_kernel(x_ref, o0_ref, o1_ref, o2_ref, s0_ref, s1_ref, s2_ref, z_ref,
                    expand_ref, wout_ref, gpost_ref, out_ref, on1_sc, on2_sc, st1_sc, st2_sc, y_sc):
    n_slabs = ATTN_WIDTH // LANES
    is_head_lane = lax.broadcasted_iota(jnp.int32, (1, LANES), 1) < N_HEADS
    for sub in range(COMBINE_TOKENS // COMBINE_SUB):
        base = sub * COMBINE_SUB
        rows = slice(base, base + COMBINE_SUB)
        for o_ref, s_ref, on_sc, st_sc, (_, dilation) in (
                (o1_ref, s1_ref, on1_sc, st1_sc, DILATED_GROUPS[1]),
                (o2_ref, s2_ref, on2_sc, st2_sc, DILATED_GROUPS[2])):
            n = COMBINE_SUB // dilation
            src = slice(sub * n, (sub + 1) * n)
            for r in range(dilation):
                dst = pl.ds(base + r, n, stride=dilation)
                st_sc[dst, :] = s_ref[0, r, src, :]
                for c in range(n_slabs):
                    on_sc[c, dst, :] = o_ref[0, r, src, c * LANES:(c + 1) * LANES].astype(F32)

        sts = [s0_ref[rows, :], st1_sc[rows, :], st2_sc[rows, :]]
        m = jnp.maximum(jnp.maximum(sts[0], sts[1]), sts[2])
        es = [jnp.exp2(s - m) for s in sts]
        total = None
        for e, s in zip(es, sts):
            term = e * pltpu.roll(s, LANES - N_HEADS, 1)
            total = term if total is None else total + term
        inv = 1.0 / jnp.where(is_head_lane, total, 1.0)
        alphas = []
        for e in es:
            alpha = jnp.where(is_head_lane, e * inv, 0.0)
            hi = alpha.astype(BF16)
            lo = (alpha - hi.astype(F32)).astype(BF16)
            alphas.append(jnp.dot(jnp.concatenate([hi, lo], axis=1), expand_ref[...],
                                  preferred_element_type=F32))
        for c in range(n_slabs):
            sl = slice(c * LANES, (c + 1) * LANES)
            o = (alphas[0][:, sl] * o0_ref[rows, sl].astype(F32)
                 + alphas[1][:, sl] * on1_sc[c, rows, :] + alphas[2][:, sl] * on2_sc[c, rows, :])
            y_sc[rows, sl] = (o * _silu(z_ref[rows, sl].astype(F32))).astype(BF16)
        out = jnp.dot(y_sc[rows, :], wout_ref[...], preferred_element_type=F32)
        out_ref[rows, :] = x_ref[rows, :] + _rms_normalize(out, gpost_ref[...])


def _combine_layer(x2, outs, stats, proj2, w_out, gpost):
    n_tok = x2.shape[0]
    tm = COMBINE_TOKENS
    n_tiles = n_tok // TILE
    per_tile = TILE // tm
    head_of_lane = np.arange(ATTN_WIDTH) // HEAD_DIM
    expand = jnp.asarray(np.tile(np.arange(LANES)[:, None] == head_of_lane[None, :], (2, 1)), dtype=BF16)
    tok = lambda i: (i, 0)
    const = lambda i: (0, 0)
    in_tile = lambda i: (i // per_tile, 0, i % per_tile, 0)
    d1, d2 = DILATED_GROUPS[1][1], DILATED_GROUPS[2][1]
    return pl.pallas_call(
        _combine_kernel,
        grid=(n_tok // tm,),
        in_specs=[
            pl.BlockSpec((tm, D_MODEL), tok),
            pl.BlockSpec((tm, ATTN_WIDTH), tok),
            pl.BlockSpec((1, d1, tm // d1, ATTN_WIDTH), in_tile),
            pl.BlockSpec((1, d2, tm // d2, ATTN_WIDTH), in_tile),
            pl.BlockSpec((tm, LANES), tok),
            pl.BlockSpec((1, d1, tm // d1, LANES), in_tile),
            pl.BlockSpec((1, d2, tm // d2, LANES), in_tile),
            pl.BlockSpec((tm, ATTN_WIDTH), lambda i: (i, GATE_OUT_SLAB)),
            pl.BlockSpec((2 * LANES, ATTN_WIDTH), const),
            pl.BlockSpec((ATTN_WIDTH, D_MODEL), const),
            pl.BlockSpec((1, D_MODEL), const),
        ],
        out_specs=pl.BlockSpec((tm, D_MODEL), tok),
        out_shape=jax.ShapeDtypeStruct(x2.shape, F32),
        scratch_shapes=[
            pltpu.VMEM((ATTN_WIDTH // LANES, tm, LANES), F32),
            pltpu.VMEM((ATTN_WIDTH // LANES, tm, LANES), F32),
            pltpu.VMEM((tm, LANES), F32),
            pltpu.VMEM((tm, LANES), F32),
            pltpu.VMEM((tm, ATTN_WIDTH), BF16),
        ],
        compiler_params=pltpu.CompilerParams(
            dimension_semantics=("arbitrary",), vmem_limit_bytes=VMEM_LIMIT),
        name="attn_combine",
    )(x2,
      outs[0],
      outs[1].reshape(n_tiles, d1, TILE // d1, ATTN_WIDTH),
      outs[2].reshape(n_tiles, d2, TILE // d2, ATTN_WIDTH),
      stats[0],
      stats[1].reshape(n_tiles, d1, TILE // d1, LANES),
      stats[2].reshape(n_tiles, d2, TILE // d2, LANES),
      proj2, expand, w_out, gpost.reshape(1, -1))


def _rope_tables(seq_len):
    half = HEAD_DIM // 2
    exponent = np.arange(0, HEAD_DIM, 2, dtype=np.float32) / np.float32(HEAD_DIM)
    power = np.power(np.float64(ROPE_THETA), exponent.astype(np.float64)).astype(np.float32)
    inv_freq = np.float32(1.0) / power
    ang = np.arange(seq_len, dtype=np.float32)[:, None] * inv_freq[None, :]
    cos = np.cos(ang.astype(np.float64)).astype(np.float32)
    sin = np.sin(ang.astype(np.float64)).astype(np.float32)
    cos2 = np.tile(cos, (1, LANES // half))
    sin2 = np.concatenate([-sin, -sin, sin, sin], axis=1)
    return jnp.asarray(cos2), jnp.asarray(sin2)


def kernel(x, norm_pre, norm_post, a_w_in, a_ln_g, a_ln_b, a_w_s, a_b_s, a_w_out, b_w_in, b_w_out):
    batch, seq_len, d = x.shape
    depth = norm_pre.shape[0]
    cos2, sin2 = _rope_tables(seq_len)
    x2 = x.reshape(batch * seq_len, d)
    for i in range(depth):
        j = i // 2
        if i % 2 == 0:
            x2 = _gmlp_layer(x2, norm_pre[i], norm_post[i], _bf16_weight(a_w_in, j), a_ln_g[j], a_ln_b[j],
                             a_w_s[j], a_b_s[j], _bf16_weight(a_w_out, j))
        else:
            proj2 = _proj_layer(x2, norm_pre[i], _proj_weight(b_w_in, j), cos2, sin2, seq_len)
            outs, stats = [], []
            for g, (_, dilation) in enumerate(DILATED_GROUPS):
                o, st = _attn_group(proj2, g, dilation, batch, seq_len)
                outs.append(o)
                stats.append(st)
            x2 = _combine_layer(x2, outs, stats, proj2, _bf16_weight(b_w_out, j), norm_post[i])
    return x2.reshape(batch, seq_len, d)
```

```python
import functools

import numpy as np
import jax
import jax.numpy as jnp
from jax import lax
from jax.experimental import pallas as pl
from jax.experimental.pallas import tpu as pltpu

F32 = jnp.float32
BF16 = jnp.bfloat16

D_MODEL = 1024
RMS_EPS = 1e-6
LN_EPS = 1e-5

GMLP_WIDTH = 2048
GMLP_CHUNK = 128
GMLP_GROUPS = 8
GMLP_GROUP_WIDTH = GMLP_WIDTH // GMLP_GROUPS
GMLP_DOT_GROUPS = 4

HEAD_DIM = 64
N_HEADS = 16
ATTN_WIDTH = N_HEADS * HEAD_DIM
DILATED_GROUPS = ((128, 1), (512, 4), (2048, 16))
N_DIL_GROUPS = len(DILATED_GROUPS)
ATTN_BLOCK = 128
ATTN_QB = (16, 16, 16)
ROPE_THETA = 10000.0
N_PROJ_SLABS = 3 * N_DIL_GROUPS + 1
Z_SLAB = 3 * N_DIL_GROUPS
GROUP_FIRST_SLAB = (0, 4, 7)
GATE_OUT_SLAB = 3
TILE = 2048

LANES = 128
NEG_BIG = -1e30

GMLP_TOKENS = 512
PROJ_ROW_CHUNK = 512
COMBINE_TOKENS = 1024
COMBINE_SUB = 512
VMEM_LIMIT = 60 * 1024 * 1024
WEIGHT_CAST_BLOCK_BYTES = 4 * 1024 * 1024


def _gelu(x):
    return 0.5 * x * (1.0 + lax.erf(x * np.float32(np.sqrt(0.5))))


def _silu(x):
    return x * jax.nn.sigmoid(x)


def _rms_normalize(x, gain):
    ms = jnp.mean(x * x, axis=-1, keepdims=True)
    return x * lax.rsqrt(ms + RMS_EPS) * gain


def _gmlp_kernel(x_ref, gpre_ref, gpost_ref, win_ref, lng_ref, lnb_ref, ws_ref,
                 bs_ref, wout_ref, o_ref, h_sc, v_sc, t_sc, y_sc):
    tm = x_ref.shape[0]
    gw = GMLP_GROUP_WIDTH
    x = x_ref[...]
    h_sc[...] = _rms_normalize(x, gpre_ref[...]).astype(BF16)

    pw = GMLP_DOT_GROUPS * gw
    s1 = jnp.zeros((tm, 1), F32)
    for gp in range(GMLP_WIDTH // pw):
        c0 = GMLP_WIDTH + gp * pw
        v = _gelu(jnp.dot(h_sc[...], win_ref[:, c0:c0 + pw], preferred_element_type=F32))
        v_sc[:, gp * pw:(gp + 1) * pw] = v
        s1 = s1 + jnp.sum(v, axis=-1, keepdims=True)
    for gp in range(GMLP_WIDTH // pw):
        psl = slice(gp * pw, (gp + 1) * pw)
        u = _gelu(jnp.dot(h_sc[...], win_ref[:, psl], preferred_element_type=F32))
        z0 = 2 * GMLP_WIDTH + gp * pw
        z = jnp.dot(h_sc[...], win_ref[:, z0:z0 + pw], preferred_element_type=F32)
        t_sc[:, psl] = u * _silu(z)

    mu = s1 * (1.0 / GMLP_WIDTH)
    s2 = jnp.zeros((tm, 1), F32)
    for g in range(GMLP_GROUPS):
        dv = v_sc[:, g * gw:(g + 1) * gw] - mu
        s2 = s2 + jnp.sum(dv * dv, axis=-1, keepdims=True)
    rstd = lax.rsqrt(s2 * (1.0 / GMLP_WIDTH) + LN_EPS)

    row = lax.broadcasted_iota(jnp.int32, (GMLP_CHUNK, GMLP_CHUNK), 0)
    col = lax.broadcasted_iota(jnp.int32, (GMLP_CHUNK, GMLP_CHUNK), 1)
    causal = col <= row
    for g in range(GMLP_GROUPS):
        sl = slice(g * gw, (g + 1) * gw)
        vn = ((v_sc[:, sl] - mu) * rstd * lng_ref[:, sl] + lnb_ref[:, sl]).astype(BF16)
        wm = jnp.where(causal, ws_ref[g], 0.0).astype(BF16)
        bias = bs_ref[:, g:g + 1]
        sv = jnp.concatenate(
            [jnp.dot(wm, vn[c * GMLP_CHUNK:(c + 1) * GMLP_CHUNK], preferred_element_type=F32) + bias
             for c in range(tm // GMLP_CHUNK)], axis=0)
        y_sc[:, sl] = (t_sc[:, sl] * sv).astype(BF16)

    out = jnp.dot(y_sc[...], wout_ref[...], preferred_element_type=F32)
    o_ref[...] = x_ref[...] + _rms_normalize(out, gpost_ref[...])


def _gmlp_layer(x2, gpre, gpost, w_in, ln_g, ln_b, w_s, b_s, w_out):
    n_tok = x2.shape[0]
    tm = GMLP_TOKENS
    const = lambda i: (0, 0)
    return pl.pallas_call(
        _gmlp_kernel,
        grid=(n_tok // tm,),
        in_specs=[
            pl.BlockSpec((tm, D_MODEL), lambda i: (i, 0)),
            pl.BlockSpec((1, D_MODEL), const),
            pl.BlockSpec((1, D_MODEL), const),
            pl.BlockSpec((D_MODEL, 3 * GMLP_WIDTH), const, pipeline_mode=pl.Buffered(1)),
            pl.BlockSpec((1, GMLP_WIDTH), const),
            pl.BlockSpec((1, GMLP_WIDTH), const),
            pl.BlockSpec((GMLP_GROUPS, GMLP_CHUNK, GMLP_CHUNK), lambda i: (0, 0, 0)),
            pl.BlockSpec((GMLP_CHUNK, GMLP_GROUPS), const),
            pl.BlockSpec((GMLP_WIDTH, D_MODEL), const, pipeline_mode=pl.Buffered(1)),
        ],
        out_specs=pl.BlockSpec((tm, D_MODEL), lambda i: (i, 0)),
        out_shape=jax.ShapeDtypeStruct(x2.shape, F32),
        scratch_shapes=[
            pltpu.VMEM((tm, D_MODEL), BF16),
            pltpu.VMEM((tm, GMLP_WIDTH), F32),
            pltpu.VMEM((tm, GMLP_WIDTH), F32),
            pltpu.VMEM((tm, GMLP_WIDTH), BF16),
        ],
        compiler_params=pltpu.CompilerParams(
            dimension_semantics=("arbitrary",), vmem_limit_bytes=VMEM_LIMIT),
        name="gmlp_layer",
    )(x2, gpre.reshape(1, -1), gpost.reshape(1, -1), w_in,
      ln_g.reshape(1, -1), ln_b.reshape(1, -1), w_s, jnp.transpose(b_s), w_out)


def _proj_kernel(x_ref, gpre_ref, w_ref, cos_ref, sin_ref, o_ref,
                 hs_sc, hs1_sc, h_sc, cos1_sc, sin1_sc, cos2_sc, sin2_sc):
    j = pl.program_id(1)
    n_chunks = TILE // PROJ_ROW_CHUNK
    n_slabs = D_MODEL // LANES
    first1, first2 = GROUP_FIRST_SLAB[1], GROUP_FIRST_SLAB[2]
    d1, d2 = DILATED_GROUPS[1][1], DILATED_GROUPS[2][1]
    ratio = d2 // d1

    @pl.when(j == 0)
    def _():
        for rc in range(n_chunks):
            rows = slice(rc * PROJ_ROW_CHUNK, (rc + 1) * PROJ_ROW_CHUNK)
            hs = _rms_normalize(x_ref[rows, :], gpre_ref[...])
            for c in range(n_slabs):
                hs_sc[c, rows, :] = hs[:, c * LANES:(c + 1) * LANES]

    @pl.when(j == first1)
    def _():
        rows = TILE // d1
        for b in range(d1):
            out, src = slice(b * rows, (b + 1) * rows), pl.ds(b, rows, stride=d1)
            for c in range(n_slabs):
                v = hs_sc[c, src, :]
                hs1_sc[c, out, :] = v
                h_sc[out, c * LANES:(c + 1) * LANES] = v.astype(BF16)
            cos1_sc[out, :] = cos_ref[src, :]
            sin1_sc[out, :] = sin_ref[src, :]

    @pl.when(j == first2)
    def _():
        rows = TILE // d2
        for b in range(d1):
            for a in range(ratio):
                r = a * d1 + b
                out, src = slice(r * rows, (r + 1) * rows), pl.ds(b * (TILE // d1) + a, rows, stride=ratio)
                for c in range(n_slabs):
                    h_sc[out, c * LANES:(c + 1) * LANES] = hs1_sc[c, src, :].astype(BF16)
                cos2_sc[out, :] = cos1_sc[src, :]
                sin2_sc[out, :] = sin1_sc[src, :]

    is_q = functools.reduce(jnp.logical_or, [j == s for s in GROUP_FIRST_SLAB])
    is_k = functools.reduce(jnp.logical_or, [j == s + 1 for s in GROUP_FIRST_SLAB])
    is_rope = jnp.logical_or(is_q, is_k)
    in_group0 = j < first1
    in_group1 = jnp.logical_and(j >= first1, j < first2)
    in_group2 = j >= first2

    def lhs_rows(from_token_order, rows):
        if from_token_order:
            return jnp.concatenate([hs_sc[c, rows, :] for c in range(n_slabs)], axis=1).astype(BF16)
        return h_sc[rows, :]

    def rope_slab(from_token_order, cos_tab, sin_tab):
        scale = jnp.where(is_q, np.float32(np.log2(np.e) / np.sqrt(HEAD_DIM)), np.float32(1.0))
        for rc in range(n_chunks):
            rows = slice(rc * PROJ_ROW_CHUNK, (rc + 1) * PROJ_ROW_CHUNK)
            p = jnp.dot(lhs_rows(from_token_order, rows), w_ref[...], preferred_element_type=F32)
            cos = cos_tab[rows, :] * scale
            sin = sin_tab[rows, :] * scale
            for s in range(ATTN_WIDTH // LANES):
                t = p[:, s * LANES:(s + 1) * LANES]
                o_ref[rows, s * LANES:(s + 1) * LANES] = (
                    t * cos + pltpu.roll(t, LANES // 2, 1) * sin).astype(BF16)

    def plain_slab(from_token_order):
        for rc in range(n_chunks):
            rows = slice(rc * PROJ_ROW_CHUNK, (rc + 1) * PROJ_ROW_CHUNK)
            o_ref[rows, :] = jnp.dot(lhs_rows(from_token_order, rows), w_ref[...],
                                     preferred_element_type=F32).astype(BF16)

    pl.when(jnp.logical_and(is_rope, in_group0))(functools.partial(rope_slab, True, cos_ref, sin_ref))
    pl.when(jnp.logical_and(is_rope, in_group1))(functools.partial(rope_slab, False, cos1_sc, sin1_sc))
    pl.when(jnp.logical_and(is_rope, in_group2))(functools.partial(rope_slab, False, cos2_sc, sin2_sc))
    is_plain = jnp.logical_not(is_rope)
    pl.when(jnp.logical_and(is_plain, in_group0))(functools.partial(plain_slab, True))
    pl.when(jnp.logical_and(is_plain, jnp.logical_not(in_group0)))(functools.partial(plain_slab, False))


def _proj_weight_kernel(w_ref, o_ref):
    s = pl.program_id(0)
    is_qk = functools.reduce(jnp.logical_or, [jnp.logical_or(s == f, s == f + 1) for f in GROUP_FIRST_SLAB])

    @pl.when(is_qk)
    def _():
        half = HEAD_DIM // 2
        quarter = lax.broadcasted_iota(jnp.int32, (1, LANES), 1) // half
        for c in range(ATTN_WIDTH // LANES):
            sl = slice(c * LANES, (c + 1) * LANES)
            w = w_ref[0, :, sl]
            w = jnp.where(quarter == 1, pltpu.roll(w, LANES - half, 1),
                          jnp.where(quarter == 2, pltpu.roll(w, half, 1), w))
            o_ref[:, sl] = w.astype(BF16)

    @pl.when(jnp.logical_not(is_qk))
    def _():
        o_ref[...] = w_ref[0].astype(BF16)


def _proj_weight(w_in_all, layer):
    def source_slab(s):
        return jnp.where(s == GATE_OUT_SLAB, Z_SLAB, jnp.where(s < GATE_OUT_SLAB, s, s - 1))

    return pl.pallas_call(
        _proj_weight_kernel,
        grid=(N_PROJ_SLABS,),
        in_specs=[pl.BlockSpec((1, D_MODEL, ATTN_WIDTH), lambda s: (layer, 0, source_slab(s)))],
        out_specs=pl.BlockSpec((D_MODEL, ATTN_WIDTH), lambda s: (0, s)),
        out_shape=jax.ShapeDtypeStruct((D_MODEL, N_PROJ_SLABS * ATTN_WIDTH), BF16),
        compiler_params=pltpu.CompilerParams(
            dimension_semantics=("arbitrary",), vmem_limit_bytes=VMEM_LIMIT),
        name="proj_weight_prep",
    )(w_in_all)


def _cast_kernel(w_ref, o_ref):
    o_ref[...] = w_ref[0].astype(BF16)


def _bf16_weight(w_all, layer):
    _, rows, cols = w_all.shape
    block_rows = rows
    while block_rows * cols * 4 > WEIGHT_CAST_BLOCK_BYTES:
        block_rows //= 2
    assert rows % block_rows == 0 and block_rows % 16 == 0
    return pl.pallas_call(
        _cast_kernel,
        grid=(rows // block_rows,),
        in_specs=[pl.BlockSpec((1, block_rows, cols), lambda i: (layer, i, 0))],
        out_specs=pl.BlockSpec((block_rows, cols), lambda i: (i, 0)),
        out_shape=jax.ShapeDtypeStruct((rows, cols), BF16),
        compiler_params=pltpu.CompilerParams(
            dimension_semantics=("arbitrary",), vmem_limit_bytes=VMEM_LIMIT),
        name="weight_cast",
    )(w_all)


def _proj_layer(x2, gpre, w_in, cos2, sin2, seq_len):
    n_tok = x2.shape[0]
    tiles_per_seq = seq_len // TILE
    return pl.pallas_call(
        _proj_kernel,
        grid=(n_tok // TILE, N_PROJ_SLABS),
        in_specs=[
            pl.BlockSpec((TILE, D_MODEL), lambda i, j: (i, 0)),
            pl.BlockSpec((1, D_MODEL), lambda i, j: (0, 0)),
            pl.BlockSpec((D_MODEL, ATTN_WIDTH), lambda i, j: (0, j)),
            pl.BlockSpec((TILE, LANES), lambda i, j: (i % tiles_per_seq, 0)),
            pl.BlockSpec((TILE, LANES), lambda i, j: (i % tiles_per_seq, 0)),
        ],
        out_specs=pl.BlockSpec((TILE, ATTN_WIDTH), lambda i, j: (i, j)),
        out_shape=jax.ShapeDtypeStruct((n_tok, N_PROJ_SLABS * ATTN_WIDTH), BF16),
        scratch_shapes=[
            pltpu.VMEM((D_MODEL // LANES, TILE, LANES), F32),
            pltpu.VMEM((D_MODEL // LANES, TILE, LANES), F32),
            pltpu.VMEM((TILE, D_MODEL), BF16),
        ] + 4 * [pltpu.VMEM((TILE, LANES), F32)],
        compiler_params=pltpu.CompilerParams(
            dimension_semantics=("arbitrary", "arbitrary"), vmem_limit_bytes=VMEM_LIMIT),
        name="attn_proj",
    )(x2, gpre.reshape(1, -1), w_in, cos2, sin2)


def _attn_masks():
    t = ATTN_BLOCK
    kj = np.arange(2 * t)[:, None]
    qc = np.arange(t)[None, :]
    cur = (kj >= t) & (kj - t <= qc)
    prev = (kj < t) & (kj >= qc)
    valid = np.stack([cur, cur | prev])
    return jnp.asarray(np.where(valid, 0.0, NEG_BIG), dtype=BF16)


def _attn_kernel(q_ref, kp_ref, kc_ref, vp_ref, vc_ref, mask_ref, eye_ref, o_ref, st_ref, *, chain):
    t = ATTN_BLOCK
    first_step_mask = mask_ref[jnp.minimum(pl.program_id(2), 1)]
    eye = eye_ref[...]
    ones = jnp.ones((2 * t, LANES), BF16)
    lane = lax.broadcasted_iota(jnp.int32, (1, LANES), 1)
    head_a_qk = ((lane // (HEAD_DIM // 2)) % 2) == 0
    head_a_v = (lane // HEAD_DIM) == 0
    for jb in range(q_ref.shape[0] // t):
        rows = slice(jb * t, (jb + 1) * t)
        if jb % chain > 0:
            prev_k, prev_v, prev_rows, mask_t = kc_ref, vc_ref, slice((jb - 1) * t, jb * t), mask_ref[1]
        else:
            prev_k, prev_v, mask_t = kp_ref, vp_ref, first_step_mask
            start = 0 if kp_ref.shape[0] == t else (jb + chain - 1) * t
            prev_rows = slice(start, start + t)
        st = jnp.zeros((t, LANES), F32)
        for hp in range(N_HEADS // 2):
            sl = slice(hp * LANES, (hp + 1) * LANES)
            q2 = q_ref[rows, sl]
            zero = jnp.zeros_like(q2)
            lhs = jnp.concatenate([
                jnp.concatenate([jnp.where(head_a_qk, q2, zero), eye], axis=1),
                jnp.concatenate([jnp.where(head_a_qk, zero, q2), eye], axis=1)], axis=0)
            k_ext = jnp.concatenate(
                [jnp.concatenate([prev_k[prev_rows, sl], kc_ref[rows, sl]], axis=0), mask_t], axis=1)
            v_ext = jnp.concatenate(
                [jnp.concatenate([prev_v[prev_rows, sl], vc_ref[rows, sl]], axis=0), ones], axis=1)
            s = lax.dot_general(lhs, k_ext, (((1,), (1,)), ((), ())), preferred_element_type=F32)
            m = jnp.max(s, axis=-1, keepdims=True)
            p = jnp.exp2(s - m).astype(BF16)
            pv = jnp.dot(p, v_ext, preferred_element_type=F32)
            o_ref[rows, sl] = jnp.where(head_a_v, pv[:t, :LANES], pv[t:, :LANES]).astype(BF16)
            denom = pv[:, LANES:]
            st = jnp.where(lane == 2 * hp, m[:t], st)
            st = jnp.where(lane == 2 * hp + 1, m[t:], st)
            st = jnp.where(lane == N_HEADS + 2 * hp, denom[:t], st)
            st = jnp.where(lane == N_HEADS + 2 * hp + 1, denom[t:], st)
        st_ref[rows, :] = st


def _attn_group(proj2, g, dilation, batch, seq_len):
    n_tok = batch * seq_len
    t, qb = ATTN_BLOCK, ATTN_QB[g]
    n_blocks = seq_len // dilation // t
    tiles_per_seq = seq_len // TILE
    blocks_per_tile = TILE // t
    bpc = blocks_per_tile // dilation
    first = GROUP_FIRST_SLAB[g]
    if bpc >= qb:
        assert bpc % qb == 0
        chain = qb
        grid = (batch, dilation, n_blocks // qb)

        def row_block(b, r, n):
            return (b * tiles_per_seq + n // bpc) * blocks_per_tile + r * bpc + n % bpc

        cur = lambda b, r, n: row_block(b, r, qb * n) // qb
        prev = lambda b, r, n: row_block(b, r, jnp.maximum(qb * n - 1, 0))
        prev_rows = t
    else:
        assert qb % bpc == 0 and blocks_per_tile % qb == 0
        chain = bpc
        grid = (batch, blocks_per_tile // qb, n_blocks // bpc)
        cur = lambda b, r, n: (b * tiles_per_seq + n) * (blocks_per_tile // qb) + r
        prev = lambda b, r, n: cur(b, r, jnp.maximum(n - 1, 0))
        prev_rows = qb * t

    def spec(c, is_prev):
        if is_prev:
            return pl.BlockSpec((prev_rows, ATTN_WIDTH), lambda b, r, n: (prev(b, r, n), first + c))
        return pl.BlockSpec((qb * t, ATTN_WIDTH), lambda b, r, n: (cur(b, r, n), first + c))

    return pl.pallas_call(
        functools.partial(_attn_kernel, chain=chain),
        grid=grid,
        in_specs=[spec(0, False), spec(1, True), spec(1, False), spec(2, True), spec(2, False),
                  pl.BlockSpec((2, 2 * t, t), lambda b, r, n: (0, 0, 0)),
                  pl.BlockSpec((t, t), lambda b, r, n: (0, 0))],
        out_specs=[
            pl.BlockSpec((qb * t, ATTN_WIDTH), lambda b, r, n: (cur(b, r, n), 0)),
            pl.BlockSpec((qb * t, LANES), lambda b, r, n: (cur(b, r, n), 0)),
        ],
        out_shape=[
            jax.ShapeDtypeStruct((n_tok, ATTN_WIDTH), BF16),
            jax.ShapeDtypeStruct((n_tok, LANES), F32),
        ],
        compiler_params=pltpu.CompilerParams(
            dimension_semantics=("arbitrary", "arbitrary", "arbitrary"), vmem_limit_bytes=VMEM_LIMIT),
        name=f"dilated_attn_g{g}",
    )(proj2, proj2, proj2, proj2, proj2, _attn_masks(), jnp.eye(t, dtype=BF16))


def _combine_kernel(x_ref, o0_ref, o1_ref, o2_ref, s0_ref, s1_ref, s2_ref, z_ref,
                    expand_ref, wout_ref, gpost_ref, out_ref, on1_sc, on2_sc, st1_sc, st2_sc, y_sc):
    n_slabs = ATTN_WIDTH // LANES
    is_head_lane = lax.broadcasted_iota(jnp.int32, (1, LANES), 1) < N_HEADS
    for sub in range(COMBINE_TOKENS // COMBINE_SUB):
        base = sub * COMBINE_SUB
        rows = slice(base, base + COMBINE_SUB)
        for o_ref, s_ref, on_sc, st_sc, (_, dilation) in (
                (o1_ref, s1_ref, on1_sc, st1_sc, DILATED_GROUPS[1]),
                (o2_ref, s2_ref, on2_sc, st2_sc, DILATED_GROUPS[2])):
            n = COMBINE_SUB // dilation
            src = slice(sub * n, (sub + 1) * n)
            for r in range(dilation):
                dst = pl.ds(base + r, n, stride=dilation)
                st_sc[dst, :] = s_ref[0, r, src, :]
                for c in range(n_slabs):
                    on_sc[c, dst, :] = o_ref[0, r, src, c * LANES:(c + 1) * LANES].astype(F32)

        sts = [s0_ref[rows, :], st1_sc[rows, :], st2_sc[rows, :]]
        m = jnp.maximum(jnp.maximum(sts[0], sts[1]), sts[2])
        es = [jnp.exp2(s - m) for s in sts]
        total = None
        for e, s in zip(es, sts):
            term = e * pltpu.roll(s, LANES - N_HEADS, 1)
            total = term if total is None else total + term
        inv = 1.0 / jnp.where(is_head_lane, total, 1.0)
        alphas = []
        for e in es:
            alpha = jnp.where(is_head_lane, e * inv, 0.0)
            hi = alpha.astype(BF16)
            lo = (alpha - hi.astype(F32)).astype(BF16)
            alphas.append(jnp.dot(jnp.concatenate([hi, lo], axis=1), expand_ref[...],
                                  preferred_element_type=F32))
        for c in range(n_slabs):
            sl = slice(c * LANES, (c + 1) * LANES)
            o = (alphas[0][:, sl] * o0_ref[rows, sl].astype(F32)
                 + alphas[1][:, sl] * on1_sc[c, rows, :] + alphas[2][:, sl] * on2_sc[c, rows, :])
            y_sc[rows, sl] = (o * _silu(z_ref[rows, sl].astype(F32))).astype(BF16)
        out = jnp.dot(y_sc[rows, :], wout_ref[...], preferred_element_type=F32)
        out_ref[rows, :] = x_ref[rows, :] + _rms_normalize(out, gpost_ref[...])


def _combine_layer(x2, outs, stats, proj2, w_out, gpost):
    n_tok = x2.shape[0]
    tm = COMBINE_TOKENS
    n_tiles = n_tok // TILE
    per_tile = TILE // tm
    head_of_lane = np.arange(ATTN_WIDTH) // HEAD_DIM
    expand = jnp.asarray(np.tile(np.arange(LANES)[:, None] == head_of_lane[None, :], (2, 1)), dtype=BF16)
    tok = lambda i: (i, 0)
    const = lambda i: (0, 0)
    in_tile = lambda i: (i // per_tile, 0, i % per_tile, 0)
    d1, d2 = DILATED_GROUPS[1][1], DILATED_GROUPS[2][1]
    return pl.pallas_call(
        _combine_kernel,
        grid=(n_tok // tm,),
        in_specs=[
            pl.BlockSpec((tm, D_MODEL), tok),
            pl.BlockSpec((tm, ATTN_WIDTH), tok),
            pl.BlockSpec((1, d1, tm // d1, ATTN_WIDTH), in_tile),
            pl.BlockSpec((1, d2, tm // d2, ATTN_WIDTH), in_tile),
            pl.BlockSpec((tm, LANES), tok),
            pl.BlockSpec((1, d1, tm // d1, LANES), in_tile),
            pl.BlockSpec((1, d2, tm // d2, LANES), in_tile),
            pl.BlockSpec((tm, ATTN_WIDTH), lambda i: (i, GATE_OUT_SLAB)),
            pl.BlockSpec((2 * LANES, ATTN_WIDTH), const),
            pl.BlockSpec((ATTN_WIDTH, D_MODEL), const),
            pl.BlockSpec((1, D_MODEL), const),
        ],
        out_specs=pl.BlockSpec((tm, D_MODEL), tok),
        out_shape=jax.ShapeDtypeStruct(x2.shape, F32),
        scratch_shapes=[
            pltpu.VMEM((ATTN_WIDTH // LANES, tm, LANES), F32),
            pltpu.VMEM((ATTN_WIDTH // LANES, tm, LANES), F32),
            pltpu.VMEM((tm, LANES), F32),
            pltpu.VMEM((tm, LANES), F32),
            pltpu.VMEM((tm, ATTN_WIDTH), BF16),
        ],
        compiler_params=pltpu.CompilerParams(
            dimension_semantics=("arbitrary",), vmem_limit_bytes=VMEM_LIMIT),
        name="attn_combine",
    )(x2,
      outs[0],
      outs[1].reshape(n_tiles, d1, TILE // d1, ATTN_WIDTH),
      outs[2].reshape(n_tiles, d2, TILE // d2, ATTN_WIDTH),
      stats[0],
      stats[1].reshape(n_tiles, d1, TILE // d1, LANES),
      stats[2].reshape(n_tiles, d2, TILE // d2, LANES),
      proj2, expand, w_out, gpost.reshape(1, -1))


def _rope_tables(seq_len):
    half = HEAD_DIM // 2
    exponent = np.arange(0, HEAD_DIM, 2, dtype=np.float32) / np.float32(HEAD_DIM)
    power = np.power(np.float64(ROPE_THETA), exponent.astype(np.float64)).astype(np.float32)
    inv_freq = np.float32(1.0) / power
    ang = np.arange(seq_len, dtype=np.float32)[:, None] * inv_freq[None, :]
    cos = np.cos(ang.astype(np.float64)).astype(np.float32)
    sin = np.sin(ang.astype(np.float64)).astype(np.float32)
    cos2 = np.tile(cos, (1, LANES // half))
    sin2 = np.concatenate([-sin, -sin, sin, sin], axis=1)
    return jnp.asarray(cos2), jnp.asarray(sin2)


def kernel(x, norm_pre, norm_post, a_w_in, a_ln_g, a_ln_b, a_w_s, a_b_s, a_w_out, b_w_in, b_w_out):
    batch, seq_len, d = x.shape
    depth = norm_pre.shape[0]
    cos2, sin2 = _rope_tables(seq_len)
    x2 = x.reshape(batch * seq_len, d)
    for i in range(depth):
        j = i // 2
        if i % 2 == 0:
            x2 = _gmlp_layer(x2, norm_pre[i], norm_post[i], _bf16_weight(a_w_in, j), a_ln_g[j], a_ln_b[j],
                             a_w_s[j], a_b_s[j], _bf16_weight(a_w_out, j))
        else:
            proj2 = _proj_layer(x2, norm_pre[i], _proj_weight(b_w_in, j), cos2, sin2, seq_len)
            outs, stats = [], []
            for g, (_, dilation) in enumerate(DILATED_GROUPS):
                o, st = _attn_group(proj2, g, dilation, batch, seq_len)
                outs.append(o)
                stats.append(st)
            x2 = _combine_layer(x2, outs, stats, proj2, _bf16_weight(b_w_out, j), norm_post[i])
    return x2.reshape(batch, seq_len, d)
```

```python
import functools

import numpy as np
import jax
import jax.numpy as jnp
from jax import lax
from jax.experimental import pallas as pl
from jax.experimental.pallas import tpu as pltpu

F32 = jnp.float32
BF16 = jnp.bfloat16

D_MODEL = 1024
RMS_EPS = 1e-6
LN_EPS = 1e-5

GMLP_WIDTH = 2048
GMLP_CHUNK = 128
GMLP_GROUPS = 8
GMLP_GROUP_WIDTH = GMLP_WIDTH // GMLP_GROUPS
GMLP_DOT_GROUPS = 4

HEAD_DIM = 64
N_HEADS = 16
ATTN_WIDTH = N_HEADS * HEAD_DIM
DILATED_GROUPS = ((128, 1), (512, 4), (2048, 16))
N_DIL_GROUPS = len(DILATED_GROUPS)
ATTN_BLOCK = 128
ROPE_THETA = 10000.0
N_PROJ_SLABS = 3 * N_DIL_GROUPS + 1
Z_SLAB = 3 * N_DIL_GROUPS
GROUP_FIRST_SLAB = (0, 4, 7)
GATE_OUT_SLAB = 3
TILE = 2048

LANES = 128
NEG_BIG = -1e30

GMLP_TOKENS = 512
PROJ_ROW_CHUNK = 512
COMBINE_TOKENS = 1024
COMBINE_SUB = 512
VMEM_LIMIT = 60 * 1024 * 1024
WEIGHT_CAST_BLOCK_BYTES = 4 * 1024 * 1024


def _gelu(x):
    return 0.5 * x * (1.0 + lax.erf(x * np.float32(np.sqrt(0.5))))


def _silu(x):
    return x * jax.nn.sigmoid(x)


def _rms_normalize(x, gain):
    ms = jnp.mean(x * x, axis=-1, keepdims=True)
    return x * lax.rsqrt(ms + RMS_EPS) * gain


def _gmlp_kernel(x_ref, gpre_ref, gpost_ref, win_ref, lng_ref, lnb_ref, ws_ref,
                 bs_ref, wout_ref, o_ref, h_sc, v_sc, t_sc, y_sc):
    tm = x_ref.shape[0]
    gw = GMLP_GROUP_WIDTH
    x = x_ref[...]
    h_sc[...] = _rms_normalize(x, gpre_ref[...]).astype(BF16)

    pw = GMLP_DOT_GROUPS * gw
    s1 = jnp.zeros((tm, 1), F32)
    for gp in range(GMLP_WIDTH // pw):
        c0 = GMLP_WIDTH + gp * pw
        v = _gelu(jnp.dot(h_sc[...], win_ref[:, c0:c0 + pw], preferred_element_type=F32))
        v_sc[:, gp * pw:(gp + 1) * pw] = v
        s1 = s1 + jnp.sum(v, axis=-1, keepdims=True)
    for gp in range(GMLP_WIDTH // pw):
        psl = slice(gp * pw, (gp + 1) * pw)
        u = _gelu(jnp.dot(h_sc[...], win_ref[:, psl], preferred_element_type=F32))
        z0 = 2 * GMLP_WIDTH + gp * pw
        z = jnp.dot(h_sc[...], win_ref[:, z0:z0 + pw], preferred_element_type=F32)
        t_sc[:, psl] = u * _silu(z)

    mu = s1 * (1.0 / GMLP_WIDTH)
    s2 = jnp.zeros((tm, 1), F32)
    for g in range(GMLP_GROUPS):
        dv = v_sc[:, g * gw:(g + 1) * gw] - mu
        s2 = s2 + jnp.sum(dv * dv, axis=-1, keepdims=True)
    rstd = lax.rsqrt(s2 * (1.0 / GMLP_WIDTH) + LN_EPS)

    row = lax.broadcasted_iota(jnp.int32, (GMLP_CHUNK, GMLP_CHUNK), 0)
    col = lax.broadcasted_iota(jnp.int32, (GMLP_CHUNK, GMLP_CHUNK), 1)
    causal = col <= row
    for g in range(GMLP_GROUPS):
        sl = slice(g * gw, (g + 1) * gw)
        vn = ((v_sc[:, sl] - mu) * rstd * lng_ref[:, sl] + lnb_ref[:, sl]).astype(BF16)
        wm = jnp.where(causal, ws_ref[g], 0.0).astype(BF16)
        bias = bs_ref[:, g:g + 1]
        sv = jnp.concatenate(
            [jnp.dot(wm, vn[c * GMLP_CHUNK:(c + 1) * GMLP_CHUNK], preferred_element_type=F32) + bias
             for c in range(tm // GMLP_CHUNK)], axis=0)
        y_sc[:, sl] = (t_sc[:, sl] * sv).astype(BF16)

    out = jnp.dot(y_sc[...], wout_ref[...], preferred_element_type=F32)
    o_ref[...] = x_ref[...] + _rms_normalize(out, gpost_ref[...])


def _gmlp_layer(x2, gpre, gpost, w_in, ln_g, ln_b, w_s, b_s, w_out):
    n_tok = x2.shape[0]
    tm = GMLP_TOKENS
    const = lambda i: (0, 0)
    return pl.pallas_call(
        _gmlp_kernel,
        grid=(n_tok // tm,),
        in_specs=[
            pl.BlockSpec((tm, D_MODEL), lambda i: (i, 0)),
            pl.BlockSpec((1, D_MODEL), const),
            pl.BlockSpec((1, D_MODEL), const),
            pl.BlockSpec((D_MODEL, 3 * GMLP_WIDTH), const, pipeline_mode=pl.Buffered(1)),
            pl.BlockSpec((1, GMLP_WIDTH), const),
            pl.BlockSpec((1, GMLP_WIDTH), const),
            pl.BlockSpec((GMLP_GROUPS, GMLP_CHUNK, GMLP_CHUNK), lambda i: (0, 0, 0)),
            pl.BlockSpec((GMLP_CHUNK, GMLP_GROUPS), const),
            pl.BlockSpec((GMLP_WIDTH, D_MODEL), const, pipeline_mode=pl.Buffered(1)),
        ],
        out_specs=pl.BlockSpec((tm, D_MODEL), lambda i: (i, 0)),
        out_shape=jax.ShapeDtypeStruct(x2.shape, F32),
        scratch_shapes=[
            pltpu.VMEM((tm, D_MODEL), BF16),
            pltpu.VMEM((tm, GMLP_WIDTH), F32),
            pltpu.VMEM((tm, GMLP_WIDTH), F32),
            pltpu.VMEM((tm, GMLP_WIDTH), BF16),
        ],
        compiler_params=pltpu.CompilerParams(
            dimension_semantics=("arbitrary",), vmem_limit_bytes=VMEM_LIMIT),
        name="gmlp_layer",
    )(x2, gpre.reshape(1, -1), gpost.reshape(1, -1), w_in,
      ln_g.reshape(1, -1), ln_b.reshape(1, -1), w_s, jnp.transpose(b_s), w_out)


def _proj_kernel(x_ref, gpre_ref, w_ref, cos_ref, sin_ref, o_ref,
                 hs_sc, hs1_sc, h_sc, cos1_sc, sin1_sc, cos2_sc, sin2_sc):
    j = pl.program_id(1)
    n_chunks = TILE // PROJ_ROW_CHUNK
    n_slabs = D_MODEL // LANES
    first1, first2 = GROUP_FIRST_SLAB[1], GROUP_FIRST_SLAB[2]
    d1, d2 = DILATED_GROUPS[1][1], DILATED_GROUPS[2][1]
    ratio = d2 // d1

    @pl.when(j == 0)
    def _():
        for rc in range(n_chunks):
            rows = slice(rc * PROJ_ROW_CHUNK, (rc + 1) * PROJ_ROW_CHUNK)
            hs = _rms_normalize(x_ref[rows, :], gpre_ref[...])
            for c in range(n_slabs):
                hs_sc[c, rows, :] = hs[:, c * LANES:(c + 1) * LANES]

    @pl.when(j == first1)
    def _():
        rows = TILE // d1
        for b in range(d1):
            out, src = slice(b * rows, (b + 1) * rows), pl.ds(b, rows, stride=d1)
            for c in range(n_slabs):
                v = hs_sc[c, src, :]
                hs1_sc[c, out, :] = v
                h_sc[out, c * LANES:(c + 1) * LANES] = v.astype(BF16)
            cos1_sc[out, :] = cos_ref[src, :]
            sin1_sc[out, :] = sin_ref[src, :]

    @pl.when(j == first2)
    def _():
        rows = TILE // d2
        for b in range(d1):
            for a in range(ratio):
                r = a * d1 + b
                out, src = slice(r * rows, (r + 1) * rows), pl.ds(b * (TILE // d1) + a, rows, stride=ratio)
                for c in range(n_slabs):
                    h_sc[out, c * LANES:(c + 1) * LANES] = hs1_sc[c, src, :].astype(BF16)
                cos2_sc[out, :] = cos1_sc[src, :]
                sin2_sc[out, :] = sin1_sc[src, :]

    is_q = functools.reduce(jnp.logical_or, [j == s for s in GROUP_FIRST_SLAB])
    is_k = functools.reduce(jnp.logical_or, [j == s + 1 for s in GROUP_FIRST_SLAB])
    is_rope = jnp.logical_or(is_q, is_k)
    in_group0 = j < first1
    in_group1 = jnp.logical_and(j >= first1, j < first2)
    in_group2 = j >= first2

    def lhs_rows(from_token_order, rows):
        if from_token_order:
            return jnp.concatenate([hs_sc[c, rows, :] for c in range(n_slabs)], axis=1).astype(BF16)
        return h_sc[rows, :]

    def rope_slab(from_token_order, cos_tab, sin_tab):
        scale = jnp.where(is_q, np.float32(np.log2(np.e) / np.sqrt(HEAD_DIM)), np.float32(1.0))
        for rc in range(n_chunks):
            rows = slice(rc * PROJ_ROW_CHUNK, (rc + 1) * PROJ_ROW_CHUNK)
            p = jnp.dot(lhs_rows(from_token_order, rows), w_ref[...], preferred_element_type=F32)
            cos = cos_tab[rows, :] * scale
            sin = sin_tab[rows, :] * scale
            for s in range(ATTN_WIDTH // LANES):
                t = p[:, s * LANES:(s + 1) * LANES]
                o_ref[rows, s * LANES:(s + 1) * LANES] = (
                    t * cos + pltpu.roll(t, LANES // 2, 1) * sin).astype(BF16)

    def plain_slab(from_token_order):
        for rc in range(n_chunks):
            rows = slice(rc * PROJ_ROW_CHUNK, (rc + 1) * PROJ_ROW_CHUNK)
            o_ref[rows, :] = jnp.dot(lhs_rows(from_token_order, rows), w_ref[...],
                                     preferred_element_type=F32).astype(BF16)

    pl.when(jnp.logical_and(is_rope, in_group0))(functools.partial(rope_slab, True, cos_ref, sin_ref))
    pl.when(jnp.logical_and(is_rope, in_group1))(functools.partial(rope_slab, False, cos1_sc, sin1_sc))
    pl.when(jnp.logical_and(is_rope, in_group2))(functools.partial(rope_slab, False, cos2_sc, sin2_sc))
    is_plain = jnp.logical_not(is_rope)
    pl.when(jnp.logical_and(is_plain, in_group0))(functools.partial(plain_slab, True))
    pl.when(jnp.logical_and(is_plain, jnp.logical_not(in_group0)))(functools.partial(plain_slab, False))


def _proj_weight_kernel(w_ref, o_ref):
    s = pl.program_id(0)
    is_qk = functools.reduce(jnp.logical_or, [jnp.logical_or(s == f, s == f + 1) for f in GROUP_FIRST_SLAB])

    @pl.when(is_qk)
    def _():
        half = HEAD_DIM // 2
        quarter = lax.broadcasted_iota(jnp.int32, (1, LANES), 1) // half
        for c in range(ATTN_WIDTH // LANES):
            sl = slice(c * LANES, (c + 1) * LANES)
            w = w_ref[0, :, sl]
            w = jnp.where(quarter == 1, pltpu.roll(w, LANES - half, 1),
                          jnp.where(quarter == 2, pltpu.roll(w, half, 1), w))
            o_ref[:, sl] = w.astype(BF16)

    @pl.when(jnp.logical_not(is_qk))
    def _():
        o_ref[...] = w_ref[0].astype(BF16)


def _proj_weight(w_in_all, layer):
    def source_slab(s):
        return jnp.where(s == GATE_OUT_SLAB, Z_SLAB, jnp.where(s < GATE_OUT_SLAB, s, s - 1))

    return pl.pallas_call(
        _proj_weight_kernel,
        grid=(N_PROJ_SLABS,),
        in_specs=[pl.BlockSpec((1, D_MODEL, ATTN_WIDTH), lambda s: (layer, 0, source_slab(s)))],
        out_specs=pl.BlockSpec((D_MODEL, ATTN_WIDTH), lambda s: (0, s)),
        out_shape=jax.ShapeDtypeStruct((D_MODEL, N_PROJ_SLABS * ATTN_WIDTH), BF16),
        compiler_params=pltpu.CompilerParams(
            dimension_semantics=("arbitrary",), vmem_limit_bytes=VMEM_LIMIT),
        name="proj_weight_prep",
    )(w_in_all)


def _cast_kernel(w_ref, o_ref):
    o_ref[...] = w_ref[0].astype(BF16)


def _bf16_weight(w_all, layer):
    _, rows, cols = w_all.shape
    block_rows = rows
    while block_rows * cols * 4 > WEIGHT_CAST_BLOCK_BYTES:
        block_rows //= 2
    assert rows % block_rows == 0 and block_rows % 16 == 0
    return pl.pallas_call(
        _cast_kernel,
        grid=(rows // block_rows,),
        in_specs=[pl.BlockSpec((1, block_rows, cols), lambda i: (layer, i, 0))],
        out_specs=pl.BlockSpec((block_rows, cols), lambda i: (i, 0)),
        out_shape=jax.ShapeDtypeStruct((rows, cols), BF16),
        compiler_params=pltpu.CompilerParams(
            dimension_semantics=("arbitrary",), vmem_limit_bytes=VMEM_LIMIT),
        name="weight_cast",
    )(w_all)


def _proj_layer(x2, gpre, w_in, cos2, sin2, seq_len):
    n_tok = x2.shape[0]
    tiles_per_seq = seq_len // TILE
    return pl.pallas_call(
        _proj_kernel,
        grid=(n_tok // TILE, N_PROJ_SLABS),
        in_specs=[
            pl.BlockSpec((TILE, D_MODEL), lambda i, j: (i, 0)),
            pl.BlockSpec((1, D_MODEL), lambda i, j: (0, 0)),
            pl.BlockSpec((D_MODEL, ATTN_WIDTH), lambda i, j: (0, j)),
            pl.BlockSpec((TILE, LANES), lambda i, j: (i % tiles_per_seq, 0)),
            pl.BlockSpec((TILE, LANES), lambda i, j: (i % tiles_per_seq, 0)),
        ],
        out_specs=pl.BlockSpec((TILE, ATTN_WIDTH), lambda i, j: (i, j)),
        out_shape=jax.ShapeDtypeStruct((n_tok, N_PROJ_SLABS * ATTN_WIDTH), BF16),
        scratch_shapes=[
            pltpu.VMEM((D_MODEL // LANES, TILE, LANES), F32),
            pltpu.VMEM((D_MODEL // LANES, TILE, LANES), F32),
            pltpu.VMEM((TILE, D_MODEL), BF16),
        ] + 4 * [pltpu.VMEM((TILE, LANES), F32)],
        compiler_params=pltpu.CompilerParams(
            dimension_semantics=("arbitrary", "arbitrary"), vmem_limit_bytes=VMEM_LIMIT),
        name="attn_proj",
    )(x2, gpre.reshape(1, -1), w_in, cos2, sin2)


def _attn_masks():
    t = ATTN_BLOCK
    kj = np.arange(2 * t)[:, None]
    qc = np.arange(t)[None, :]
    cur = (kj >= t) & (kj - t <= qc)
    prev = (kj < t) & (kj >= qc)
    valid = np.stack([cur, cur | prev])
    return jnp.asarray(np.where(valid, 0.0, NEG_BIG), dtype=BF16)


def _pack_bf16_pair(a, b):
    hi = lax.bitcast_convert_type(a.astype(BF16).astype(F32), jnp.uint32)
    lo = lax.bitcast_convert_type(b.astype(BF16).astype(F32), jnp.uint32)
    return hi | (lo >> 16)


def _unpack_bf16_pair(w):
    hi = lax.bitcast_convert_type(w & jnp.uint32(0xFFFF0000), F32)
    lo = lax.bitcast_convert_type(w << 16, F32)
    return hi, lo


def _attn_kernel(q_ref, kp_ref, kc_ref, vp_ref, vc_ref, mask_ref, eye_ref, o_ref, st_ref, *, chain, dilation):
    t = ATTN_BLOCK
    n_words = o_ref.shape[0]
    first_step_mask = mask_ref[jnp.minimum(pl.program_id(1), 1)]
    eye = eye_ref[...]
    ones = jnp.ones((2 * t, LANES), BF16)
    lane = lax.broadcasted_iota(jnp.int32, (1, LANES), 1)
    head_a_qk = ((lane // (HEAD_DIM // 2)) % 2) == 0
    head_a_v = (lane // HEAD_DIM) == 0
    for jb in range(q_ref.shape[0] // t):
        rows = slice(jb * t, (jb + 1) * t)
        if jb % chain > 0:
            prev_k, prev_v, prev_rows, mask_t = kc_ref, vc_ref, slice((jb - 1) * t, jb * t), mask_ref[1]
        else:
            prev_k, prev_v, mask_t = kp_ref, vp_ref, first_step_mask
            start = 0 if kp_ref.shape[0] == t else (jb + chain - 1) * t
            prev_rows = slice(start, start + t)
        if dilation == 1:
            token_rows = rows
        else:
            token_rows = pl.ds(jb // chain + (jb % chain) * t * dilation, t, stride=dilation)
        st = jnp.zeros((t, LANES), F32)

        def head_pair(hp, st):
            sl = slice(hp * LANES, (hp + 1) * LANES)
            q2 = q_ref[rows, sl]
            zero = jnp.zeros_like(q2)
            lhs = jnp.concatenate([
                jnp.concatenate([jnp.where(head_a_qk, q2, zero), eye], axis=1),
                jnp.concatenate([jnp.where(head_a_qk, zero, q2), eye], axis=1)], axis=0)
            k_ext = jnp.concatenate(
                [jnp.concatenate([prev_k[prev_rows, sl], kc_ref[rows, sl]], axis=0), mask_t], axis=1)
            v_ext = jnp.concatenate(
                [jnp.concatenate([prev_v[prev_rows, sl], vc_ref[rows, sl]], axis=0), ones], axis=1)
            s = lax.dot_general(lhs, k_ext, (((1,), (1,)), ((), ())), preferred_element_type=F32)
            m = jnp.max(s, axis=-1, keepdims=True)
            p = jnp.exp2(s - m).astype(BF16)
            pv = jnp.dot(p, v_ext, preferred_element_type=F32)
            denom = pv[:, LANES:]
            st = jnp.where(lane == 2 * hp, m[:t], st)
            st = jnp.where(lane == 2 * hp + 1, m[t:], st)
            st = jnp.where(lane == N_HEADS + 2 * hp, denom[:t], st)
            st = jnp.where(lane == N_HEADS + 2 * hp + 1, denom[t:], st)
            return jnp.where(head_a_v, pv[:t, :LANES], pv[t:, :LANES]), st

        for w in range(n_words):
            o_hi, st = head_pair(w, st)
            o_lo, st = head_pair(w + n_words, st)
            o_ref[w, token_rows, :] = _pack_bf16_pair(o_hi, o_lo)
        st_ref[token_rows, :] = st


def _attn_group(proj2, g, dilation, batch, seq_len):
    n_tok = batch * seq_len
    t = ATTN_BLOCK
    tiles_per_seq = seq_len // TILE
    blocks_per_tile = TILE // t
    chain = blocks_per_tile // dilation
    first = GROUP_FIRST_SLAB[g]
    n_words = ATTN_WIDTH // LANES // 2
    tile = lambda b, n: b * tiles_per_seq + n
    if dilation == 1:
        prev_spec = lambda c: pl.BlockSpec(
            (t, ATTN_WIDTH), lambda b, n: (jnp.maximum(tile(b, n) * blocks_per_tile - 1, 0), first + c))
    else:
        prev_spec = lambda c: pl.BlockSpec(
            (TILE, ATTN_WIDTH), lambda b, n: (tile(b, jnp.maximum(n - 1, 0)), first + c))
    cur_spec = lambda c: pl.BlockSpec((TILE, ATTN_WIDTH), lambda b, n: (tile(b, n), first + c))

    return pl.pallas_call(
        functools.partial(_attn_kernel, chain=chain, dilation=dilation),
        grid=(batch, tiles_per_seq),
        in_specs=[cur_spec(0), prev_spec(1), cur_spec(1), prev_spec(2), cur_spec(2),
                  pl.BlockSpec((2, 2 * t, t), lambda b, n: (0, 0, 0)),
                  pl.BlockSpec((t, t), lambda b, n: (0, 0))],
        out_specs=[
            pl.BlockSpec((n_words, TILE, LANES), lambda b, n: (0, tile(b, n), 0)),
            pl.BlockSpec((TILE, LANES), lambda b, n: (tile(b, n), 0)),
        ],
        out_shape=[
            jax.ShapeDtypeStruct((n_words, n_tok, LANES), jnp.uint32),
            jax.ShapeDtypeStruct((n_tok, LANES), F32),
        ],
        compiler_params=pltpu.CompilerParams(
            dimension_semantics=("arbitrary", "arbitrary"), vmem_limit_bytes=VMEM_LIMIT),
        name=f"dilated_attn_g{g}",
    )(proj2, proj2, proj2, proj2, proj2, _attn_masks(), jnp.eye(t, dtype=BF16))


def _combine_kernel(x_ref, o0_ref, o1_ref, o2_ref, s0_ref, s1_ref, s2_ref, z_ref,
                    expand_ref, wout_ref, gpost_ref, out_ref, y_sc):
    n_words = o0_ref.shape[0]
    is_head_lane = lax.broadcasted_iota(jnp.int32, (1, LANES), 1) < N_HEADS
    for sub in range(COMBINE_TOKENS // COMBINE_SUB):
        rows = slice(sub * COMBINE_SUB, (sub + 1) * COMBINE_SUB)
        sts = [s0_ref[rows, :], s1_ref[rows, :], s2_ref[rows, :]]
        m = jnp.maximum(jnp.maximum(sts[0], sts[1]), sts[2])
        es = [jnp.exp2(s - m) for s in sts]
        total = None
        for e, s in zip(es, sts):
            term = e * pltpu.roll(s, LANES - N_HEADS, 1)
            total = term if total is None else total + term
        inv = 1.0 / jnp.where(is_head_lane, total, 1.0)
        alphas = []
        for e in es:
            alpha = jnp.where(is_head_lane, e * inv, 0.0)
            hi = alpha.astype(BF16)
            lo = (alpha - hi.astype(F32)).astype(BF16)
            alphas.append(jnp.dot(jnp.concatenate([hi, lo], axis=1), expand_ref[...],
                                  preferred_element_type=F32))
        for w in range(n_words):
            pairs = [_unpack_bf16_pair(o_ref[w, rows, :]) for o_ref in (o0_ref, o1_ref, o2_ref)]
            for half, c in enumerate((w, w + n_words)):
                sl = slice(c * LANES, (c + 1) * LANES)
                o = (alphas[0][:, sl] * pairs[0][half] + alphas[1][:, sl] * pairs[1][half]
                     + alphas[2][:, sl] * pairs[2][half])
                y_sc[rows, sl] = (o * _silu(z_ref[rows, sl].astype(F32))).astype(BF16)
        out = jnp.dot(y_sc[rows, :], wout_ref[...], preferred_element_type=F32)
        out_ref[rows, :] = x_ref[rows, :] + _rms_normalize(out, gpost_ref[...])


def _combine_layer(x2, outs, stats, proj2, w_out, gpost):
    n_tok = x2.shape[0]
    tm = COMBINE_TOKENS
    n_words = ATTN_WIDTH // LANES // 2
    head_of_lane = np.arange(ATTN_WIDTH) // HEAD_DIM
    expand = jnp.asarray(np.tile(np.arange(LANES)[:, None] == head_of_lane[None, :], (2, 1)), dtype=BF16)
    tok = lambda i: (i, 0)
    const = lambda i: (0, 0)
    return pl.pallas_call(
        _combine_kernel,
        grid=(n_tok // tm,),
        in_specs=[pl.BlockSpec((tm, D_MODEL), tok)]
        + 3 * [pl.BlockSpec((n_words, tm, LANES), lambda i: (0, i, 0))]
        + 3 * [pl.BlockSpec((tm, LANES), tok)]
        + [pl.BlockSpec((tm, ATTN_WIDTH), lambda i: (i, GATE_OUT_SLAB)),
           pl.BlockSpec((2 * LANES, ATTN_WIDTH), const),
           pl.BlockSpec((ATTN_WIDTH, D_MODEL), const),
           pl.BlockSpec((1, D_MODEL), const)],
        out_specs=pl.BlockSpec((tm, D_MODEL), tok),
        out_shape=jax.ShapeDtypeStruct(x2.shape, F32),
        scratch_shapes=[pltpu.VMEM((tm, ATTN_WIDTH), BF16)],
        compiler_params=pltpu.CompilerParams(
            dimension_semantics=("arbitrary",), vmem_limit_bytes=VMEM_LIMIT),
        name="attn_combine",
    )(x2, *outs, *stats, proj2, expand, w_out, gpost.reshape(1, -1))


def _rope_tables(seq_len):
    half = HEAD_DIM // 2
    exponent = np.arange(0, HEAD_DIM, 2, dtype=np.float32) / np.float32(HEAD_DIM)
    power = np.power(np.float64(ROPE_THETA), exponent.astype(np.float64)).astype(np.float32)
    inv_freq = np.float32(1.0) / power
    ang = np.arange(seq_len, dtype=np.float32)[:, None] * inv_freq[None, :]
    cos = np.cos(ang.astype(np.float64)).astype(np.float32)
    sin = np.sin(ang.astype(np.float64)).astype(np.float32)
    cos2 = np.tile(cos, (1, LANES // half))
    sin2 = np.concatenate([-sin, -sin, sin, sin], axis=1)
    return jnp.asarray(cos2), jnp.asarray(sin2)


def kernel(x, norm_pre, norm_post, a_w_in, a_ln_g, a_ln_b, a_w_s, a_b_s, a_w_out, b_w_in, b_w_out):
    batch, seq_len, d = x.shape
    depth = norm_pre.shape[0]
    cos2, sin2 = _rope_tables(seq_len)
    x2 = x.reshape(batch * seq_len, d)
    for i in range(depth):
        j = i // 2
        if i % 2 == 0:
            x2 = _gmlp_layer(x2, norm_pre[i], norm_post[i], _bf16_weight(a_w_in, j), a_ln_g[j], a_ln_b[j],
                             a_w_s[j], a_b_s[j], _bf16_weight(a_w_out, j))
        else:
            proj2 = _proj_layer(x2, norm_pre[i], _proj_weight(b_w_in, j), cos2, sin2, seq_len)
            outs, stats = [], []
            for g, (_, dilation) in enumerate(DILATED_GROUPS):
                o, st = _attn_group(proj2, g, dilation, batch, seq_len)
                outs.append(o)
                stats.append(st)
            x2 = _combine_layer(x2, outs, stats, proj2, _bf16_weight(b_w_out, j), norm_post[i])
    return x2.reshape(batch, seq_len, d)
```

```python
import functools

import numpy as np
import jax
import jax.numpy as jnp
from jax import lax
from jax.experimental import pallas as pl
from jax.experimental.pallas import tpu as pltpu

F32 = jnp.float32
BF16 = jnp.bfloat16

D_MODEL = 1024
RMS_EPS = 1e-6
LN_EPS = 1e-5

GMLP_WIDTH = 2048
GMLP_CHUNK = 128
GMLP_GROUPS = 8
GMLP_GROUP_WIDTH = GMLP_WIDTH // GMLP_GROUPS
GMLP_DOT_GROUPS = 4

HEAD_DIM = 64
N_HEADS = 16
ATTN_WIDTH = N_HEADS * HEAD_DIM
DILATED_GROUPS = ((128, 1), (512, 4), (2048, 16))
N_DIL_GROUPS = len(DILATED_GROUPS)
ATTN_BLOCK = 128
ATTN_QB = (16, 16, 16)
ROPE_THETA = 10000.0
N_PROJ_SLABS = 3 * N_DIL_GROUPS + 1
Z_SLAB = 3 * N_DIL_GROUPS
GROUP_FIRST_SLAB = (0, 4, 7)
GATE_OUT_SLAB = 3
TILE = 2048

LANES = 128
NEG_BIG = -1e30

GMLP_TOKENS = 512
PROJ_ROW_CHUNK = 512
COMBINE_TOKENS = 1024
COMBINE_SUB = 512
VMEM_LIMIT = 60 * 1024 * 1024
WEIGHT_CAST_BLOCK_BYTES = 4 * 1024 * 1024


def _gelu(x):
    return 0.5 * x * (1.0 + lax.erf(x * np.float32(np.sqrt(0.5))))


def _silu(x):
    return x * jax.nn.sigmoid(x)


def _rms_normalize(x, gain):
    ms = jnp.mean(x * x, axis=-1, keepdims=True)
    return x * lax.rsqrt(ms + RMS_EPS) * gain


def _gmlp_kernel(x_ref, gpre_ref, gpost_ref, win_ref, lng_ref, lnb_ref, ws_ref,
                 bs_ref, wout_ref, o_ref, h_sc, v_sc, t_sc, y_sc):
    tm = x_ref.shape[0]
    gw = GMLP_GROUP_WIDTH
    x = x_ref[...]
    h_sc[...] = _rms_normalize(x, gpre_ref[...]).astype(BF16)

    pw = GMLP_DOT_GROUPS * gw
    s1 = jnp.zeros((tm, 1), F32)
    for gp in range(GMLP_WIDTH // pw):
        c0 = GMLP_WIDTH + gp * pw
        v = _gelu(jnp.dot(h_sc[...], win_ref[:, c0:c0 + pw], preferred_element_type=F32))
        v_sc[:, gp * pw:(gp + 1) * pw] = v
        s1 = s1 + jnp.sum(v, axis=-1, keepdims=True)
    for gp in range(GMLP_WIDTH // pw):
        psl = slice(gp * pw, (gp + 1) * pw)
        u = _gelu(jnp.dot(h_sc[...], win_ref[:, psl], preferred_element_type=F32))
        z0 = 2 * GMLP_WIDTH + gp * pw
        z = jnp.dot(h_sc[...], win_ref[:, z0:z0 + pw], preferred_element_type=F32)
        t_sc[:, psl] = u * _silu(z)

    mu = s1 * (1.0 / GMLP_WIDTH)
    s2 = jnp.zeros((tm, 1), F32)
    for g in range(GMLP_GROUPS):
        dv = v_sc[:, g * gw:(g + 1) * gw] - mu
        s2 = s2 + jnp.sum(dv * dv, axis=-1, keepdims=True)
    rstd = lax.rsqrt(s2 * (1.0 / GMLP_WIDTH) + LN_EPS)

    row = lax.broadcasted_iota(jnp.int32, (GMLP_CHUNK, GMLP_CHUNK), 0)
    col = lax.broadcasted_iota(jnp.int32, (GMLP_CHUNK, GMLP_CHUNK), 1)
    causal = col <= row
    for g in range(GMLP_GROUPS):
        sl = slice(g * gw, (g + 1) * gw)
        vn = ((v_sc[:, sl] - mu) * rstd * lng_ref[:, sl] + lnb_ref[:, sl]).astype(BF16)
        wm = jnp.where(causal, ws_ref[g], 0.0).astype(BF16)
        bias = bs_ref[:, g:g + 1]
        sv = jnp.concatenate(
            [jnp.dot(wm, vn[c * GMLP_CHUNK:(c + 1) * GMLP_CHUNK], preferred_element_type=F32) + bias
             for c in range(tm // GMLP_CHUNK)], axis=0)
        y_sc[:, sl] = (t_sc[:, sl] * sv).astype(BF16)

    out = jnp.dot(y_sc[...], wout_ref[...], preferred_element_type=F32)
    o_ref[...] = x_ref[...] + _rms_normalize(out, gpost_ref[...])


def _gmlp_layer(x2, gpre, gpost, w_in, ln_g, ln_b, w_s, b_s, w_out):
    n_tok = x2.shape[0]
    tm = GMLP_TOKENS
    const = lambda i: (0, 0)
    return pl.pallas_call(
        _gmlp_kernel,
        grid=(n_tok // tm,),
        in_specs=[
            pl.BlockSpec((tm, D_MODEL), lambda i: (i, 0)),
            pl.BlockSpec((1, D_MODEL), const),
            pl.BlockSpec((1, D_MODEL), const),
            pl.BlockSpec((D_MODEL, 3 * GMLP_WIDTH), const, pipeline_mode=pl.Buffered(1)),
            pl.BlockSpec((1, GMLP_WIDTH), const),
            pl.BlockSpec((1, GMLP_WIDTH), const),
            pl.BlockSpec((GMLP_GROUPS, GMLP_CHUNK, GMLP_CHUNK), lambda i: (0, 0, 0)),
            pl.BlockSpec((GMLP_CHUNK, GMLP_GROUPS), const),
            pl.BlockSpec((GMLP_WIDTH, D_MODEL), const, pipeline_mode=pl.Buffered(1)),
        ],
        out_specs=pl.BlockSpec((tm, D_MODEL), lambda i: (i, 0)),
        out_shape=jax.ShapeDtypeStruct(x2.shape, F32),
        scratch_shapes=[
            pltpu.VMEM((tm, D_MODEL), BF16),
            pltpu.VMEM((tm, GMLP_WIDTH), F32),
            pltpu.VMEM((tm, GMLP_WIDTH), F32),
            pltpu.VMEM((tm, GMLP_WIDTH), BF16),
        ],
        compiler_params=pltpu.CompilerParams(
            dimension_semantics=("arbitrary",), vmem_limit_bytes=VMEM_LIMIT),
        name="gmlp_layer",
    )(x2, gpre.reshape(1, -1), gpost.reshape(1, -1), w_in,
      ln_g.reshape(1, -1), ln_b.reshape(1, -1), w_s, jnp.transpose(b_s), w_out)


def _proj_kernel(x_ref, gpre_ref, w_ref, cos_ref, sin_ref, o_ref,
                 hs_sc, hs1_sc, h_sc, cos1_sc, sin1_sc, cos2_sc, sin2_sc):
    j = pl.program_id(1)
    n_chunks = TILE // PROJ_ROW_CHUNK
    n_slabs = D_MODEL // LANES
    first1, first2 = GROUP_FIRST_SLAB[1], GROUP_FIRST_SLAB[2]
    d1, d2 = DILATED_GROUPS[1][1], DILATED_GROUPS[2][1]
    ratio = d2 // d1

    @pl.when(j == 0)
    def _():
        for rc in range(n_chunks):
            rows = slice(rc * PROJ_ROW_CHUNK, (rc + 1) * PROJ_ROW_CHUNK)
            hs = _rms_normalize(x_ref[rows, :], gpre_ref[...])
            for c in range(n_slabs):
                hs_sc[c, rows, :] = hs[:, c * LANES:(c + 1) * LANES]

    @pl.when(j == first1)
    def _():
        rows = TILE // d1
        for b in range(d1):
            out, src = slice(b * rows, (b + 1) * rows), pl.ds(b, rows, stride=d1)
            for c in range(n_slabs):
                v = hs_sc[c, src, :]
                hs1_sc[c, out, :] = v
                h_sc[out, c * LANES:(c + 1) * LANES] = v.astype(BF16)
            cos1_sc[out, :] = cos_ref[src, :]
            sin1_sc[out, :] = sin_ref[src, :]

    @pl.when(j == first2)
    def _():
        rows = TILE // d2
        for b in range(d1):
            for a in range(ratio):
                r = a * d1 + b
                out, src = slice(r * rows, (r + 1) * rows), pl.ds(b * (TILE // d1) + a, rows, stride=ratio)
                for c in range(n_slabs):
                    h_sc[out, c * LANES:(c + 1) * LANES] = hs1_sc[c, src, :].astype(BF16)
                cos2_sc[out, :] = cos1_sc[src, :]
                sin2_sc[out, :] = sin1_sc[src, :]

    is_q = functools.reduce(jnp.logical_or, [j == s for s in GROUP_FIRST_SLAB])
    is_k = functools.reduce(jnp.logical_or, [j == s + 1 for s in GROUP_FIRST_SLAB])
    is_rope = jnp.logical_or(is_q, is_k)
    in_group0 = j < first1
    in_group1 = jnp.logical_and(j >= first1, j < first2)
    in_group2 = j >= first2

    def lhs_rows(from_token_order, rows):
        if from_token_order:
            return jnp.concatenate([hs_sc[c, rows, :] for c in range(n_slabs)], axis=1).astype(BF16)
        return h_sc[rows, :]

    def rope_slab(from_token_order, cos_tab, sin_tab):
        scale = jnp.where(is_q, np.float32(np.log2(np.e) / np.sqrt(HEAD_DIM)), np.float32(1.0))
        for rc in range(n_chunks):
            rows = slice(rc * PROJ_ROW_CHUNK, (rc + 1) * PROJ_ROW_CHUNK)
            p = jnp.dot(lhs_rows(from_token_order, rows), w_ref[...], preferred_element_type=F32)
            cos = cos_tab[rows, :] * scale
            sin = sin_tab[rows, :] * scale
            for s in range(ATTN_WIDTH // LANES):
                t = p[:, s * LANES:(s + 1) * LANES]
                o_ref[rows, s * LANES:(s + 1) * LANES] = (
                    t * cos + pltpu.roll(t, LANES // 2, 1) * sin).astype(BF16)

    def plain_slab(from_token_order):
        for rc in range(n_chunks):
            rows = slice(rc * PROJ_ROW_CHUNK, (rc + 1) * PROJ_ROW_CHUNK)
            o_ref[rows, :] = jnp.dot(lhs_rows(from_token_order, rows), w_ref[...],
                                     preferred_element_type=F32).astype(BF16)

    pl.when(jnp.logical_and(is_rope, in_group0))(functools.partial(rope_slab, True, cos_ref, sin_ref))
    pl.when(jnp.logical_and(is_rope, in_group1))(functools.partial(rope_slab, False, cos1_sc, sin1_sc))
    pl.when(jnp.logical_and(is_rope, in_group2))(functools.partial(rope_slab, False, cos2_sc, sin2_sc))
    is_plain = jnp.logical_not(is_rope)
    pl.when(jnp.logical_and(is_plain, in_group0))(functools.partial(plain_slab, True))
    pl.when(jnp.logical_and(is_plain, jnp.logical_not(in_group0)))(functools.partial(plain_slab, False))


def _proj_weight_kernel(w_ref, o_ref):
    s = pl.program_id(0)
    is_qk = functools.reduce(jnp.logical_or, [jnp.logical_or(s == f, s == f + 1) for f in GROUP_FIRST_SLAB])

    @pl.when(is_qk)
    def _():
        half = HEAD_DIM // 2
        quarter = lax.broadcasted_iota(jnp.int32, (1, LANES), 1) // half
        for c in range(ATTN_WIDTH // LANES):
            sl = slice(c * LANES, (c + 1) * LANES)
            w = w_ref[0, :, sl]
            w = jnp.where(quarter == 1, pltpu.roll(w, LANES - half, 1),
                          jnp.where(quarter == 2, pltpu.roll(w, half, 1), w))
            o_ref[:, sl] = w.astype(BF16)

    @pl.when(jnp.logical_not(is_qk))
    def _():
        o_ref[...] = w_ref[0].astype(BF16)


def _proj_weight(w_in_all, layer):
    def source_slab(s):
        return jnp.where(s == GATE_OUT_SLAB, Z_SLAB, jnp.where(s < GATE_OUT_SLAB, s, s - 1))

    return pl.pallas_call(
        _proj_weight_kernel,
        grid=(N_PROJ_SLABS,),
        in_specs=[pl.BlockSpec((1, D_MODEL, ATTN_WIDTH), lambda s: (layer, 0, source_slab(s)))],
        out_specs=pl.BlockSpec((D_MODEL, ATTN_WIDTH), lambda s: (0, s)),
        out_shape=jax.ShapeDtypeStruct((D_MODEL, N_PROJ_SLABS * ATTN_WIDTH), BF16),
        compiler_params=pltpu.CompilerParams(
            dimension_semantics=("arbitrary",), vmem_limit_bytes=VMEM_LIMIT),
        name="proj_weight_prep",
    )(w_in_all)


def _cast_kernel(w_ref, o_ref):
    o_ref[...] = w_ref[0].astype(BF16)


def _bf16_weight(w_all, layer):
    _, rows, cols = w_all.shape
    block_rows = rows
    while block_rows * cols * 4 > WEIGHT_CAST_BLOCK_BYTES:
        block_rows //= 2
    assert rows % block_rows == 0 and block_rows % 16 == 0
    return pl.pallas_call(
        _cast_kernel,
        grid=(rows // block_rows,),
        in_specs=[pl.BlockSpec((1, block_rows, cols), lambda i: (layer, i, 0))],
        out_specs=pl.BlockSpec((block_rows, cols), lambda i: (i, 0)),
        out_shape=jax.ShapeDtypeStruct((rows, cols), BF16),
        compiler_params=pltpu.CompilerParams(
            dimension_semantics=("arbitrary",), vmem_limit_bytes=VMEM_LIMIT),
        name="weight_cast",
    )(w_all)


def _proj_layer(x2, gpre, w_in, cos2, sin2, seq_len):
    n_tok = x2.shape[0]
    tiles_per_seq = seq_len // TILE
    return pl.pallas_call(
        _proj_kernel,
        grid=(n_tok // TILE, N_PROJ_SLABS),
        in_specs=[
            pl.BlockSpec((TILE, D_MODEL), lambda i, j: (i, 0)),
            pl.BlockSpec((1, D_MODEL), lambda i, j: (0, 0)),
            pl.BlockSpec((D_MODEL, ATTN_WIDTH), lambda i, j: (0, j)),
            pl.BlockSpec((TILE, LANES), lambda i, j: (i % tiles_per_seq, 0)),
            pl.BlockSpec((TILE, LANES), lambda i, j: (i % tiles_per_seq, 0)),
        ],
        out_specs=pl.BlockSpec((TILE, ATTN_WIDTH), lambda i, j: (i, j)),
        out_shape=jax.ShapeDtypeStruct((n_tok, N_PROJ_SLABS * ATTN_WIDTH), BF16),
        scratch_shapes=[
            pltpu.VMEM((D_MODEL // LANES, TILE, LANES), F32),
            pltpu.VMEM((D_MODEL // LANES, TILE, LANES), F32),
            pltpu.VMEM((TILE, D_MODEL), BF16),
        ] + 4 * [pltpu.VMEM((TILE, LANES), F32)],
        compiler_params=pltpu.CompilerParams(
            dimension_semantics=("arbitrary", "arbitrary"), vmem_limit_bytes=VMEM_LIMIT),
        name="attn_proj",
    )(x2, gpre.reshape(1, -1), w_in, cos2, sin2)


def _attn_masks():
    t = ATTN_BLOCK
    kj = np.arange(2 * t)[:, None]
    qc = np.arange(t)[None, :]
    cur = (kj >= t) & (kj - t <= qc)
    prev = (kj < t) & (kj >= qc)
    valid = np.stack([cur, cur | prev])
    return jnp.asarray(np.where(valid, 0.0, NEG_BIG), dtype=BF16)


def _attn_kernel(q_ref, kp_ref, kc_ref, vp_ref, vc_ref, mask_ref, eye_ref, o_ref, st_ref, *, chain):
    t = ATTN_BLOCK
    first_step_mask = mask_ref[jnp.minimum(pl.program_id(2), 1)]
    eye = eye_ref[...]
    ones = jnp.ones((2 * t, LANES), BF16)
    lane = lax.broadcasted_iota(jnp.int32, (1, LANES), 1)
    head_a_qk = ((lane // (HEAD_DIM // 2)) % 2) == 0
    head_a_v = (lane // HEAD_DIM) == 0
    for jb in range(q_ref.shape[0] // t):
        rows = slice(jb * t, (jb + 1) * t)
        if jb % chain > 0:
            prev_k, prev_v, prev_rows, mask_t = kc_ref, vc_ref, slice((jb - 1) * t, jb * t), mask_ref[1]
        else:
            prev_k, prev_v, mask_t = kp_ref, vp_ref, first_step_mask
            start = 0 if kp_ref.shape[0] == t else (jb + chain - 1) * t
            prev_rows = slice(start, start + t)
        st = jnp.zeros((t, LANES), F32)
        for hp in range(N_HEADS // 2):
            sl = slice(hp * LANES, (hp + 1) * LANES)
            q2 = q_ref[rows, sl]
            zero = jnp.zeros_like(q2)
            lhs = jnp.concatenate([
                jnp.concatenate([jnp.where(head_a_qk, q2, zero), eye], axis=1),
                jnp.concatenate([jnp.where(head_a_qk, zero, q2), eye], axis=1)], axis=0)
            k_ext = jnp.concatenate(
                [jnp.concatenate([prev_k[prev_rows, sl], kc_ref[rows, sl]], axis=0), mask_t], axis=1)
            v_ext = jnp.concatenate(
                [jnp.concatenate([prev_v[prev_rows, sl], vc_ref[rows, sl]], axis=0), ones], axis=1)
            s = lax.dot_general(lhs, k_ext, (((1,), (1,)), ((), ())), preferred_element_type=F32)
            m = jnp.max(s, axis=-1, keepdims=True)
            p = jnp.exp2(s - m).astype(BF16)
            pv = jnp.dot(p, v_ext, preferred_element_type=F32)
            o_ref[rows, sl] = jnp.where(head_a_v, pv[:t, :LANES], pv[t:, :LANES]).astype(BF16)
            denom = pv[:, LANES:]
            st = jnp.where(lane == 2 * hp, m[:t], st)
            st = jnp.where(lane == 2 * hp + 1, m[t:], st)
            st = jnp.where(lane == N_HEADS + 2 * hp, denom[:t], st)
            st = jnp.where(lane == N_HEADS + 2 * hp + 1, denom[t:], st)
        st_ref[rows, :] = st


def _attn_group(proj2, g, dilation, batch, seq_len):
    n_tok = batch * seq_len
    t, qb = ATTN_BLOCK, ATTN_QB[g]
    n_blocks = seq_len // dilation // t
    tiles_per_seq = seq_len // TILE
    blocks_per_tile = TILE // t
    bpc = blocks_per_tile // dilation
    first = GROUP_FIRST_SLAB[g]
    if bpc >= qb:
        assert bpc % qb == 0
        chain = qb
        grid = (batch, dilation, n_blocks // qb)

        def row_block(b, r, n):
            return (b * tiles_per_seq + n // bpc) * blocks_per_tile + r * bpc + n % bpc

        cur = lambda b, r, n: row_block(b, r, qb * n) // qb
        prev = lambda b, r, n: row_block(b, r, jnp.maximum(qb * n - 1, 0))
        prev_rows = t
    else:
        assert qb % bpc == 0 and blocks_per_tile % qb == 0
        chain = bpc
        grid = (batch, blocks_per_tile // qb, n_blocks // bpc)
        cur = lambda b, r, n: (b * tiles_per_seq + n) * (blocks_per_tile // qb) + r
        prev = lambda b, r, n: cur(b, r, jnp.maximum(n - 1, 0))
        prev_rows = qb * t

    def spec(c, is_prev):
        if is_prev:
            return pl.BlockSpec((prev_rows, ATTN_WIDTH), lambda b, r, n: (prev(b, r, n), first + c))
        return pl.BlockSpec((qb * t, ATTN_WIDTH), lambda b, r, n: (cur(b, r, n), first + c))

    return pl.pallas_call(
        functools.partial(_attn_kernel, chain=chain),
        grid=grid,
        in_specs=[spec(0, False), spec(1, True), spec(1, False), spec(2, True), spec(2, False),
                  pl.BlockSpec((2, 2 * t, t), lambda b, r, n: (0, 0, 0)),
                  pl.BlockSpec((t, t), lambda b, r, n: (0, 0))],
        out_specs=[
            pl.BlockSpec((qb * t, ATTN_WIDTH), lambda b, r, n: (cur(b, r, n), 0)),
            pl.BlockSpec((qb * t, LANES), lambda b, r, n: (cur(b, r, n), 0)),
        ],
        out_shape=[
            jax.ShapeDtypeStruct((n_tok, ATTN_WIDTH), BF16),
            jax.ShapeDtypeStruct((n_tok, LANES), F32),
        ],
        compiler_params=pltpu.CompilerParams(
            dimension_semantics=("arbitrary", "arbitrary", "arbitrary"), vmem_limit_bytes=VMEM_LIMIT),
        name=f"dilated_attn_g{g}",
    )(proj2, proj2, proj2, proj2, proj2, _attn_masks(), jnp.eye(t, dtype=BF16))


def _combine_kernel(x_ref, o0_ref, o1_ref, o2_ref, s0_ref, s1_ref, s2_ref, z_ref,
                    expand_ref, wout_ref, gpost_ref, out_ref, on1_sc, on2_sc, tmp_sc, st1_sc, st2_sc,
                    tmp_st_sc, y_sc):
    n_slabs = ATTN_WIDTH // LANES
    is_head_lane = lax.broadcasted_iota(jnp.int32, (1, LANES), 1) < N_HEADS
    for sub in range(COMBINE_TOKENS // COMBINE_SUB):
        base = sub * COMBINE_SUB
        rows = slice(base, base + COMBINE_SUB)
        d1, d2 = DILATED_GROUPS[1][1], DILATED_GROUPS[2][1]
        n = COMBINE_SUB // d1
        src = slice(sub * n, (sub + 1) * n)
        for b in range(d1):
            dst = pl.ds(base + b, n, stride=d1)
            st1_sc[dst, :] = s1_ref[0, b, src, :]
            for c in range(n_slabs):
                on1_sc[c, dst, :] = o1_ref[0, b, src, c * LANES:(c + 1) * LANES].astype(F32)
        ratio = d2 // d1
        n2 = COMBINE_SUB // d2
        src2 = slice(sub * n2, (sub + 1) * n2)
        for b in range(d1):
            mid = slice(base + b * n, base + (b + 1) * n)
            for a in range(ratio):
                dst = pl.ds(base + b * n + a, n2, stride=ratio)
                tmp_st_sc[dst, :] = s2_ref[0, a * d1 + b, src2, :]
                for c in range(n_slabs):
                    tmp_sc[c, dst, :] = o2_ref[0, a * d1 + b, src2, c * LANES:(c + 1) * LANES].astype(F32)
            dst = pl.ds(base + b, n, stride=d1)
            st2_sc[dst, :] = tmp_st_sc[mid, :]
            for c in range(n_slabs):
                on2_sc[c, dst, :] = tmp_sc[c, mid, :]

        sts = [s0_ref[rows, :], st1_sc[rows, :], st2_sc[rows, :]]
        m = jnp.maximum(jnp.maximum(sts[0], sts[1]), sts[2])
        es = [jnp.exp2(s - m) for s in sts]
        total = None
        for e, s in zip(es, sts):
            term = e * pltpu.roll(s, LANES - N_HEADS, 1)
            total = term if total is None else total + term
        inv = 1.0 / jnp.where(is_head_lane, total, 1.0)
        alphas = []
        for e in es:
            alpha = jnp.where(is_head_lane, e * inv, 0.0)
            hi = alpha.astype(BF16)
            lo = (alpha - hi.astype(F32)).astype(BF16)
            alphas.append(jnp.dot(jnp.concatenate([hi, lo], axis=1), expand_ref[...],
                                  preferred_element_type=F32))
        for c in range(n_slabs):
            sl = slice(c * LANES, (c + 1) * LANES)
            o = (alphas[0][:, sl] * o0_ref[rows, sl].astype(F32)
                 + alphas[1][:, sl] * on1_sc[c, rows, :] + alphas[2][:, sl] * on2_sc[c, rows, :])
            y_sc[rows, sl] = (o * _silu(z_ref[rows, sl].astype(F32))).astype(BF16)
        out = jnp.dot(y_sc[rows, :], wout_ref[...], preferred_element_type=F32)
        out_ref[rows, :] = x_ref[rows, :] + _rms_normalize(out, gpost_ref[...])


def _combine_layer(x2, outs, stats, proj2, w_out, gpost):
    n_tok = x2.shape[0]
    tm = COMBINE_TOKENS
    n_tiles = n_tok // TILE
    per_tile = TILE // tm
    head_of_lane = np.arange(ATTN_WIDTH) // HEAD_DIM
    expand = jnp.asarray(np.tile(np.arange(LANES)[:, None] == head_of_lane[None, :], (2, 1)), dtype=BF16)
    tok = lambda i: (i, 0)
    const = lambda i: (0, 0)
    in_tile = lambda i: (i // per_tile, 0, i % per_tile, 0)
    d1, d2 = DILATED_GROUPS[1][1], DILATED_GROUPS[2][1]
    return pl.pallas_call(
        _combine_kernel,
        grid=(n_tok // tm,),
        in_specs=[
            pl.BlockSpec((tm, D_MODEL), tok),
            pl.BlockSpec((tm, ATTN_WIDTH), tok),
            pl.BlockSpec((1, d1, tm // d1, ATTN_WIDTH), in_tile),
            pl.BlockSpec((1, d2, tm // d2, ATTN_WIDTH), in_tile),
            pl.BlockSpec((tm, LANES), tok),
            pl.BlockSpec((1, d1, tm // d1, LANES), in_tile),
            pl.BlockSpec((1, d2, tm // d2, LANES), in_tile),
            pl.BlockSpec((tm, ATTN_WIDTH), lambda i: (i, GATE_OUT_SLAB)),
            pl.BlockSpec((2 * LANES, ATTN_WIDTH), const),
            pl.BlockSpec((ATTN_WIDTH, D_MODEL), const),
            pl.BlockSpec((1, D_MODEL), const),
        ],
        out_specs=pl.BlockSpec((tm, D_MODEL), tok),
        out_shape=jax.ShapeDtypeStruct(x2.shape, F32),
        scratch_shapes=[
            pltpu.VMEM((ATTN_WIDTH // LANES, tm, LANES), F32),
            pltpu.VMEM((ATTN_WIDTH // LANES, tm, LANES), F32),
            pltpu.VMEM((ATTN_WIDTH // LANES, tm, LANES), F32),
            pltpu.VMEM((tm, LANES), F32),
            pltpu.VMEM((tm, LANES), F32),
            pltpu.VMEM((tm, LANES), F32),
            pltpu.VMEM((tm, ATTN_WIDTH), BF16),
        ],
        compiler_params=pltpu.CompilerParams(
            dimension_semantics=("arbitrary",), vmem_limit_bytes=VMEM_LIMIT),
        name="attn_combine",
    )(x2,
      outs[0],
      outs[1].reshape(n_tiles, d1, TILE // d1, ATTN_WIDTH),
      outs[2].reshape(n_tiles, d2, TILE // d2, ATTN_WIDTH),
      stats[0],
      stats[1].reshape(n_tiles, d1, TILE // d1, LANES),
      stats[2].reshape(n_tiles, d2, TILE // d2, LANES),
      proj2, expand, w_out, gpost.reshape(1, -1))


def _rope_tables(seq_len):
    half = HEAD_DIM // 2
    exponent = np.arange(0, HEAD_DIM, 2, dtype=np.float32) / np.float32(HEAD_DIM)
    power = np.power(np.float64(ROPE_THETA), exponent.astype(np.float64)).astype(np.float32)
    inv_freq = np.float32(1.0) / power
    ang = np.arange(seq_len, dtype=np.float32)[:, None] * inv_freq[None, :]
    cos = np.cos(ang.astype(np.float64)).astype(np.float32)
    sin = np.sin(ang.astype(np.float64)).astype(np.float32)
    cos2 = np.tile(cos, (1, LANES // half))
    sin2 = np.concatenate([-sin, -sin, sin, sin], axis=1)
    return jnp.asarray(cos2), jnp.asarray(sin2)


def kernel(x, norm_pre, norm_post, a_w_in, a_ln_g, a_ln_b, a_w_s, a_b_s, a_w_out, b_w_in, b_w_out):
    batch, seq_len, d = x.shape
    depth = norm_pre.shape[0]
    cos2, sin2 = _rope_tables(seq_len)
    x2 = x.reshape(batch * seq_len, d)
    for i in range(depth):
        j = i // 2
        if i % 2 == 0:
            x2 = _gmlp_layer(x2, norm_pre[i], norm_post[i], _bf16_weight(a_w_in, j), a_ln_g[j], a_ln_b[j],
                             a_w_s[j], a_b_s[j], _bf16_weight(a_w_out, j))
        else:
            proj2 = _proj_layer(x2, norm_pre[i], _proj_weight(b_w_in, j), cos2, sin2, seq_len)
            outs, stats = [], []
            for g, (_, dilation) in enumerate(DILATED_GROUPS):
                o, st = _attn_group(proj2, g, dilation, batch, seq_len)
                outs.append(o)
                stats.append(st)
            x2 = _combine_layer(x2, outs, stats, proj2, _bf16_weight(b_w_out, j), norm_post[i])
    return x2.reshape(batch, seq_len, d)
```

```python
import functools

import numpy as np
import jax
import jax.numpy as jnp
from jax import lax
from jax.experimental import pallas as pl
from jax.experimental.pallas import tpu as pltpu

F32 = jnp.float32
BF16 = jnp.bfloat16

D_MODEL = 1024
RMS_EPS = 1e-6
LN_EPS = 1e-5

GMLP_WIDTH = 2048
GMLP_CHUNK = 128
GMLP_GROUPS = 8
GMLP_GROUP_WIDTH = GMLP_WIDTH // GMLP_GROUPS
GMLP_DOT_GROUPS = 4

HEAD_DIM = 64
N_HEADS = 16
ATTN_WIDTH = N_HEADS * HEAD_DIM
DILATED_GROUPS = ((128, 1), (512, 4), (2048, 16))
N_DIL_GROUPS = len(DILATED_GROUPS)
ATTN_BLOCK = 128
ATTN_QB = (16, 16, 16)
ROPE_THETA = 10000.0
N_PROJ_SLABS = 3 * N_DIL_GROUPS + 1
Z_SLAB = 3 * N_DIL_GROUPS
GROUP_FIRST_SLAB = (0, 4, 7)
GATE_OUT_SLAB = 3
TILE = 2048

LANES = 128
NEG_BIG = -1e30

GMLP_TOKENS = 1024
GMLP_SUB = 512
PROJ_ROW_CHUNK = 512
COMBINE_TOKENS = 1024
COMBINE_SUB = 512
VMEM_LIMIT = 60 * 1024 * 1024
WEIGHT_CAST_BLOCK_BYTES = 4 * 1024 * 1024


def _gelu(x):
    return 0.5 * x * (1.0 + lax.erf(x * np.float32(np.sqrt(0.5))))


def _silu(x):
    return x * jax.nn.sigmoid(x)


def _rms_normalize(x, gain):
    ms = jnp.mean(x * x, axis=-1, keepdims=True)
    return x * lax.rsqrt(ms + RMS_EPS) * gain


def _gmlp_kernel(x_ref, gpre_ref, gpost_ref, win_ref, lng_ref, lnb_ref, ws_ref,
                 bs_ref, wout_ref, o_ref, h_sc, v_sc, t_sc, y_sc):
    gw = GMLP_GROUP_WIDTH
    pw = GMLP_DOT_GROUPS * gw
    row = lax.broadcasted_iota(jnp.int32, (GMLP_CHUNK, GMLP_CHUNK), 0)
    col = lax.broadcasted_iota(jnp.int32, (GMLP_CHUNK, GMLP_CHUNK), 1)
    causal = col <= row
    for part in range(GMLP_TOKENS // GMLP_SUB):
        rows = slice(part * GMLP_SUB, (part + 1) * GMLP_SUB)
        h_sc[rows, :] = _rms_normalize(x_ref[rows, :], gpre_ref[...]).astype(BF16)

        s1 = jnp.zeros((GMLP_SUB, 1), F32)
        for gp in range(GMLP_WIDTH // pw):
            c0 = GMLP_WIDTH + gp * pw
            v = _gelu(jnp.dot(h_sc[rows, :], win_ref[:, c0:c0 + pw], preferred_element_type=F32))
            v_sc[rows, gp * pw:(gp + 1) * pw] = v
            s1 = s1 + jnp.sum(v, axis=-1, keepdims=True)
        for gp in range(GMLP_WIDTH // pw):
            psl = slice(gp * pw, (gp + 1) * pw)
            u = _gelu(jnp.dot(h_sc[rows, :], win_ref[:, psl], preferred_element_type=F32))
            z0 = 2 * GMLP_WIDTH + gp * pw
            z = jnp.dot(h_sc[rows, :], win_ref[:, z0:z0 + pw], preferred_element_type=F32)
            t_sc[rows, psl] = u * _silu(z)

        mu = s1 * (1.0 / GMLP_WIDTH)
        s2 = jnp.zeros((GMLP_SUB, 1), F32)
        for g in range(GMLP_GROUPS):
            dv = v_sc[rows, g * gw:(g + 1) * gw] - mu
            s2 = s2 + jnp.sum(dv * dv, axis=-1, keepdims=True)
        rstd = lax.rsqrt(s2 * (1.0 / GMLP_WIDTH) + LN_EPS)

        for g in range(GMLP_GROUPS):
            sl = slice(g * gw, (g + 1) * gw)
            vn = ((v_sc[rows, sl] - mu) * rstd * lng_ref[:, sl] + lnb_ref[:, sl]).astype(BF16)
            wm = jnp.where(causal, ws_ref[g], 0.0).astype(BF16)
            bias = bs_ref[:, g:g + 1]
            sv = jnp.concatenate(
                [jnp.dot(wm, vn[c * GMLP_CHUNK:(c + 1) * GMLP_CHUNK], preferred_element_type=F32) + bias
                 for c in range(GMLP_SUB // GMLP_CHUNK)], axis=0)
            y_sc[rows, sl] = (t_sc[rows, sl] * sv).astype(BF16)

        out = jnp.dot(y_sc[rows, :], wout_ref[...], preferred_element_type=F32)
        o_ref[rows, :] = x_ref[rows, :] + _rms_normalize(out, gpost_ref[...])


def _gmlp_layer(x2, gpre, gpost, w_in, ln_g, ln_b, w_s, b_s, w_out):
    n_tok = x2.shape[0]
    tm = GMLP_TOKENS
    const = lambda i: (0, 0)
    return pl.pallas_call(
        _gmlp_kernel,
        grid=(n_tok // tm,),
        in_specs=[
            pl.BlockSpec((tm, D_MODEL), lambda i: (i, 0)),
            pl.BlockSpec((1, D_MODEL), const),
            pl.BlockSpec((1, D_MODEL), const),
            pl.BlockSpec((D_MODEL, 3 * GMLP_WIDTH), const, pipeline_mode=pl.Buffered(1)),
            pl.BlockSpec((1, GMLP_WIDTH), const),
            pl.BlockSpec((1, GMLP_WIDTH), const),
            pl.BlockSpec((GMLP_GROUPS, GMLP_CHUNK, GMLP_CHUNK), lambda i: (0, 0, 0)),
            pl.BlockSpec((GMLP_CHUNK, GMLP_GROUPS), const),
            pl.BlockSpec((GMLP_WIDTH, D_MODEL), const, pipeline_mode=pl.Buffered(1)),
        ],
        out_specs=pl.BlockSpec((tm, D_MODEL), lambda i: (i, 0)),
        out_shape=jax.ShapeDtypeStruct(x2.shape, F32),
        scratch_shapes=[
            pltpu.VMEM((tm, D_MODEL), BF16),
            pltpu.VMEM((tm, GMLP_WIDTH), F32),
            pltpu.VMEM((tm, GMLP_WIDTH), F32),
            pltpu.VMEM((tm, GMLP_WIDTH), BF16),
        ],
        compiler_params=pltpu.CompilerParams(
            dimension_semantics=("arbitrary",), vmem_limit_bytes=VMEM_LIMIT),
        name="gmlp_layer",
    )(x2, gpre.reshape(1, -1), gpost.reshape(1, -1), w_in,
      ln_g.reshape(1, -1), ln_b.reshape(1, -1), w_s, jnp.transpose(b_s), w_out)


def _proj_kernel(x_ref, gpre_ref, w_ref, cos_ref, sin_ref, o_ref,
                 hs_sc, hs1_sc, h_sc, cos1_sc, sin1_sc, cos2_sc, sin2_sc):
    j = pl.program_id(1)
    n_chunks = TILE // PROJ_ROW_CHUNK
    n_slabs = D_MODEL // LANES
    first1, first2 = GROUP_FIRST_SLAB[1], GROUP_FIRST_SLAB[2]
    d1, d2 = DILATED_GROUPS[1][1], DILATED_GROUPS[2][1]
    ratio = d2 // d1

    @pl.when(j == 0)
    def _():
        for rc in range(n_chunks):
            rows = slice(rc * PROJ_ROW_CHUNK, (rc + 1) * PROJ_ROW_CHUNK)
            hs = _rms_normalize(x_ref[rows, :], gpre_ref[...])
            for c in range(n_slabs):
                hs_sc[c, rows, :] = hs[:, c * LANES:(c + 1) * LANES]

    @pl.when(j == first1)
    def _():
        rows = TILE // d1
        for b in range(d1):
            out, src = slice(b * rows, (b + 1) * rows), pl.ds(b, rows, stride=d1)
            for c in range(n_slabs):
                v = hs_sc[c, src, :]
                hs1_sc[c, out, :] = v
                h_sc[out, c * LANES:(c + 1) * LANES] = v.astype(BF16)
            cos1_sc[out, :] = cos_ref[src, :]
            sin1_sc[out, :] = sin_ref[src, :]

    @pl.when(j == first2)
    def _():
        rows = TILE // d2
        for b in range(d1):
            for a in range(ratio):
                r = a * d1 + b
                out, src = slice(r * rows, (r + 1) * rows), pl.ds(b * (TILE // d1) + a, rows, stride=ratio)
                for c in range(n_slabs):
                    h_sc[out, c * LANES:(c + 1) * LANES] = hs1_sc[c, src, :].astype(BF16)
                cos2_sc[out, :] = cos1_sc[src, :]
                sin2_sc[out, :] = sin1_sc[src, :]

    is_q = functools.reduce(jnp.logical_or, [j == s for s in GROUP_FIRST_SLAB])
    is_k = functools.reduce(jnp.logical_or, [j == s + 1 for s in GROUP_FIRST_SLAB])
    is_rope = jnp.logical_or(is_q, is_k)
    in_group0 = j < first1
    in_group1 = jnp.logical_and(j >= first1, j < first2)
    in_group2 = j >= first2

    def lhs_rows(from_token_order, rows):
        if from_token_order:
            return jnp.concatenate([hs_sc[c, rows, :] for c in range(n_slabs)], axis=1).astype(BF16)
        return h_sc[rows, :]

    def rope_slab(from_token_order, cos_tab, sin_tab):
        scale = jnp.where(is_q, np.float32(np.log2(np.e) / np.sqrt(HEAD_DIM)), np.float32(1.0))
        for rc in range(n_chunks):
            rows = slice(rc * PROJ_ROW_CHUNK, (rc + 1) * PROJ_ROW_CHUNK)
            p = jnp.dot(lhs_rows(from_token_order, rows), w_ref[...], preferred_element_type=F32)
            cos = cos_tab[rows, :] * scale
            sin = sin_tab[rows, :] * scale
            for s in range(ATTN_WIDTH // LANES):
                t = p[:, s * LANES:(s + 1) * LANES]
                o_ref[rows, s * LANES:(s + 1) * LANES] = (
                    t * cos + pltpu.roll(t, LANES // 2, 1) * sin).astype(BF16)

    def plain_slab(from_token_order):
        for rc in range(n_chunks):
            rows = slice(rc * PROJ_ROW_CHUNK, (rc + 1) * PROJ_ROW_CHUNK)
            o_ref[rows, :] = jnp.dot(lhs_rows(from_token_order, rows), w_ref[...],
                                     preferred_element_type=F32).astype(BF16)

    pl.when(jnp.logical_and(is_rope, in_group0))(functools.partial(rope_slab, True, cos_ref, sin_ref))
    pl.when(jnp.logical_and(is_rope, in_group1))(functools.partial(rope_slab, False, cos1_sc, sin1_sc))
    pl.when(jnp.logical_and(is_rope, in_group2))(functools.partial(rope_slab, False, cos2_sc, sin2_sc))
    is_plain = jnp.logical_not(is_rope)
    pl.when(jnp.logical_and(is_plain, in_group0))(functools.partial(plain_slab, True))
    pl.when(jnp.logical_and(is_plain, jnp.logical_not(in_group0)))(functools.partial(plain_slab, False))


def _proj_weight_kernel(w_ref, o_ref):
    s = pl.program_id(0)
    is_qk = functools.reduce(jnp.logical_or, [jnp.logical_or(s == f, s == f + 1) for f in GROUP_FIRST_SLAB])

    @pl.when(is_qk)
    def _():
        half = HEAD_DIM // 2
        quarter = lax.broadcasted_iota(jnp.int32, (1, LANES), 1) // half
        for c in range(ATTN_WIDTH // LANES):
            sl = slice(c * LANES, (c + 1) * LANES)
            w = w_ref[0, :, sl]
            w = jnp.where(quarter == 1, pltpu.roll(w, LANES - half, 1),
                          jnp.where(quarter == 2, pltpu.roll(w, half, 1), w))
            o_ref[:, sl] = w.astype(BF16)

    @pl.when(jnp.logical_not(is_qk))
    def _():
        o_ref[...] = w_ref[0].astype(BF16)


def _proj_weight(w_in_all, layer):
    def source_slab(s):
        return jnp.where(s == GATE_OUT_SLAB, Z_SLAB, jnp.where(s < GATE_OUT_SLAB, s, s - 1))

    return pl.pallas_call(
        _proj_weight_kernel,
        grid=(N_PROJ_SLABS,),
        in_specs=[pl.BlockSpec((1, D_MODEL, ATTN_WIDTH), lambda s: (layer, 0, source_slab(s)))],
        out_specs=pl.BlockSpec((D_MODEL, ATTN_WIDTH), lambda s: (0, s)),
        out_shape=jax.ShapeDtypeStruct((D_MODEL, N_PROJ_SLABS * ATTN_WIDTH), BF16),
        compiler_params=pltpu.CompilerParams(
            dimension_semantics=("arbitrary",), vmem_limit_bytes=VMEM_LIMIT),
        name="proj_weight_prep",
    )(w_in_all)


def _cast_kernel(w_ref, o_ref):
    o_ref[...] = w_ref[0].astype(BF16)


def _bf16_weight(w_all, layer):
    _, rows, cols = w_all.shape
    block_rows = rows
    while block_rows * cols * 4 > WEIGHT_CAST_BLOCK_BYTES:
        block_rows //= 2
    assert rows % block_rows == 0 and block_rows % 16 == 0
    return pl.pallas_call(
        _cast_kernel,
        grid=(rows // block_rows,),
        in_specs=[pl.BlockSpec((1, block_rows, cols), lambda i: (layer, i, 0))],
        out_specs=pl.BlockSpec((block_rows, cols), lambda i: (i, 0)),
        out_shape=jax.ShapeDtypeStruct((rows, cols), BF16),
        compiler_params=pltpu.CompilerParams(
            dimension_semantics=("arbitrary",), vmem_limit_bytes=VMEM_LIMIT),
        name="weight_cast",
    )(w_all)


def _proj_layer(x2, gpre, w_in, cos2, sin2, seq_len):
    n_tok = x2.shape[0]
    tiles_per_seq = seq_len // TILE
    return pl.pallas_call(
        _proj_kernel,
        grid=(n_tok // TILE, N_PROJ_SLABS),
        in_specs=[
            pl.BlockSpec((TILE, D_MODEL), lambda i, j: (i, 0)),
            pl.BlockSpec((1, D_MODEL), lambda i, j: (0, 0)),
            pl.BlockSpec((D_MODEL, ATTN_WIDTH), lambda i, j: (0, j)),
            pl.BlockSpec((TILE, LANES), lambda i, j: (i % tiles_per_seq, 0)),
            pl.BlockSpec((TILE, LANES), lambda i, j: (i % tiles_per_seq, 0)),
        ],
        out_specs=pl.BlockSpec((TILE, ATTN_WIDTH), lambda i, j: (i, j)),
        out_shape=jax.ShapeDtypeStruct((n_tok, N_PROJ_SLABS * ATTN_WIDTH), BF16),
        scratch_shapes=[
            pltpu.VMEM((D_MODEL // LANES, TILE, LANES), F32),
            pltpu.VMEM((D_MODEL // LANES, TILE, LANES), F32),
            pltpu.VMEM((TILE, D_MODEL), BF16),
        ] + 4 * [pltpu.VMEM((TILE, LANES), F32)],
        compiler_params=pltpu.CompilerParams(
            dimension_semantics=("arbitrary", "arbitrary"), vmem_limit_bytes=VMEM_LIMIT),
        name="attn_proj",
    )(x2, gpre.reshape(1, -1), w_in, cos2, sin2)


def _attn_masks():
    t = ATTN_BLOCK
    kj = np.arange(2 * t)[:, None]
    qc = np.arange(t)[None, :]
    cur = (kj >= t) & (kj - t <= qc)
    prev = (kj < t) & (kj >= qc)
    valid = np.stack([cur, cur | prev])
    return jnp.asarray(np.where(valid, 0.0, NEG_BIG), dtype=BF16)


def _attn_kernel(q_ref, kp_ref, kc_ref, vp_ref, vc_ref, mask_ref, eye_ref, o_ref, st_ref, *, chain):
    t = ATTN_BLOCK
    first_step_mask = mask_ref[jnp.minimum(pl.program_id(2), 1)]
    eye = eye_ref[...]
    ones = jnp.ones((2 * t, LANES), BF16)
    lane = lax.broadcasted_iota(jnp.int32, (1, LANES), 1)
    head_a_qk = ((lane // (HEAD_DIM // 2)) % 2) == 0
    head_a_v = (lane // HEAD_DIM) == 0
    for jb in range(q_ref.shape[0] // t):
        rows = slice(jb * t, (jb + 1) * t)
        if jb % chain > 0:
            prev_k, prev_v, prev_rows, mask_t = kc_ref, vc_ref, slice((jb - 1) * t, jb * t), mask_ref[1]
        else:
            prev_k, prev_v, mask_t = kp_ref, vp_ref, first_step_mask
            start = 0 if kp_ref.shape[0] == t else (jb + chain - 1) * t
            prev_rows = slice(start, start + t)
        st = jnp.zeros((t, LANES), F32)
        for hp in range(N_HEADS // 2):
            sl = slice(hp * LANES, (hp + 1) * LANES)
            q2 = q_ref[rows, sl]
            zero = jnp.zeros_like(q2)
            lhs = jnp.concatenate([
                jnp.concatenate([jnp.where(head_a_qk, q2, zero), eye], axis=1),
                jnp.concatenate([jnp.where(head_a_qk, zero, q2), eye], axis=1)], axis=0)
            k_ext = jnp.concatenate(
                [jnp.concatenate([prev_k[prev_rows, sl], kc_ref[rows, sl]], axis=0), mask_t], axis=1)
            v_ext = jnp.concatenate(
                [jnp.concatenate([prev_v[prev_rows, sl], vc_ref[rows, sl]], axis=0), ones], axis=1)
            s = lax.dot_general(lhs, k_ext, (((1,), (1,)), ((), ())), preferred_element_type=F32)
            m = jnp.max(s, axis=-1, keepdims=True)
            p = jnp.exp2(s - m).astype(BF16)
            pv = jnp.dot(p, v_ext, preferred_element_type=F32)
            o_ref[rows, sl] = jnp.where(head_a_v, pv[:t, :LANES], pv[t:, :LANES]).astype(BF16)
            denom = pv[:, LANES:]
            st = jnp.where(lane == 2 * hp, m[:t], st)
            st = jnp.where(lane == 2 * hp + 1, m[t:], st)
            st = jnp.where(lane == N_HEADS + 2 * hp, denom[:t], st)
            st = jnp.where(lane == N_HEADS + 2 * hp + 1, denom[t:], st)
        st_ref[rows, :] = st


def _attn_group(proj2, g, dilation, batch, seq_len):
    n_tok = batch * seq_len
    t, qb = ATTN_BLOCK, ATTN_QB[g]
    n_blocks = seq_len // dilation // t
    tiles_per_seq = seq_len // TILE
    blocks_per_tile = TILE // t
    bpc = blocks_per_tile // dilation
    first = GROUP_FIRST_SLAB[g]
    if bpc >= qb:
        assert bpc % qb == 0
        chain = qb
        grid = (batch, dilation, n_blocks // qb)

        def row_block(b, r, n):
            return (b * tiles_per_seq + n // bpc) * blocks_per_tile + r * bpc + n % bpc

        cur = lambda b, r, n: row_block(b, r, qb * n) // qb
        prev = lambda b, r, n: row_block(b, r, jnp.maximum(qb * n - 1, 0))
        prev_rows = t
    else:
        assert qb % bpc == 0 and blocks_per_tile % qb == 0
        chain = bpc
        grid = (batch, blocks_per_tile // qb, n_blocks // bpc)
        cur = lambda b, r, n: (b * tiles_per_seq + n) * (blocks_per_tile // qb) + r
        prev = lambda b, r, n: cur(b, r, jnp.maximum(n - 1, 0))
        prev_rows = qb * t

    def spec(c, is_prev):
        if is_prev:
            return pl.BlockSpec((prev_rows, ATTN_WIDTH), lambda b, r, n: (prev(b, r, n), first + c))
        return pl.BlockSpec((qb * t, ATTN_WIDTH), lambda b, r, n: (cur(b, r, n), first + c))

    return pl.pallas_call(
        functools.partial(_attn_kernel, chain=chain),
        grid=grid,
        in_specs=[spec(0, False), spec(1, True), spec(1, False), spec(2, True), spec(2, False),
                  pl.BlockSpec((2, 2 * t, t), lambda b, r, n: (0, 0, 0)),
                  pl.BlockSpec((t, t), lambda b, r, n: (0, 0))],
        out_specs=[
            pl.BlockSpec((qb * t, ATTN_WIDTH), lambda b, r, n: (cur(b, r, n), 0)),
            pl.BlockSpec((qb * t, LANES), lambda b, r, n: (cur(b, r, n), 0)),
        ],
        out_shape=[
            jax.ShapeDtypeStruct((n_tok, ATTN_WIDTH), BF16),
            jax.ShapeDtypeStruct((n_tok, LANES), F32),
        ],
        compiler_params=pltpu.CompilerParams(
            dimension_semantics=("arbitrary", "arbitrary", "arbitrary"), vmem_limit_bytes=VMEM_LIMIT),
        name=f"dilated_attn_g{g}",
    )(proj2, proj2, proj2, proj2, proj2, _attn_masks(), jnp.eye(t, dtype=BF16))


def _combine_kernel(x_ref, o0_ref, o1_ref, o2_ref, s0_ref, s1_ref, s2_ref, z_ref,
                    expand_ref, wout_ref, gpost_ref, out_ref, on1_sc, on2_sc, tmp_sc, st1_sc, st2_sc,
                    tmp_st_sc, y_sc):
    n_slabs = ATTN_WIDTH // LANES
    is_head_lane = lax.broadcasted_iota(jnp.int32, (1, LANES), 1) < N_HEADS
    for sub in range(COMBINE_TOKENS // COMBINE_SUB):
        base = sub * COMBINE_SUB
        rows = slice(base, base + COMBINE_SUB)
        d1, d2 = DILATED_GROUPS[1][1], DILATED_GROUPS[2][1]
        n = COMBINE_SUB // d1
        src = slice(sub * n, (sub + 1) * n)
        for b in range(d1):
            dst = pl.ds(base + b, n, stride=d1)
            st1_sc[dst, :] = s1_ref[0, b, src, :]
            for c in range(n_slabs):
                on1_sc[c, dst, :] = o1_ref[0, b, src, c * LANES:(c + 1) * LANES].astype(F32)
        ratio = d2 // d1
        n2 = COMBINE_SUB // d2
        src2 = slice(sub * n2, (sub + 1) * n2)
        for b in range(d1):
            mid = slice(base + b * n, base + (b + 1) * n)
            for a in range(ratio):
                dst = pl.ds(base + b * n + a, n2, stride=ratio)
                tmp_st_sc[dst, :] = s2_ref[0, a * d1 + b, src2, :]
                for c in range(n_slabs):
                    tmp_sc[c, dst, :] = o2_ref[0, a * d1 + b, src2, c * LANES:(c + 1) * LANES].astype(F32)
            dst = pl.ds(base + b, n, stride=d1)
            st2_sc[dst, :] = tmp_st_sc[mid, :]
            for c in range(n_slabs):
                on2_sc[c, dst, :] = tmp_sc[c, mid, :]

        sts = [s0_ref[rows, :], st1_sc[rows, :], st2_sc[rows, :]]
        m = jnp.maximum(jnp.maximum(sts[0], sts[1]), sts[2])
        es = [jnp.exp2(s - m) for s in sts]
        total = None
        for e, s in zip(es, sts):
            term = e * pltpu.roll(s, LANES - N_HEADS, 1)
            total = term if total is None else total + term
        inv = 1.0 / jnp.where(is_head_lane, total, 1.0)
        alphas = []
        for e in es:
            alpha = jnp.where(is_head_lane, e * inv, 0.0)
            hi = alpha.astype(BF16)
            lo = (alpha - hi.astype(F32)).astype(BF16)
            alphas.append(jnp.dot(jnp.concatenate([hi, lo], axis=1), expand_ref[...],
                                  preferred_element_type=F32))
        for c in range(n_slabs):
            sl = slice(c * LANES, (c + 1) * LANES)
            o = (alphas[0][:, sl] * o0_ref[rows, sl].astype(F32)
                 + alphas[1][:, sl] * on1_sc[c, rows, :] + alphas[2][:, sl] * on2_sc[c, rows, :])
            y_sc[rows, sl] = (o * _silu(z_ref[rows, sl].astype(F32))).astype(BF16)
        out = jnp.dot(y_sc[rows, :], wout_ref[...], preferred_element_type=F32)
        out_ref[rows, :] = x_ref[rows, :] + _rms_normalize(out, gpost_ref[...])


def _combine_layer(x2, outs, stats, proj2, w_out, gpost):
    n_tok = x2.shape[0]
    tm = COMBINE_TOKENS
    n_tiles = n_tok // TILE
    per_tile = TILE // tm
    head_of_lane = np.arange(ATTN_WIDTH) // HEAD_DIM
    expand = jnp.asarray(np.tile(np.arange(LANES)[:, None] == head_of_lane[None, :], (2, 1)), dtype=BF16)
    tok = lambda i: (i, 0)
    const = lambda i: (0, 0)
    in_tile = lambda i: (i // per_tile, 0, i % per_tile, 0)
    d1, d2 = DILATED_GROUPS[1][1], DILATED_GROUPS[2][1]
    return pl.pallas_call(
        _combine_kernel,
        grid=(n_tok // tm,),
        in_specs=[
            pl.BlockSpec((tm, D_MODEL), tok),
            pl.BlockSpec((tm, ATTN_WIDTH), tok),
            pl.BlockSpec((1, d1, tm // d1, ATTN_WIDTH), in_tile),
            pl.BlockSpec((1, d2, tm // d2, ATTN_WIDTH), in_tile),
            pl.BlockSpec((tm, LANES), tok),
            pl.BlockSpec((1, d1, tm // d1, LANES), in_tile),
            pl.BlockSpec((1, d2, tm // d2, LANES), in_tile),
            pl.BlockSpec((tm, ATTN_WIDTH), lambda i: (i, GATE_OUT_SLAB)),
            pl.BlockSpec((2 * LANES, ATTN_WIDTH), const),
            pl.BlockSpec((ATTN_WIDTH, D_MODEL), const),
            pl.BlockSpec((1, D_MODEL), const),
        ],
        out_specs=pl.BlockSpec((tm, D_MODEL), tok),
        out_shape=jax.ShapeDtypeStruct(x2.shape, F32),
        scratch_shapes=[
            pltpu.VMEM((ATTN_WIDTH // LANES, tm, LANES), F32),
            pltpu.VMEM((ATTN_WIDTH // LANES, tm, LANES), F32),
            pltpu.VMEM((ATTN_WIDTH // LANES, tm, LANES), F32),
            pltpu.VMEM((tm, LANES), F32),
            pltpu.VMEM((tm, LANES), F32),
            pltpu.VMEM((tm, LANES), F32),
            pltpu.VMEM((tm, ATTN_WIDTH), BF16),
        ],
        compiler_params=pltpu.CompilerParams(
            dimension_semantics=("arbitrary",), vmem_limit_bytes=VMEM_LIMIT),
        name="attn_combine",
    )(x2,
      outs[0],
      outs[1].reshape(n_tiles, d1, TILE // d1, ATTN_WIDTH),
      outs[2].reshape(n_tiles, d2, TILE // d2, ATTN_WIDTH),
      stats[0],
      stats[1].reshape(n_tiles, d1, TILE // d1, LANES),
      stats[2].reshape(n_tiles, d2, TILE // d2, LANES),
      proj2, expand, w_out, gpost.reshape(1, -1))


def _rope_tables(seq_len):
    half = HEAD_DIM // 2
    exponent = np.arange(0, HEAD_DIM, 2, dtype=np.float32) / np.float32(HEAD_DIM)
    power = np.power(np.float64(ROPE_THETA), exponent.astype(np.float64)).astype(np.float32)
    inv_freq = np.float32(1.0) / power
    ang = np.arange(seq_len, dtype=np.float32)[:, None] * inv_freq[None, :]
    cos = np.cos(ang.astype(np.float64)).astype(np.float32)
    sin = np.sin(ang.astype(np.float64)).astype(np.float32)
    cos2 = np.tile(cos, (1, LANES // half))
    sin2 = np.concatenate([-sin, -sin, sin, sin], axis=1)
    return jnp.asarray(cos2), jnp.asarray(sin2)


def kernel(x, norm_pre, norm_post, a_w_in, a_ln_g, a_ln_b, a_w_s, a_b_s, a_w_out, b_w_in, b_w_out):
    batch, seq_len, d = x.shape
    depth = norm_pre.shape[0]
    cos2, sin2 = _rope_tables(seq_len)
    x2 = x.reshape(batch * seq_len, d)
    for i in range(depth):
        j = i // 2
        if i % 2 == 0:
            x2 = _gmlp_layer(x2, norm_pre[i], norm_post[i], _bf16_weight(a_w_in, j), a_ln_g[j], a_ln_b[j],
                             a_w_s[j], a_b_s[j], _bf16_weight(a_w_out, j))
        else:
            proj2 = _proj_layer(x2, norm_pre[i], _proj_weight(b_w_in, j), cos2, sin2, seq_len)
            outs, stats = [], []
            for g, (_, dilation) in enumerate(DILATED_GROUPS):
                o, st = _attn_group(proj2, g, dilation, batch, seq_len)
                outs.append(o)
                stats.append(st)
            x2 = _combine_layer(x2, outs, stats, proj2, _bf16_weight(b_w_out, j), norm_post[i])
    return x2.reshape(batch, seq_len, d)
```

```python
import functools

import numpy as np
import jax
import jax.numpy as jnp
from jax import lax
from jax.experimental import pallas as pl
from jax.experimental.pallas import tpu as pltpu

F32 = jnp.float32
BF16 = jnp.bfloat16

D_MODEL = 1024
RMS_EPS = 1e-6
LN_EPS = 1e-5

GMLP_WIDTH = 2048
GMLP_CHUNK = 128
GMLP_GROUPS = 8
GMLP_GROUP_WIDTH = GMLP_WIDTH // GMLP_GROUPS
GMLP_DOT_GROUPS = 4

HEAD_DIM = 64
N_HEADS = 16
ATTN_WIDTH = N_HEADS * HEAD_DIM
DILATED_GROUPS = ((128, 1), (512, 4), (2048, 16))
N_DIL_GROUPS = len(DILATED_GROUPS)
ATTN_BLOCK = 128
KV_SLOTS = 3
ROPE_THETA = 10000.0
N_PROJ_SLABS = 3 * N_DIL_GROUPS + 1
Z_SLAB = 3 * N_DIL_GROUPS
GROUP_FIRST_SLAB = (0, 4, 7)
GATE_OUT_SLAB = 3
TILE = 2048

LANES = 128
NEG_BIG = -1e30

GMLP_TOKENS = 512
PROJ_ROW_CHUNK = 512
COMBINE_TOKENS = 1024
COMBINE_SUB = 512
VMEM_LIMIT = 60 * 1024 * 1024
WEIGHT_CAST_BLOCK_BYTES = 4 * 1024 * 1024


def _gelu(x):
    return 0.5 * x * (1.0 + lax.erf(x * np.float32(np.sqrt(0.5))))


def _silu(x):
    return x * jax.nn.sigmoid(x)


def _rms_normalize(x, gain):
    ms = jnp.mean(x * x, axis=-1, keepdims=True)
    return x * lax.rsqrt(ms + RMS_EPS) * gain


def _gmlp_kernel(x_ref, gpre_ref, gpost_ref, win_ref, lng_ref, lnb_ref, ws_ref,
                 bs_ref, wout_ref, o_ref, h_sc, v_sc, t_sc, y_sc):
    tm = x_ref.shape[0]
    gw = GMLP_GROUP_WIDTH
    x = x_ref[...]
    h_sc[...] = _rms_normalize(x, gpre_ref[...]).astype(BF16)

    pw = GMLP_DOT_GROUPS * gw
    s1 = jnp.zeros((tm, 1), F32)
    for gp in range(GMLP_WIDTH // pw):
        c0 = GMLP_WIDTH + gp * pw
        v = _gelu(jnp.dot(h_sc[...], win_ref[:, c0:c0 + pw], preferred_element_type=F32))
        v_sc[:, gp * pw:(gp + 1) * pw] = v
        s1 = s1 + jnp.sum(v, axis=-1, keepdims=True)
    for gp in range(GMLP_WIDTH // pw):
        psl = slice(gp * pw, (gp + 1) * pw)
        u = _gelu(jnp.dot(h_sc[...], win_ref[:, psl], preferred_element_type=F32))
        z0 = 2 * GMLP_WIDTH + gp * pw
        z = jnp.dot(h_sc[...], win_ref[:, z0:z0 + pw], preferred_element_type=F32)
        t_sc[:, psl] = u * _silu(z)

    mu = s1 * (1.0 / GMLP_WIDTH)
    s2 = jnp.zeros((tm, 1), F32)
    for g in range(GMLP_GROUPS):
        dv = v_sc[:, g * gw:(g + 1) * gw] - mu
        s2 = s2 + jnp.sum(dv * dv, axis=-1, keepdims=True)
    rstd = lax.rsqrt(s2 * (1.0 / GMLP_WIDTH) + LN_EPS)

    row = lax.broadcasted_iota(jnp.int32, (GMLP_CHUNK, GMLP_CHUNK), 0)
    col = lax.broadcasted_iota(jnp.int32, (GMLP_CHUNK, GMLP_CHUNK), 1)
    causal = col <= row
    for g in range(GMLP_GROUPS):
        sl = slice(g * gw, (g + 1) * gw)
        vn = ((v_sc[:, sl] - mu) * rstd * lng_ref[:, sl] + lnb_ref[:, sl]).astype(BF16)
        wm = jnp.where(causal, ws_ref[g], 0.0).astype(BF16)
        bias = bs_ref[:, g:g + 1]
        sv = jnp.concatenate(
            [jnp.dot(wm, vn[c * GMLP_CHUNK:(c + 1) * GMLP_CHUNK], preferred_element_type=F32) + bias
             for c in range(tm // GMLP_CHUNK)], axis=0)
        y_sc[:, sl] = (t_sc[:, sl] * sv).astype(BF16)

    out = jnp.dot(y_sc[...], wout_ref[...], preferred_element_type=F32)
    o_ref[...] = x_ref[...] + _rms_normalize(out, gpost_ref[...])


def _gmlp_layer(x2, gpre, gpost, w_in, ln_g, ln_b, w_s, b_s, w_out):
    n_tok = x2.shape[0]
    tm = GMLP_TOKENS
    const = lambda i: (0, 0)
    return pl.pallas_call(
        _gmlp_kernel,
        grid=(n_tok // tm,),
        in_specs=[
            pl.BlockSpec((tm, D_MODEL), lambda i: (i, 0)),
            pl.BlockSpec((1, D_MODEL), const),
            pl.BlockSpec((1, D_MODEL), const),
            pl.BlockSpec((D_MODEL, 3 * GMLP_WIDTH), const, pipeline_mode=pl.Buffered(1)),
            pl.BlockSpec((1, GMLP_WIDTH), const),
            pl.BlockSpec((1, GMLP_WIDTH), const),
            pl.BlockSpec((GMLP_GROUPS, GMLP_CHUNK, GMLP_CHUNK), lambda i: (0, 0, 0)),
            pl.BlockSpec((GMLP_CHUNK, GMLP_GROUPS), const),
            pl.BlockSpec((GMLP_WIDTH, D_MODEL), const, pipeline_mode=pl.Buffered(1)),
        ],
        out_specs=pl.BlockSpec((tm, D_MODEL), lambda i: (i, 0)),
        out_shape=jax.ShapeDtypeStruct(x2.shape, F32),
        scratch_shapes=[
            pltpu.VMEM((tm, D_MODEL), BF16),
            pltpu.VMEM((tm, GMLP_WIDTH), F32),
            pltpu.VMEM((tm, GMLP_WIDTH), F32),
            pltpu.VMEM((tm, GMLP_WIDTH), BF16),
        ],
        compiler_params=pltpu.CompilerParams(
            dimension_semantics=("arbitrary",), vmem_limit_bytes=VMEM_LIMIT),
        name="gmlp_layer",
    )(x2, gpre.reshape(1, -1), gpost.reshape(1, -1), w_in,
      ln_g.reshape(1, -1), ln_b.reshape(1, -1), w_s, jnp.transpose(b_s), w_out)


def _proj_kernel(x_ref, gpre_ref, w_ref, cos_ref, sin_ref, o_ref,
                 hs_sc, hs1_sc, h_sc, cos1_sc, sin1_sc, cos2_sc, sin2_sc):
    j = pl.program_id(1)
    n_chunks = TILE // PROJ_ROW_CHUNK
    n_slabs = D_MODEL // LANES
    first1, first2 = GROUP_FIRST_SLAB[1], GROUP_FIRST_SLAB[2]
    d1, d2 = DILATED_GROUPS[1][1], DILATED_GROUPS[2][1]
    ratio = d2 // d1

    @pl.when(j == 0)
    def _():
        for rc in range(n_chunks):
            rows = slice(rc * PROJ_ROW_CHUNK, (rc + 1) * PROJ_ROW_CHUNK)
            hs = _rms_normalize(x_ref[rows, :], gpre_ref[...])
            for c in range(n_slabs):
                hs_sc[c, rows, :] = hs[:, c * LANES:(c + 1) * LANES]

    @pl.when(j == first1)
    def _():
        rows = TILE // d1
        for b in range(d1):
            out, src = slice(b * rows, (b + 1) * rows), pl.ds(b, rows, stride=d1)
            for c in range(n_slabs):
                v = hs_sc[c, src, :]
                hs1_sc[c, out, :] = v
                h_sc[out, c * LANES:(c + 1) * LANES] = v.astype(BF16)
            cos1_sc[out, :] = cos_ref[src, :]
            sin1_sc[out, :] = sin_ref[src, :]

    @pl.when(j == first2)
    def _():
        rows = TILE // d2
        for b in range(d1):
            for a in range(ratio):
                r = a * d1 + b
                out, src = slice(r * rows, (r + 1) * rows), pl.ds(b * (TILE // d1) + a, rows, stride=ratio)
                for c in range(n_slabs):
                    h_sc[out, c * LANES:(c + 1) * LANES] = hs1_sc[c, src, :].astype(BF16)
                cos2_sc[out, :] = cos1_sc[src, :]
                sin2_sc[out, :] = sin1_sc[src, :]

    is_q = functools.reduce(jnp.logical_or, [j == s for s in GROUP_FIRST_SLAB])
    is_k = functools.reduce(jnp.logical_or, [j == s + 1 for s in GROUP_FIRST_SLAB])
    is_rope = jnp.logical_or(is_q, is_k)
    in_group0 = j < first1
    in_group1 = jnp.logical_and(j >= first1, j < first2)
    in_group2 = j >= first2

    def lhs_rows(from_token_order, rows):
        if from_token_order:
            return jnp.concatenate([hs_sc[c, rows, :] for c in range(n_slabs)], axis=1).astype(BF16)
        return h_sc[rows, :]

    def rope_slab(from_token_order, cos_tab, sin_tab):
        scale = jnp.where(is_q, np.float32(np.log2(np.e) / np.sqrt(HEAD_DIM)), np.float32(1.0))
        for rc in range(n_chunks):
            rows = slice(rc * PROJ_ROW_CHUNK, (rc + 1) * PROJ_ROW_CHUNK)
            p = jnp.dot(lhs_rows(from_token_order, rows), w_ref[...], preferred_element_type=F32)
            cos = cos_tab[rows, :] * scale
            sin = sin_tab[rows, :] * scale
            for s in range(ATTN_WIDTH // LANES):
                t = p[:, s * LANES:(s + 1) * LANES]
                o_ref[rows, s * LANES:(s + 1) * LANES] = (
                    t * cos + pltpu.roll(t, LANES // 2, 1) * sin).astype(BF16)

    def plain_slab(from_token_order):
        for rc in range(n_chunks):
            rows = slice(rc * PROJ_ROW_CHUNK, (rc + 1) * PROJ_ROW_CHUNK)
            o_ref[rows, :] = jnp.dot(lhs_rows(from_token_order, rows), w_ref[...],
                                     preferred_element_type=F32).astype(BF16)

    pl.when(jnp.logical_and(is_rope, in_group0))(functools.partial(rope_slab, True, cos_ref, sin_ref))
    pl.when(jnp.logical_and(is_rope, in_group1))(functools.partial(rope_slab, False, cos1_sc, sin1_sc))
    pl.when(jnp.logical_and(is_rope, in_group2))(functools.partial(rope_slab, False, cos2_sc, sin2_sc))
    is_plain = jnp.logical_not(is_rope)
    pl.when(jnp.logical_and(is_plain, in_group0))(functools.partial(plain_slab, True))
    pl.when(jnp.logical_and(is_plain, jnp.logical_not(in_group0)))(functools.partial(plain_slab, False))


def _proj_weight_kernel(w_ref, o_ref):
    s = pl.program_id(0)
    is_qk = functools.reduce(jnp.logical_or, [jnp.logical_or(s == f, s == f + 1) for f in GROUP_FIRST_SLAB])

    @pl.when(is_qk)
    def _():
        half = HEAD_DIM // 2
        quarter = lax.broadcasted_iota(jnp.int32, (1, LANES), 1) // half
        for c in range(ATTN_WIDTH // LANES):
            sl = slice(c * LANES, (c + 1) * LANES)
            w = w_ref[0, :, sl]
            w = jnp.where(quarter == 1, pltpu.roll(w, LANES - half, 1),
                          jnp.where(quarter == 2, pltpu.roll(w, half, 1), w))
            o_ref[:, sl] = w.astype(BF16)

    @pl.when(jnp.logical_not(is_qk))
    def _():
        o_ref[...] = w_ref[0].astype(BF16)


def _proj_weight(w_in_all, layer):
    def source_slab(s):
        return jnp.where(s == GATE_OUT_SLAB, Z_SLAB, jnp.where(s < GATE_OUT_SLAB, s, s - 1))

    return pl.pallas_call(
        _proj_weight_kernel,
        grid=(N_PROJ_SLABS,),
        in_specs=[pl.BlockSpec((1, D_MODEL, ATTN_WIDTH), lambda s: (layer, 0, source_slab(s)))],
        out_specs=pl.BlockSpec((D_MODEL, ATTN_WIDTH), lambda s: (0, s)),
        out_shape=jax.ShapeDtypeStruct((D_MODEL, N_PROJ_SLABS * ATTN_WIDTH), BF16),
        compiler_params=pltpu.CompilerParams(
            dimension_semantics=("arbitrary",), vmem_limit_bytes=VMEM_LIMIT),
        name="proj_weight_prep",
    )(w_in_all)


def _cast_kernel(w_ref, o_ref):
    o_ref[...] = w_ref[0].astype(BF16)


def _bf16_weight(w_all, layer):
    _, rows, cols = w_all.shape
    block_rows = rows
    while block_rows * cols * 4 > WEIGHT_CAST_BLOCK_BYTES:
        block_rows //= 2
    assert rows % block_rows == 0 and block_rows % 16 == 0
    return pl.pallas_call(
        _cast_kernel,
        grid=(rows // block_rows,),
        in_specs=[pl.BlockSpec((1, block_rows, cols), lambda i: (layer, i, 0))],
        out_specs=pl.BlockSpec((block_rows, cols), lambda i: (i, 0)),
        out_shape=jax.ShapeDtypeStruct((rows, cols), BF16),
        compiler_params=pltpu.CompilerParams(
            dimension_semantics=("arbitrary",), vmem_limit_bytes=VMEM_LIMIT),
        name="weight_cast",
    )(w_all)


def _proj_layer(x2, gpre, w_in, cos2, sin2, seq_len):
    n_tok = x2.shape[0]
    tiles_per_seq = seq_len // TILE
    return pl.pallas_call(
        _proj_kernel,
        grid=(n_tok // TILE, N_PROJ_SLABS),
        in_specs=[
            pl.BlockSpec((TILE, D_MODEL), lambda i, j: (i, 0)),
            pl.BlockSpec((1, D_MODEL), lambda i, j: (0, 0)),
            pl.BlockSpec((D_MODEL, ATTN_WIDTH), lambda i, j: (0, j)),
            pl.BlockSpec((TILE, LANES), lambda i, j: (i % tiles_per_seq, 0)),
            pl.BlockSpec((TILE, LANES), lambda i, j: (i % tiles_per_seq, 0)),
        ],
        out_specs=pl.BlockSpec((TILE, ATTN_WIDTH), lambda i, j: (i, j)),
        out_shape=jax.ShapeDtypeStruct((n_tok, N_PROJ_SLABS * ATTN_WIDTH), BF16),
        scratch_shapes=[
            pltpu.VMEM((D_MODEL // LANES, TILE, LANES), F32),
            pltpu.VMEM((D_MODEL // LANES, TILE, LANES), F32),
            pltpu.VMEM((TILE, D_MODEL), BF16),
        ] + 4 * [pltpu.VMEM((TILE, LANES), F32)],
        compiler_params=pltpu.CompilerParams(
            dimension_semantics=("arbitrary", "arbitrary"), vmem_limit_bytes=VMEM_LIMIT),
        name="attn_proj",
    )(x2, gpre.reshape(1, -1), w_in, cos2, sin2)


def _attn_masks():
    t = ATTN_BLOCK
    kj = np.arange(2 * t)[:, None]
    qc = np.arange(t)[None, :]
    cur = (kj >= t) & (kj - t <= qc)
    prev = (kj < t) & (kj >= qc)
    valid = np.stack([cur, cur | prev])
    return jnp.asarray(np.where(valid, 0.0, NEG_BIG), dtype=BF16)


def _attn_ring_kernel(q_ref, kv_hbm, mask_ref, eye_ref, o_ref, st_ref, k_ring, v_ring, sem, *,
                      chain, first_slab, n_steps):
    n = pl.program_id(1)
    step = pl.program_id(0) * pl.num_programs(1) + n

    def copies(tile_idx, slot):
        rows = pl.ds(pl.multiple_of(tile_idx * TILE, TILE), TILE)
        return [pltpu.make_async_copy(
            kv_hbm.at[rows, pl.ds((first_slab + 1 + c) * ATTN_WIDTH, ATTN_WIDTH)], ring.at[slot], sem.at[c, slot])
            for c, ring in enumerate((k_ring, v_ring))]

    @pl.when(step == 0)
    def _():
        for cp in copies(0, 0):
            cp.start()

    @pl.when(step + 1 < n_steps)
    def _():
        for cp in copies(step + 1, lax.rem(step + 1, KV_SLOTS)):
            cp.start()

    cur = lax.rem(step, KV_SLOTS)
    for cp in copies(step, cur):
        cp.wait()
    prev = jnp.where(n == 0, cur, lax.rem(step + KV_SLOTS - 1, KV_SLOTS))
    _attn_kernel(q_ref, k_ring.at[prev], k_ring.at[cur], v_ring.at[prev], v_ring.at[cur], mask_ref, eye_ref,
                 o_ref, st_ref, chain=chain, first_tile=n == 0)


def _attn_kernel(q_ref, kp_ref, kc_ref, vp_ref, vc_ref, mask_ref, eye_ref, o_ref, st_ref, *, chain, first_tile):
    t = ATTN_BLOCK
    first_step_mask = mask_ref[jnp.where(first_tile, 0, 1)]
    eye = eye_ref[...]
    ones = jnp.ones((2 * t, LANES), BF16)
    lane = lax.broadcasted_iota(jnp.int32, (1, LANES), 1)
    head_a_qk = ((lane // (HEAD_DIM // 2)) % 2) == 0
    head_a_v = (lane // HEAD_DIM) == 0
    for jb in range(q_ref.shape[0] // t):
        rows = slice(jb * t, (jb + 1) * t)
        if jb % chain > 0:
            prev_k, prev_v, prev_rows, mask_t = kc_ref, vc_ref, slice((jb - 1) * t, jb * t), mask_ref[1]
        else:
            prev_k, prev_v, mask_t = kp_ref, vp_ref, first_step_mask
            start = 0 if kp_ref.shape[0] == t else (jb + chain - 1) * t
            prev_rows = slice(start, start + t)
        st = jnp.zeros((t, LANES), F32)
        for hp in range(N_HEADS // 2):
            sl = slice(hp * LANES, (hp + 1) * LANES)
            q2 = q_ref[rows, sl]
            zero = jnp.zeros_like(q2)
            lhs = jnp.concatenate([
                jnp.concatenate([jnp.where(head_a_qk, q2, zero), eye], axis=1),
                jnp.concatenate([jnp.where(head_a_qk, zero, q2), eye], axis=1)], axis=0)
            k_ext = jnp.concatenate(
                [jnp.concatenate([prev_k[prev_rows, sl], kc_ref[rows, sl]], axis=0), mask_t], axis=1)
            v_ext = jnp.concatenate(
                [jnp.concatenate([prev_v[prev_rows, sl], vc_ref[rows, sl]], axis=0), ones], axis=1)
            s = lax.dot_general(lhs, k_ext, (((1,), (1,)), ((), ())), preferred_element_type=F32)
            m = jnp.max(s, axis=-1, keepdims=True)
            p = jnp.exp2(s - m).astype(BF16)
            pv = jnp.dot(p, v_ext, preferred_element_type=F32)
            o_ref[rows, sl] = jnp.where(head_a_v, pv[:t, :LANES], pv[t:, :LANES]).astype(BF16)
            denom = pv[:, LANES:]
            st = jnp.where(lane == 2 * hp, m[:t], st)
            st = jnp.where(lane == 2 * hp + 1, m[t:], st)
            st = jnp.where(lane == N_HEADS + 2 * hp, denom[:t], st)
            st = jnp.where(lane == N_HEADS + 2 * hp + 1, denom[t:], st)
        st_ref[rows, :] = st


def _attn_group(proj2, g, dilation, batch, seq_len):
    n_tok = batch * seq_len
    t = ATTN_BLOCK
    tiles_per_seq = seq_len // TILE
    chain = TILE // t // dilation
    first = GROUP_FIRST_SLAB[g]
    tile = lambda b, n: b * tiles_per_seq + n
    return pl.pallas_call(
        functools.partial(_attn_ring_kernel, chain=chain, first_slab=first, n_steps=batch * tiles_per_seq),
        grid=(batch, tiles_per_seq),
        in_specs=[pl.BlockSpec((TILE, ATTN_WIDTH), lambda b, n: (tile(b, n), first)),
                  pl.BlockSpec(memory_space=pl.ANY),
                  pl.BlockSpec((2, 2 * t, t), lambda b, n: (0, 0, 0)),
                  pl.BlockSpec((t, t), lambda b, n: (0, 0))],
        out_specs=[
            pl.BlockSpec((TILE, ATTN_WIDTH), lambda b, n: (tile(b, n), 0)),
            pl.BlockSpec((TILE, LANES), lambda b, n: (tile(b, n), 0)),
        ],
        out_shape=[
            jax.ShapeDtypeStruct((n_tok, ATTN_WIDTH), BF16),
            jax.ShapeDtypeStruct((n_tok, LANES), F32),
        ],
        scratch_shapes=[
            pltpu.VMEM((KV_SLOTS, TILE, ATTN_WIDTH), BF16),
            pltpu.VMEM((KV_SLOTS, TILE, ATTN_WIDTH), BF16),
            pltpu.SemaphoreType.DMA((2, KV_SLOTS)),
        ],
        compiler_params=pltpu.CompilerParams(
            dimension_semantics=("arbitrary", "arbitrary"), vmem_limit_bytes=VMEM_LIMIT),
        name=f"dilated_attn_g{g}",
    )(proj2, proj2, _attn_masks(), jnp.eye(t, dtype=BF16))


def _combine_kernel(x_ref, o0_ref, o1_ref, o2_ref, s0_ref, s1_ref, s2_ref, z_ref,
                    expand_ref, wout_ref, gpost_ref, out_ref, on1_sc, on2_sc, tmp_sc, st1_sc, st2_sc,
                    tmp_st_sc, y_sc):
    n_slabs = ATTN_WIDTH // LANES
    is_head_lane = lax.broadcasted_iota(jnp.int32, (1, LANES), 1) < N_HEADS
    for sub in range(COMBINE_TOKENS // COMBINE_SUB):
        base = sub * COMBINE_SUB
        rows = slice(base, base + COMBINE_SUB)
        d1, d2 = DILATED_GROUPS[1][1], DILATED_GROUPS[2][1]
        n = COMBINE_SUB // d1
        src = slice(sub * n, (sub + 1) * n)
        for b in range(d1):
            dst = pl.ds(base + b, n, stride=d1)
            st1_sc[dst, :] = s1_ref[0, b, src, :]
            for c in range(n_slabs):
                on1_sc[c, dst, :] = o1_ref[0, b, src, c * LANES:(c + 1) * LANES].astype(F32)
        ratio = d2 // d1
        n2 = COMBINE_SUB // d2
        src2 = slice(sub * n2, (sub + 1) * n2)
        for b in range(d1):
            mid = slice(base + b * n, base + (b + 1) * n)
            for a in range(ratio):
                dst = pl.ds(base + b * n + a, n2, stride=ratio)
                tmp_st_sc[dst, :] = s2_ref[0, a * d1 + b, src2, :]
                for c in range(n_slabs):
                    tmp_sc[c, dst, :] = o2_ref[0, a * d1 + b, src2, c * LANES:(c + 1) * LANES].astype(F32)
            dst = pl.ds(base + b, n, stride=d1)
            st2_sc[dst, :] = tmp_st_sc[mid, :]
            for c in range(n_slabs):
                on2_sc[c, dst, :] = tmp_sc[c, mid, :]

        sts = [s0_ref[rows, :], st1_sc[rows, :], st2_sc[rows, :]]
        m = jnp.maximum(jnp.maximum(sts[0], sts[1]), sts[2])
        es = [jnp.exp2(s - m) for s in sts]
        total = None
        for e, s in zip(es, sts):
            term = e * pltpu.roll(s, LANES - N_HEADS, 1)
            total = term if total is None else total + term
        inv = 1.0 / jnp.where(is_head_lane, total, 1.0)
        alphas = []
        for e in es:
            alpha = jnp.where(is_head_lane, e * inv, 0.0)
            hi = alpha.astype(BF16)
            lo = (alpha - hi.astype(F32)).astype(BF16)
            alphas.append(jnp.dot(jnp.concatenate([hi, lo], axis=1), expand_ref[...],
                                  preferred_element_type=F32))
        for c in range(n_slabs):
            sl = slice(c * LANES, (c + 1) * LANES)
            o = (alphas[0][:, sl] * o0_ref[rows, sl].astype(F32)
                 + alphas[1][:, sl] * on1_sc[c, rows, :] + alphas[2][:, sl] * on2_sc[c, rows, :])
            y_sc[rows, sl] = (o * _silu(z_ref[rows, sl].astype(F32))).astype(BF16)
        out = jnp.dot(y_sc[rows, :], wout_ref[...], preferred_element_type=F32)
        out_ref[rows, :] = x_ref[rows, :] + _rms_normalize(out, gpost_ref[...])


def _combine_layer(x2, outs, stats, proj2, w_out, gpost):
    n_tok = x2.shape[0]
    tm = COMBINE_TOKENS
    n_tiles = n_tok // TILE
    per_tile = TILE // tm
    head_of_lane = np.arange(ATTN_WIDTH) // HEAD_DIM
    expand = jnp.asarray(np.tile(np.arange(LANES)[:, None] == head_of_lane[None, :], (2, 1)), dtype=BF16)
    tok = lambda i: (i, 0)
    const = lambda i: (0, 0)
    in_tile = lambda i: (i // per_tile, 0, i % per_tile, 0)
    d1, d2 = DILATED_GROUPS[1][1], DILATED_GROUPS[2][1]
    return pl.pallas_call(
        _combine_kernel,
        grid=(n_tok // tm,),
        in_specs=[
            pl.BlockSpec((tm, D_MODEL), tok),
            pl.BlockSpec((tm, ATTN_WIDTH), tok),
            pl.BlockSpec((1, d1, tm // d1, ATTN_WIDTH), in_tile),
            pl.BlockSpec((1, d2, tm // d2, ATTN_WIDTH), in_tile),
            pl.BlockSpec((tm, LANES), tok),
            pl.BlockSpec((1, d1, tm // d1, LANES), in_tile),
            pl.BlockSpec((1, d2, tm // d2, LANES), in_tile),
            pl.BlockSpec((tm, ATTN_WIDTH), lambda i: (i, GATE_OUT_SLAB)),
            pl.BlockSpec((2 * LANES, ATTN_WIDTH), const),
            pl.BlockSpec((ATTN_WIDTH, D_MODEL), const),
            pl.BlockSpec((1, D_MODEL), const),
        ],
        out_specs=pl.BlockSpec((tm, D_MODEL), tok),
        out_shape=jax.ShapeDtypeStruct(x2.shape, F32),
        scratch_shapes=[
            pltpu.VMEM((ATTN_WIDTH // LANES, tm, LANES), F32),
            pltpu.VMEM((ATTN_WIDTH // LANES, tm, LANES), F32),
            pltpu.VMEM((ATTN_WIDTH // LANES, tm, LANES), F32),
            pltpu.VMEM((tm, LANES), F32),
            pltpu.VMEM((tm, LANES), F32),
            pltpu.VMEM((tm, LANES), F32),
            pltpu.VMEM((tm, ATTN_WIDTH), BF16),
        ],
        compiler_params=pltpu.CompilerParams(
            dimension_semantics=("arbitrary",), vmem_limit_bytes=VMEM_LIMIT),
        name="attn_combine",
    )(x2,
      outs[0],
      outs[1].reshape(n_tiles, d1, TILE // d1, ATTN_WIDTH),
      outs[2].reshape(n_tiles, d2, TILE // d2, ATTN_WIDTH),
      stats[0],
      stats[1].reshape(n_tiles, d1, TILE // d1, LANES),
      stats[2].reshape(n_tiles, d2, TILE // d2, LANES),
      proj2, expand, w_out, gpost.reshape(1, -1))


def _rope_tables(seq_len):
    half = HEAD_DIM // 2
    exponent = np.arange(0, HEAD_DIM, 2, dtype=np.float32) / np.float32(HEAD_DIM)
    power = np.power(np.float64(ROPE_THETA), exponent.astype(np.float64)).astype(np.float32)
    inv_freq = np.float32(1.0) / power
    ang = np.arange(seq_len, dtype=np.float32)[:, None] * inv_freq[None, :]
    cos = np.cos(ang.astype(np.float64)).astype(np.float32)
    sin = np.sin(ang.astype(np.float64)).astype(np.float32)
    cos2 = np.tile(cos, (1, LANES // half))
    sin2 = np.concatenate([-sin, -sin, sin, sin], axis=1)
    return jnp.asarray(cos2), jnp.asarray(sin2)


def kernel(x, norm_pre, norm_post, a_w_in, a_ln_g, a_ln_b, a_w_s, a_b_s, a_w_out, b_w_in, b_w_out):
    batch, seq_len, d = x.shape
    depth = norm_pre.shape[0]
    cos2, sin2 = _rope_tables(seq_len)
    x2 = x.reshape(batch * seq_len, d)
    for i in range(depth):
        j = i // 2
        if i % 2 == 0:
            x2 = _gmlp_layer(x2, norm_pre[i], norm_post[i], _bf16_weight(a_w_in, j), a_ln_g[j], a_ln_b[j],
                             a_w_s[j], a_b_s[j], _bf16_weight(a_w_out, j))
        else:
            proj2 = _proj_layer(x2, norm_pre[i], _proj_weight(b_w_in, j), cos2, sin2, seq_len)
            outs, stats = [], []
            for g, (_, dilation) in enumerate(DILATED_GROUPS):
                o, st = _attn_group(proj2, g, dilation, batch, seq_len)
                outs.append(o)
                stats.append(st)
            x2 = _combine_layer(x2, outs, stats, proj2, _bf16_weight(b_w_out, j), norm_post[i])
    return x2.reshape(batch, seq_len, d)
```

```python
import functools

import numpy as np
import jax
import jax.numpy as jnp
from jax import lax
from jax.experimental import pallas as pl
from jax.experimental.pallas import tpu as pltpu

F32 = jnp.float32
BF16 = jnp.bfloat16

D_MODEL = 1024
RMS_EPS = 1e-6
LN_EPS = 1e-5

GMLP_WIDTH = 2048
GMLP_CHUNK = 128
GMLP_GROUPS = 8
GMLP_GROUP_WIDTH = GMLP_WIDTH // GMLP_GROUPS
GMLP_DOT_GROUPS = 4

HEAD_DIM = 64
N_HEADS = 16
ATTN_WIDTH = N_HEADS * HEAD_DIM
DILATED_GROUPS = ((128, 1), (512, 4), (2048, 16))
N_DIL_GROUPS = len(DILATED_GROUPS)
ATTN_BLOCK = 128
KV_SLOTS = 3
ROPE_THETA = 10000.0
N_PROJ_SLABS = 3 * N_DIL_GROUPS + 1
Z_SLAB = 3 * N_DIL_GROUPS
GROUP_FIRST_SLAB = (0, 4, 7)
GATE_OUT_SLAB = 3
TILE = 2048

LANES = 128
NEG_BIG = -1e30

GMLP_TOKENS = 512
PROJ_ROW_CHUNK = 512
COMBINE_TOKENS = 1024
COMBINE_SUB = 512
VMEM_LIMIT = 60 * 1024 * 1024
WEIGHT_CAST_BLOCK_BYTES = 4 * 1024 * 1024


def _gelu(x):
    return 0.5 * x * (1.0 + lax.erf(x * np.float32(np.sqrt(0.5))))


def _silu(x):
    return x * jax.nn.sigmoid(x)


def _rms_normalize(x, gain):
    ms = jnp.mean(x * x, axis=-1, keepdims=True)
    return x * lax.rsqrt(ms + RMS_EPS) * gain


def _gmlp_kernel(x_ref, gpre_ref, gpost_ref, win_ref, lng_ref, lnb_ref, ws_ref,
                 bs_ref, wout_ref, o_ref, h_sc, v_sc, t_sc, y_sc):
    tm = x_ref.shape[0]
    gw = GMLP_GROUP_WIDTH
    x = x_ref[...]
    h_sc[...] = _rms_normalize(x, gpre_ref[...]).astype(BF16)

    pw = GMLP_DOT_GROUPS * gw
    s1 = jnp.zeros((tm, 1), F32)
    for gp in range(GMLP_WIDTH // pw):
        c0 = GMLP_WIDTH + gp * pw
        v = _gelu(jnp.dot(h_sc[...], win_ref[:, c0:c0 + pw], preferred_element_type=F32))
        v_sc[:, gp * pw:(gp + 1) * pw] = v
        s1 = s1 + jnp.sum(v, axis=-1, keepdims=True)
    for gp in range(GMLP_WIDTH // pw):
        psl = slice(gp * pw, (gp + 1) * pw)
        u = _gelu(jnp.dot(h_sc[...], win_ref[:, psl], preferred_element_type=F32))
        z0 = 2 * GMLP_WIDTH + gp * pw
        z = jnp.dot(h_sc[...], win_ref[:, z0:z0 + pw], preferred_element_type=F32)
        t_sc[:, psl] = u * _silu(z)

    mu = s1 * (1.0 / GMLP_WIDTH)
    s2 = jnp.zeros((tm, 1), F32)
    for g in range(GMLP_GROUPS):
        dv = v_sc[:, g * gw:(g + 1) * gw] - mu
        s2 = s2 + jnp.sum(dv * dv, axis=-1, keepdims=True)
    rstd = lax.rsqrt(s2 * (1.0 / GMLP_WIDTH) + LN_EPS)

    row = lax.broadcasted_iota(jnp.int32, (GMLP_CHUNK, GMLP_CHUNK), 0)
    col = lax.broadcasted_iota(jnp.int32, (GMLP_CHUNK, GMLP_CHUNK), 1)
    causal = col <= row
    for g in range(GMLP_GROUPS):
        sl = slice(g * gw, (g + 1) * gw)
        vn = ((v_sc[:, sl] - mu) * rstd * lng_ref[:, sl] + lnb_ref[:, sl]).astype(BF16)
        wm = jnp.where(causal, ws_ref[g], 0.0).astype(BF16)
        bias = bs_ref[:, g:g + 1]
        sv = jnp.concatenate(
            [jnp.dot(wm, vn[c * GMLP_CHUNK:(c + 1) * GMLP_CHUNK], preferred_element_type=F32) + bias
             for c in range(tm // GMLP_CHUNK)], axis=0)
        y_sc[:, sl] = (t_sc[:, sl] * sv).astype(BF16)

    out = jnp.dot(y_sc[...], wout_ref[...], preferred_element_type=F32)
    o_ref[...] = x_ref[...] + _rms_normalize(out, gpost_ref[...])


def _gmlp_layer(x2, gpre, gpost, w_in, ln_g, ln_b, w_s, b_s, w_out):
    n_tok = x2.shape[0]
    tm = GMLP_TOKENS
    const = lambda i: (0, 0)
    return pl.pallas_call(
        _gmlp_kernel,
        grid=(n_tok // tm,),
        in_specs=[
            pl.BlockSpec((tm, D_MODEL), lambda i: (i, 0)),
            pl.BlockSpec((1, D_MODEL), const),
            pl.BlockSpec((1, D_MODEL), const),
            pl.BlockSpec((D_MODEL, 3 * GMLP_WIDTH), const, pipeline_mode=pl.Buffered(1)),
            pl.BlockSpec((1, GMLP_WIDTH), const),
            pl.BlockSpec((1, GMLP_WIDTH), const),
            pl.BlockSpec((GMLP_GROUPS, GMLP_CHUNK, GMLP_CHUNK), lambda i: (0, 0, 0)),
            pl.BlockSpec((GMLP_CHUNK, GMLP_GROUPS), const),
            pl.BlockSpec((GMLP_WIDTH, D_MODEL), const, pipeline_mode=pl.Buffered(1)),
        ],
        out_specs=pl.BlockSpec((tm, D_MODEL), lambda i: (i, 0)),
        out_shape=jax.ShapeDtypeStruct(x2.shape, F32),
        scratch_shapes=[
            pltpu.VMEM((tm, D_MODEL), BF16),
            pltpu.VMEM((tm, GMLP_WIDTH), F32),
            pltpu.VMEM((tm, GMLP_WIDTH), F32),
            pltpu.VMEM((tm, GMLP_WIDTH), BF16),
        ],
        compiler_params=pltpu.CompilerParams(
            dimension_semantics=("arbitrary",), vmem_limit_bytes=VMEM_LIMIT),
        name="gmlp_layer",
    )(x2, gpre.reshape(1, -1), gpost.reshape(1, -1), w_in,
      ln_g.reshape(1, -1), ln_b.reshape(1, -1), w_s, jnp.transpose(b_s), w_out)


def _proj_kernel(x_ref, gpre_ref, w_ref, cos_ref, sin_ref, o_ref,
                 hs_sc, hs1_sc, h_sc, cos1_sc, sin1_sc, cos2_sc, sin2_sc):
    j = pl.program_id(1)
    n_chunks = TILE // PROJ_ROW_CHUNK
    n_slabs = D_MODEL // LANES
    first1, first2 = GROUP_FIRST_SLAB[1], GROUP_FIRST_SLAB[2]
    d1, d2 = DILATED_GROUPS[1][1], DILATED_GROUPS[2][1]
    ratio = d2 // d1

    @pl.when(j == 0)
    def _():
        for rc in range(n_chunks):
            rows = slice(rc * PROJ_ROW_CHUNK, (rc + 1) * PROJ_ROW_CHUNK)
            hs = _rms_normalize(x_ref[rows, :], gpre_ref[...])
            for c in range(n_slabs):
                hs_sc[c, rows, :] = hs[:, c * LANES:(c + 1) * LANES]

    @pl.when(j == first1)
    def _():
        rows = TILE // d1
        for b in range(d1):
            out, src = slice(b * rows, (b + 1) * rows), pl.ds(b, rows, stride=d1)
            for c in range(n_slabs):
                v = hs_sc[c, src, :]
                hs1_sc[c, out, :] = v
                h_sc[out, c * LANES:(c + 1) * LANES] = v.astype(BF16)
            cos1_sc[out, :] = cos_ref[src, :]
            sin1_sc[out, :] = sin_ref[src, :]

    @pl.when(j == first2)
    def _():
        rows = TILE // d2
        for b in range(d1):
            for a in range(ratio):
                r = a * d1 + b
                out, src = slice(r * rows, (r + 1) * rows), pl.ds(b * (TILE // d1) + a, rows, stride=ratio)
                for c in range(n_slabs):
                    h_sc[out, c * LANES:(c + 1) * LANES] = hs1_sc[c, src, :].astype(BF16)
                cos2_sc[out, :] = cos1_sc[src, :]
                sin2_sc[out, :] = sin1_sc[src, :]

    is_q = functools.reduce(jnp.logical_or, [j == s for s in GROUP_FIRST_SLAB])
    is_k = functools.reduce(jnp.logical_or, [j == s + 1 for s in GROUP_FIRST_SLAB])
    is_rope = jnp.logical_or(is_q, is_k)
    in_group0 = j < first1
    in_group1 = jnp.logical_and(j >= first1, j < first2)
    in_group2 = j >= first2

    def lhs_rows(from_token_order, rows):
        if from_token_order:
            return jnp.concatenate([hs_sc[c, rows, :] for c in range(n_slabs)], axis=1).astype(BF16)
        return h_sc[rows, :]

    def rope_slab(from_token_order, cos_tab, sin_tab):
        scale = jnp.where(is_q, np.float32(np.log2(np.e) / np.sqrt(HEAD_DIM)), np.float32(1.0))
        for rc in range(n_chunks):
            rows = slice(rc * PROJ_ROW_CHUNK, (rc + 1) * PROJ_ROW_CHUNK)
            p = jnp.dot(lhs_rows(from_token_order, rows), w_ref[...], preferred_element_type=F32)
            cos = cos_tab[rows, :] * scale
            sin = sin_tab[rows, :] * scale
            for s in range(ATTN_WIDTH // LANES):
                t = p[:, s * LANES:(s + 1) * LANES]
                o_ref[rows, s * LANES:(s + 1) * LANES] = (
                    t * cos + pltpu.roll(t, LANES // 2, 1) * sin).astype(BF16)

    def plain_slab(from_token_order):
        for rc in range(n_chunks):
            rows = slice(rc * PROJ_ROW_CHUNK, (rc + 1) * PROJ_ROW_CHUNK)
            o_ref[rows, :] = jnp.dot(lhs_rows(from_token_order, rows), w_ref[...],
                                     preferred_element_type=F32).astype(BF16)

    pl.when(jnp.logical_and(is_rope, in_group0))(functools.partial(rope_slab, True, cos_ref, sin_ref))
    pl.when(jnp.logical_and(is_rope, in_group1))(functools.partial(rope_slab, False, cos1_sc, sin1_sc))
    pl.when(jnp.logical_and(is_rope, in_group2))(functools.partial(rope_slab, False, cos2_sc, sin2_sc))
    is_plain = jnp.logical_not(is_rope)
    pl.when(jnp.logical_and(is_plain, in_group0))(functools.partial(plain_slab, True))
    pl.when(jnp.logical_and(is_plain, jnp.logical_not(in_group0)))(functools.partial(plain_slab, False))


def _proj_weight_kernel(w_ref, o_ref):
    s = pl.program_id(0)
    is_qk = functools.reduce(jnp.logical_or, [jnp.logical_or(s == f, s == f + 1) for f in GROUP_FIRST_SLAB])

    @pl.when(is_qk)
    def _():
        half = HEAD_DIM // 2
        quarter = lax.broadcasted_iota(jnp.int32, (1, LANES), 1) // half
        for c in range(ATTN_WIDTH // LANES):
            sl = slice(c * LANES, (c + 1) * LANES)
            w = w_ref[0, :, sl]
            w = jnp.where(quarter == 1, pltpu.roll(w, LANES - half, 1),
                          jnp.where(quarter == 2, pltpu.roll(w, half, 1), w))
            o_ref[:, sl] = w.astype(BF16)

    @pl.when(jnp.logical_not(is_qk))
    def _():
        o_ref[...] = w_ref[0].astype(BF16)


def _proj_weight(w_in_all, layer):
    def source_slab(s):
        return jnp.where(s == GATE_OUT_SLAB, Z_SLAB, jnp.where(s < GATE_OUT_SLAB, s, s - 1))

    return pl.pallas_call(
        _proj_weight_kernel,
        grid=(N_PROJ_SLABS,),
        in_specs=[pl.BlockSpec((1, D_MODEL, ATTN_WIDTH), lambda s: (layer, 0, source_slab(s)))],
        out_specs=pl.BlockSpec((D_MODEL, ATTN_WIDTH), lambda s: (0, s)),
        out_shape=jax.ShapeDtypeStruct((D_MODEL, N_PROJ_SLABS * ATTN_WIDTH), BF16),
        compiler_params=pltpu.CompilerParams(
            dimension_semantics=("arbitrary",), vmem_limit_bytes=VMEM_LIMIT),
        name="proj_weight_prep",
    )(w_in_all)


def _cast_kernel(w_ref, o_ref):
    o_ref[...] = w_ref[0].astype(BF16)


def _bf16_weight(w_all, layer):
    _, rows, cols = w_all.shape
    block_rows = rows
    while block_rows * cols * 4 > WEIGHT_CAST_BLOCK_BYTES:
        block_rows //= 2
    assert rows % block_rows == 0 and block_rows % 16 == 0
    return pl.pallas_call(
        _cast_kernel,
        grid=(rows // block_rows,),
        in_specs=[pl.BlockSpec((1, block_rows, cols), lambda i: (layer, i, 0))],
        out_specs=pl.BlockSpec((block_rows, cols), lambda i: (i, 0)),
        out_shape=jax.ShapeDtypeStruct((rows, cols), BF16),
        compiler_params=pltpu.CompilerParams(
            dimension_semantics=("arbitrary",), vmem_limit_bytes=VMEM_LIMIT),
        name="weight_cast",
    )(w_all)


def _proj_layer(x2, gpre, w_in, cos2, sin2, seq_len):
    n_tok = x2.shape[0]
    tiles_per_seq = seq_len // TILE
    return pl.pallas_call(
        _proj_kernel,
        grid=(n_tok // TILE, N_PROJ_SLABS),
        in_specs=[
            pl.BlockSpec((TILE, D_MODEL), lambda i, j: (i, 0)),
            pl.BlockSpec((1, D_MODEL), lambda i, j: (0, 0)),
            pl.BlockSpec((D_MODEL, ATTN_WIDTH), lambda i, j: (0, j)),
            pl.BlockSpec((TILE, LANES), lambda i, j: (i % tiles_per_seq, 0)),
            pl.BlockSpec((TILE, LANES), lambda i, j: (i % tiles_per_seq, 0)),
        ],
        out_specs=pl.BlockSpec((TILE, ATTN_WIDTH), lambda i, j: (i, j)),
        out_shape=jax.ShapeDtypeStruct((n_tok, N_PROJ_SLABS * ATTN_WIDTH), BF16),
        scratch_shapes=[
            pltpu.VMEM((D_MODEL // LANES, TILE, LANES), F32),
            pltpu.VMEM((D_MODEL // LANES, TILE, LANES), F32),
            pltpu.VMEM((TILE, D_MODEL), BF16),
        ] + 4 * [pltpu.VMEM((TILE, LANES), F32)],
        compiler_params=pltpu.CompilerParams(
            dimension_semantics=("arbitrary", "arbitrary"), vmem_limit_bytes=VMEM_LIMIT),
        name="attn_proj",
    )(x2, gpre.reshape(1, -1), w_in, cos2, sin2)


def _attn_masks():
    t = ATTN_BLOCK
    kj = np.arange(2 * t)[:, None]
    qc = np.arange(t)[None, :]
    cur = (kj >= t) & (kj - t <= qc)
    prev = (kj < t) & (kj >= qc)
    valid = np.stack([cur, cur | prev])
    return jnp.asarray(np.where(valid, 0.0, NEG_BIG), dtype=BF16)


def _attn_ring_kernel(q_ref, kv_hbm, mask_ref, eye_ref, o_ref, st_ref, k_ring, v_ring, sem, *,
                      chain, first_slab, n_steps):
    n = pl.program_id(1)
    step = pl.program_id(0) * pl.num_programs(1) + n

    def copies(tile_idx, slot):
        rows = pl.ds(pl.multiple_of(tile_idx * TILE, TILE), TILE)
        return [pltpu.make_async_copy(
            kv_hbm.at[rows, pl.ds((first_slab + 1 + c) * ATTN_WIDTH, ATTN_WIDTH)], ring.at[slot], sem.at[c, slot])
            for c, ring in enumerate((k_ring, v_ring))]

    @pl.when(step == 0)
    def _():
        for cp in copies(0, 0):
            cp.start()

    @pl.when(step + 1 < n_steps)
    def _():
        for cp in copies(step + 1, lax.rem(step + 1, KV_SLOTS)):
            cp.start()

    cur = lax.rem(step, KV_SLOTS)
    for cp in copies(step, cur):
        cp.wait()
    prev = jnp.where(n == 0, cur, lax.rem(step + KV_SLOTS - 1, KV_SLOTS))
    _attn_kernel(q_ref, k_ring.at[prev], k_ring.at[cur], v_ring.at[prev], v_ring.at[cur], mask_ref, eye_ref,
                 o_ref, st_ref, chain=chain, first_tile=n == 0)


def _attn_kernel(q_ref, kp_ref, kc_ref, vp_ref, vc_ref, mask_ref, eye_ref, o_ref, st_ref, *, chain, first_tile):
    t = ATTN_BLOCK
    first_step_mask = mask_ref[jnp.where(first_tile, 0, 1)]
    eye = eye_ref[...]
    ones = jnp.ones((2 * t, LANES), BF16)
    lane = lax.broadcasted_iota(jnp.int32, (1, LANES), 1)
    head_a_qk = ((lane // (HEAD_DIM // 2)) % 2) == 0
    head_a_v = (lane // HEAD_DIM) == 0
    for jb in range(q_ref.shape[0] // t):
        rows = slice(jb * t, (jb + 1) * t)
        if jb % chain > 0:
            prev_k, prev_v, prev_rows, mask_t = kc_ref, vc_ref, slice((jb - 1) * t, jb * t), mask_ref[1]
        else:
            prev_k, prev_v, mask_t = kp_ref, vp_ref, first_step_mask
            prev_rows = slice((jb + chain - 1) * t, (jb + chain) * t)
        st = jnp.zeros((t, LANES), F32)
        for hp in range(N_HEADS // 2):
            sl = slice(hp * LANES, (hp + 1) * LANES)
            q2 = q_ref[rows, sl]
            zero = jnp.zeros_like(q2)
            lhs = jnp.concatenate([
                jnp.concatenate([jnp.where(head_a_qk, q2, zero), eye], axis=1),
                jnp.concatenate([jnp.where(head_a_qk, zero, q2), eye], axis=1)], axis=0)
            k_ext = jnp.concatenate(
                [jnp.concatenate([prev_k[prev_rows, sl], kc_ref[rows, sl]], axis=0), mask_t], axis=1)
            v_ext = jnp.concatenate(
                [jnp.concatenate([prev_v[prev_rows, sl], vc_ref[rows, sl]], axis=0), ones], axis=1)
            s = lax.dot_general(lhs, k_ext, (((1,), (1,)), ((), ())), preferred_element_type=F32)
            m = jnp.max(s, axis=-1, keepdims=True)
            p = jnp.exp2(s - m).astype(BF16)
            pv = jnp.dot(p, v_ext, preferred_element_type=F32)
            o_ref[rows, sl] = jnp.where(head_a_v, pv[:t, :LANES], pv[t:, :LANES]).astype(BF16)
            denom = pv[:, LANES:]
            st = jnp.where(lane == 2 * hp, m[:t], st)
            st = jnp.where(lane == 2 * hp + 1, m[t:], st)
            st = jnp.where(lane == N_HEADS + 2 * hp, denom[:t], st)
            st = jnp.where(lane == N_HEADS + 2 * hp + 1, denom[t:], st)
        st_ref[rows, :] = st


def _attn_group(proj2, g, dilation, batch, seq_len):
    n_tok = batch * seq_len
    t = ATTN_BLOCK
    tiles_per_seq = seq_len // TILE
    chain = TILE // t // dilation
    first = GROUP_FIRST_SLAB[g]
    tile = lambda b, n: b * tiles_per_seq + n
    return pl.pallas_call(
        functools.partial(_attn_ring_kernel, chain=chain, first_slab=first, n_steps=batch * tiles_per_seq),
        grid=(batch, tiles_per_seq),
        in_specs=[pl.BlockSpec((TILE, ATTN_WIDTH), lambda b, n: (tile(b, n), first)),
                  pl.BlockSpec(memory_space=pl.ANY),
                  pl.BlockSpec((2, 2 * t, t), lambda b, n: (0, 0, 0)),
                  pl.BlockSpec((t, t), lambda b, n: (0, 0))],
        out_specs=[
            pl.BlockSpec((TILE, ATTN_WIDTH), lambda b, n: (tile(b, n), 0)),
            pl.BlockSpec((TILE, LANES), lambda b, n: (tile(b, n), 0)),
        ],
        out_shape=[
            jax.ShapeDtypeStruct((n_tok, ATTN_WIDTH), BF16),
            jax.ShapeDtypeStruct((n_tok, LANES), F32),
        ],
        scratch_shapes=[
            pltpu.VMEM((KV_SLOTS, TILE, ATTN_WIDTH), BF16),
            pltpu.VMEM((KV_SLOTS, TILE, ATTN_WIDTH), BF16),
            pltpu.SemaphoreType.DMA((2, KV_SLOTS)),
        ],
        compiler_params=pltpu.CompilerParams(
            dimension_semantics=("arbitrary", "arbitrary"), vmem_limit_bytes=VMEM_LIMIT),
        name=f"dilated_attn_g{g}",
    )(proj2, proj2, _attn_masks(), jnp.eye(t, dtype=BF16))


def _combine_kernel(x_ref, o0_ref, o1_ref, o2_ref, s0_ref, s1_ref, s2_ref, z_ref,
                    expand_ref, wout_ref, gpost_ref, out_ref, on1_sc, on2_sc, tmp_sc, st1_sc, st2_sc,
                    tmp_st_sc, y_sc):
    n_slabs = ATTN_WIDTH // LANES
    is_head_lane = lax.broadcasted_iota(jnp.int32, (1, LANES), 1) < N_HEADS
    for sub in range(COMBINE_TOKENS // COMBINE_SUB):
        base = sub * COMBINE_SUB
        rows = slice(base, base + COMBINE_SUB)
        d1, d2 = DILATED_GROUPS[1][1], DILATED_GROUPS[2][1]
        n = COMBINE_SUB // d1
        src = slice(sub * n, (sub + 1) * n)
        for b in range(d1):
            dst = pl.ds(base + b, n, stride=d1)
            st1_sc[dst, :] = s1_ref[0, b, src, :]
            for c in range(n_slabs):
                on1_sc[c, dst, :] = o1_ref[0, b, src, c * LANES:(c + 1) * LANES].astype(F32)
        ratio = d2 // d1
        n2 = COMBINE_SUB // d2
        src2 = slice(sub * n2, (sub + 1) * n2)
        for b in range(d1):
            mid = slice(base + b * n, base + (b + 1) * n)
            for a in range(ratio):
                dst = pl.ds(base + b * n + a, n2, stride=ratio)
                tmp_st_sc[dst, :] = s2_ref[0, a * d1 + b, src2, :]
                for c in range(n_slabs):
                    tmp_sc[c, dst, :] = o2_ref[0, a * d1 + b, src2, c * LANES:(c + 1) * LANES].astype(F32)
            dst = pl.ds(base + b, n, stride=d1)
            st2_sc[dst, :] = tmp_st_sc[mid, :]
            for c in range(n_slabs):
                on2_sc[c, dst, :] = tmp_sc[c, mid, :]

        sts = [s0_ref[rows, :], st1_sc[rows, :], st2_sc[rows, :]]
        m = jnp.maximum(jnp.maximum(sts[0], sts[1]), sts[2])
        es = [jnp.exp2(s - m) for s in sts]
        total = None
        for e, s in zip(es, sts):
            term = e * pltpu.roll(s, LANES - N_HEADS, 1)
            total = term if total is None else total + term
        inv = 1.0 / jnp.where(is_head_lane, total, 1.0)
        alphas = []
        for e in es:
            alpha = jnp.where(is_head_lane, e * inv, 0.0)
            hi = alpha.astype(BF16)
            lo = (alpha - hi.astype(F32)).astype(BF16)
            alphas.append(jnp.dot(jnp.concatenate([hi, lo], axis=1), expand_ref[...],
                                  preferred_element_type=F32))
        for c in range(n_slabs):
            sl = slice(c * LANES, (c + 1) * LANES)
            o = (alphas[0][:, sl] * o0_ref[rows, sl].astype(F32)
                 + alphas[1][:, sl] * on1_sc[c, rows, :] + alphas[2][:, sl] * on2_sc[c, rows, :])
            y_sc[rows, sl] = (o * _silu(z_ref[rows, sl].astype(F32))).astype(BF16)
        out = jnp.dot(y_sc[rows, :], wout_ref[...], preferred_element_type=F32)
        out_ref[rows, :] = x_ref[rows, :] + _rms_normalize(out, gpost_ref[...])


def _combine_layer(x2, outs, stats, proj2, w_out, gpost):
    n_tok = x2.shape[0]
    tm = COMBINE_TOKENS
    n_tiles = n_tok // TILE
    per_tile = TILE // tm
    head_of_lane = np.arange(ATTN_WIDTH) // HEAD_DIM
    expand = jnp.asarray(np.tile(np.arange(LANES)[:, None] == head_of_lane[None, :], (2, 1)), dtype=BF16)
    tok = lambda i: (i, 0)
    const = lambda i: (0, 0)
    in_tile = lambda i: (i // per_tile, 0, i % per_tile, 0)
    d1, d2 = DILATED_GROUPS[1][1], DILATED_GROUPS[2][1]
    return pl.pallas_call(
        _combine_kernel,
        grid=(n_tok // tm,),
        in_specs=[
            pl.BlockSpec((tm, D_MODEL), tok),
            pl.BlockSpec((tm, ATTN_WIDTH), tok),
            pl.BlockSpec((1, d1, tm // d1, ATTN_WIDTH), in_tile),
            pl.BlockSpec((1, d2, tm // d2, ATTN_WIDTH), in_tile),
            pl.BlockSpec((tm, LANES), tok),
            pl.BlockSpec((1, d1, tm // d1, LANES), in_tile),
            pl.BlockSpec((1, d2, tm // d2, LANES), in_tile),
            pl.BlockSpec((tm, ATTN_WIDTH), lambda i: (i, GATE_OUT_SLAB)),
            pl.BlockSpec((2 * LANES, ATTN_WIDTH), const),
            pl.BlockSpec((ATTN_WIDTH, D_MODEL), const),
            pl.BlockSpec((1, D_MODEL), const),
        ],
        out_specs=pl.BlockSpec((tm, D_MODEL), tok),
        out_shape=jax.ShapeDtypeStruct(x2.shape, F32),
        scratch_shapes=[
            pltpu.VMEM((ATTN_WIDTH // LANES, tm, LANES), F32),
            pltpu.VMEM((ATTN_WIDTH // LANES, tm, LANES), F32),
            pltpu.VMEM((ATTN_WIDTH // LANES, tm, LANES), F32),
            pltpu.VMEM((tm, LANES), F32),
            pltpu.VMEM((tm, LANES), F32),
            pltpu.VMEM((tm, LANES), F32),
            pltpu.VMEM((tm, ATTN_WIDTH), BF16),
        ],
        compiler_params=pltpu.CompilerParams(
            dimension_semantics=("arbitrary",), vmem_limit_bytes=VMEM_LIMIT),
        name="attn_combine",
    )(x2,
      outs[0],
      outs[1].reshape(n_tiles, d1, TILE // d1, ATTN_WIDTH),
      outs[2].reshape(n_tiles, d2, TILE // d2, ATTN_WIDTH),
      stats[0],
      stats[1].reshape(n_tiles, d1, TILE // d1, LANES),
      stats[2].reshape(n_tiles, d2, TILE // d2, LANES),
      proj2, expand, w_out, gpost.reshape(1, -1))


def _rope_tables(seq_len):
    half = HEAD_DIM // 2
    exponent = np.arange(0, HEAD_DIM, 2, dtype=np.float32) / np.float32(HEAD_DIM)
    power = np.power(np.float64(ROPE_THETA), exponent.astype(np.float64)).astype(np.float32)
    inv_freq = np.float32(1.0) / power
    ang = np.arange(seq_len, dtype=np.float32)[:, None] * inv_freq[None, :]
    cos = np.cos(ang.astype(np.float64)).astype(np.float32)
    sin = np.sin(ang.astype(np.float64)).astype(np.float32)
    cos2 = np.tile(cos, (1, LANES // half))
    sin2 = np.concatenate([-sin, -sin, sin, sin], axis=1)
    return jnp.asarray(cos2), jnp.asarray(sin2)


def kernel(x, norm_pre, norm_post, a_w_in, a_ln_g, a_ln_b, a_w_s, a_b_s, a_w_out, b_w_in, b_w_out):
    batch, seq_len, d = x.shape
    depth = norm_pre.shape[0]
    cos2, sin2 = _rope_tables(seq_len)
    x2 = x.reshape(batch * seq_len, d)
    for i in range(depth):
        j = i // 2
        if i % 2 == 0:
            x2 = _gmlp_layer(x2, norm_pre[i], norm_post[i], _bf16_weight(a_w_in, j), a_ln_g[j], a_ln_b[j],
                             a_w_s[j], a_b_s[j], _bf16_weight(a_w_out, j))
        else:
            proj2 = _proj_layer(x2, norm_pre[i], _proj_weight(b_w_in, j), cos2, sin2, seq_len)
            outs, stats = [], []
            for g, (_, dilation) in enumerate(DILATED_GROUPS):
                o, st = _attn_group(proj2, g, dilation, batch, seq_len)
                outs.append(o)
                stats.append(st)
            x2 = _combine_layer(x2, outs, stats, proj2, _bf16_weight(b_w_out, j), norm_post[i])
    return x2.reshape(batch, seq_len, d)
```

```python
import functools

import numpy as np
import jax
import jax.numpy as jnp
from jax import lax
from jax.experimental import pallas as pl
from jax.experimental.pallas import tpu as pltpu

F32 = jnp.float32
BF16 = jnp.bfloat16

D_MODEL = 1024
RMS_EPS = 1e-6
LN_EPS = 1e-5

GMLP_WIDTH = 2048
GMLP_CHUNK = 128
GMLP_GROUPS = 8
GMLP_GROUP_WIDTH = GMLP_WIDTH // GMLP_GROUPS
GMLP_DOT_GROUPS = 4

HEAD_DIM = 64
N_HEADS = 16
ATTN_WIDTH = N_HEADS * HEAD_DIM
DILATED_GROUPS = ((128, 1), (512, 4), (2048, 16))
N_DIL_GROUPS = len(DILATED_GROUPS)
ATTN_BLOCK = 128
KV_SLOTS = 3
ROPE_THETA = 10000.0
N_PROJ_SLABS = 3 * N_DIL_GROUPS + 1
Z_SLAB = 3 * N_DIL_GROUPS
GROUP_FIRST_SLAB = (0, 4, 7)
GATE_OUT_SLAB = 3
TILE = 2048

LANES = 128
NEG_BIG = -1e30

GMLP_TOKENS = 512
PROJ_ROW_CHUNK = 512
COMBINE_TOKENS = 1024
COMBINE_SUB = 512
VMEM_LIMIT = 60 * 1024 * 1024
WEIGHT_CAST_BLOCK_BYTES = 4 * 1024 * 1024


def _gelu(x):
    return 0.5 * x * (1.0 + lax.erf(x * np.float32(np.sqrt(0.5))))


def _silu(x):
    return x * jax.nn.sigmoid(x)


def _rms_normalize(x, gain):
    ms = jnp.mean(x * x, axis=-1, keepdims=True)
    return x * lax.rsqrt(ms + RMS_EPS) * gain


def _gmlp_kernel(x_ref, gpre_ref, gpost_ref, win_ref, lng_ref, lnb_ref, ws_ref,
                 bs_ref, wout_ref, o_ref, h_sc, v_sc, t_sc, y_sc):
    tm = x_ref.shape[0]
    gw = GMLP_GROUP_WIDTH
    x = x_ref[...]
    h_sc[...] = _rms_normalize(x, gpre_ref[...]).astype(BF16)

    pw = GMLP_DOT_GROUPS * gw
    s1 = jnp.zeros((tm, 1), F32)
    for gp in range(GMLP_WIDTH // pw):
        c0 = GMLP_WIDTH + gp * pw
        v = _gelu(jnp.dot(h_sc[...], win_ref[:, c0:c0 + pw], preferred_element_type=F32))
        v_sc[:, gp * pw:(gp + 1) * pw] = v
        s1 = s1 + jnp.sum(v, axis=-1, keepdims=True)
    for gp in range(GMLP_WIDTH // pw):
        psl = slice(gp * pw, (gp + 1) * pw)
        u = _gelu(jnp.dot(h_sc[...], win_ref[:, psl], preferred_element_type=F32))
        z0 = 2 * GMLP_WIDTH + gp * pw
        z = jnp.dot(h_sc[...], win_ref[:, z0:z0 + pw], preferred_element_type=F32)
        t_sc[:, psl] = u * _silu(z)

    mu = s1 * (1.0 / GMLP_WIDTH)
    s2 = jnp.zeros((tm, 1), F32)
    for g in range(GMLP_GROUPS):
        dv = v_sc[:, g * gw:(g + 1) * gw] - mu
        s2 = s2 + jnp.sum(dv * dv, axis=-1, keepdims=True)
    rstd = lax.rsqrt(s2 * (1.0 / GMLP_WIDTH) + LN_EPS)

    row = lax.broadcasted_iota(jnp.int32, (GMLP_CHUNK, GMLP_CHUNK), 0)
    col = lax.broadcasted_iota(jnp.int32, (GMLP_CHUNK, GMLP_CHUNK), 1)
    causal = col <= row
    for g in range(GMLP_GROUPS):
        sl = slice(g * gw, (g + 1) * gw)
        vn = ((v_sc[:, sl] - mu) * rstd * lng_ref[:, sl] + lnb_ref[:, sl]).astype(BF16)
        wm = jnp.where(causal, ws_ref[g], 0.0).astype(BF16)
        bias = bs_ref[:, g:g + 1]
        sv = jnp.concatenate(
            [jnp.dot(wm, vn[c * GMLP_CHUNK:(c + 1) * GMLP_CHUNK], preferred_element_type=F32) + bias
             for c in range(tm // GMLP_CHUNK)], axis=0)
        y_sc[:, sl] = (t_sc[:, sl] * sv).astype(BF16)

    out = jnp.dot(y_sc[...], wout_ref[...], preferred_element_type=F32)
    o_ref[...] = x_ref[...] + _rms_normalize(out, gpost_ref[...])


def _gmlp_layer(x2, gpre, gpost, w_in, ln_g, ln_b, w_s, b_s, w_out):
    n_tok = x2.shape[0]
    tm = GMLP_TOKENS
    const = lambda i: (0, 0)
    return pl.pallas_call(
        _gmlp_kernel,
        grid=(n_tok // tm,),
        in_specs=[
            pl.BlockSpec((tm, D_MODEL), lambda i: (i, 0)),
            pl.BlockSpec((1, D_MODEL), const),
            pl.BlockSpec((1, D_MODEL), const),
            pl.BlockSpec((D_MODEL, 3 * GMLP_WIDTH), const, pipeline_mode=pl.Buffered(1)),
            pl.BlockSpec((1, GMLP_WIDTH), const),
            pl.BlockSpec((1, GMLP_WIDTH), const),
            pl.BlockSpec((GMLP_GROUPS, GMLP_CHUNK, GMLP_CHUNK), lambda i: (0, 0, 0)),
            pl.BlockSpec((GMLP_CHUNK, GMLP_GROUPS), const),
            pl.BlockSpec((GMLP_WIDTH, D_MODEL), const, pipeline_mode=pl.Buffered(1)),
        ],
        out_specs=pl.BlockSpec((tm, D_MODEL), lambda i: (i, 0)),
        out_shape=jax.ShapeDtypeStruct(x2.shape, F32),
        scratch_shapes=[
            pltpu.VMEM((tm, D_MODEL), BF16),
            pltpu.VMEM((tm, GMLP_WIDTH), F32),
            pltpu.VMEM((tm, GMLP_WIDTH), F32),
            pltpu.VMEM((tm, GMLP_WIDTH), BF16),
        ],
        compiler_params=pltpu.CompilerParams(
            dimension_semantics=("arbitrary",), vmem_limit_bytes=VMEM_LIMIT),
        name="gmlp_layer",
    )(x2, gpre.reshape(1, -1), gpost.reshape(1, -1), w_in,
      ln_g.reshape(1, -1), ln_b.reshape(1, -1), w_s, jnp.transpose(b_s), w_out)


def _proj_kernel(x_ref, gpre_ref, w_ref, cos_ref, sin_ref, o_ref,
                 hs_sc, hs1_sc, h_sc, cos1_sc, sin1_sc, cos2_sc, sin2_sc):
    j = pl.program_id(1)
    n_chunks = TILE // PROJ_ROW_CHUNK
    n_slabs = D_MODEL // LANES
    first1, first2 = GROUP_FIRST_SLAB[1], GROUP_FIRST_SLAB[2]
    d1, d2 = DILATED_GROUPS[1][1], DILATED_GROUPS[2][1]
    ratio = d2 // d1

    @pl.when(j == 0)
    def _():
        for rc in range(n_chunks):
            rows = slice(rc * PROJ_ROW_CHUNK, (rc + 1) * PROJ_ROW_CHUNK)
            hs = _rms_normalize(x_ref[rows, :], gpre_ref[...])
            for c in range(n_slabs):
                hs_sc[c, rows, :] = hs[:, c * LANES:(c + 1) * LANES]

    @pl.when(j == first1)
    def _():
        rows = TILE // d1
        for b in range(d1):
            out, src = slice(b * rows, (b + 1) * rows), pl.ds(b, rows, stride=d1)
            for c in range(n_slabs):
                v = hs_sc[c, src, :]
                hs1_sc[c, out, :] = v
                h_sc[out, c * LANES:(c + 1) * LANES] = v.astype(BF16)
            cos1_sc[out, :] = cos_ref[src, :]
            sin1_sc[out, :] = sin_ref[src, :]

    @pl.when(j == first2)
    def _():
        rows = TILE // d2
        for b in range(d1):
            for a in range(ratio):
                r = a * d1 + b
                out, src = slice(r * rows, (r + 1) * rows), pl.ds(b * (TILE // d1) + a, rows, stride=ratio)
                for c in range(n_slabs):
                    h_sc[out, c * LANES:(c + 1) * LANES] = hs1_sc[c, src, :].astype(BF16)
                cos2_sc[out, :] = cos1_sc[src, :]
                sin2_sc[out, :] = sin1_sc[src, :]

    is_q = functools.reduce(jnp.logical_or, [j == s for s in GROUP_FIRST_SLAB])
    is_k = functools.reduce(jnp.logical_or, [j == s + 1 for s in GROUP_FIRST_SLAB])
    is_rope = jnp.logical_or(is_q, is_k)
    in_group0 = j < first1
    in_group1 = jnp.logical_and(j >= first1, j < first2)
    in_group2 = j >= first2

    def lhs_rows(from_token_order, rows):
        if from_token_order:
            return jnp.concatenate([hs_sc[c, rows, :] for c in range(n_slabs)], axis=1).astype(BF16)
        return h_sc[rows, :]

    def rope_slab(from_token_order, cos_tab, sin_tab):
        scale = jnp.where(is_q, np.float32(np.log2(np.e) / np.sqrt(HEAD_DIM)), np.float32(1.0))
        for rc in range(n_chunks):
            rows = slice(rc * PROJ_ROW_CHUNK, (rc + 1) * PROJ_ROW_CHUNK)
            p = jnp.dot(lhs_rows(from_token_order, rows), w_ref[...], preferred_element_type=F32)
            cos = cos_tab[rows, :] * scale
            sin = sin_tab[rows, :] * scale
            for s in range(ATTN_WIDTH // LANES):
                t = p[:, s * LANES:(s + 1) * LANES]
                o_ref[rows, s * LANES:(s + 1) * LANES] = (
                    t * cos + pltpu.roll(t, LANES // 2, 1) * sin).astype(BF16)

    def plain_slab(from_token_order):
        for rc in range(n_chunks):
            rows = slice(rc * PROJ_ROW_CHUNK, (rc + 1) * PROJ_ROW_CHUNK)
            o_ref[rows, :] = jnp.dot(lhs_rows(from_token_order, rows), w_ref[...],
                                     preferred_element_type=F32).astype(BF16)

    pl.when(jnp.logical_and(is_rope, in_group0))(functools.partial(rope_slab, True, cos_ref, sin_ref))
    pl.when(jnp.logical_and(is_rope, in_group1))(functools.partial(rope_slab, False, cos1_sc, sin1_sc))
    pl.when(jnp.logical_and(is_rope, in_group2))(functools.partial(rope_slab, False, cos2_sc, sin2_sc))
    is_plain = jnp.logical_not(is_rope)
    pl.when(jnp.logical_and(is_plain, in_group0))(functools.partial(plain_slab, True))
    pl.when(jnp.logical_and(is_plain, jnp.logical_not(in_group0)))(functools.partial(plain_slab, False))


def _proj_weight_kernel(w_ref, o_ref):
    s = pl.program_id(0)
    is_qk = functools.reduce(jnp.logical_or, [jnp.logical_or(s == f, s == f + 1) for f in GROUP_FIRST_SLAB])

    @pl.when(is_qk)
    def _():
        half = HEAD_DIM // 2
        quarter = lax.broadcasted_iota(jnp.int32, (1, LANES), 1) // half
        for c in range(ATTN_WIDTH // LANES):
            sl = slice(c * LANES, (c + 1) * LANES)
            w = w_ref[0, :, sl]
            w = jnp.where(quarter == 1, pltpu.roll(w, LANES - half, 1),
                          jnp.where(quarter == 2, pltpu.roll(w, half, 1), w))
            o_ref[:, sl] = w.astype(BF16)

    @pl.when(jnp.logical_not(is_qk))
    def _():
        o_ref[...] = w_ref[0].astype(BF16)


def _proj_weight(w_in_all, layer):
    def source_slab(s):
        return jnp.where(s == GATE_OUT_SLAB, Z_SLAB, jnp.where(s < GATE_OUT_SLAB, s, s - 1))

    return pl.pallas_call(
        _proj_weight_kernel,
        grid=(N_PROJ_SLABS,),
        in_specs=[pl.BlockSpec((1, D_MODEL, ATTN_WIDTH), lambda s: (layer, 0, source_slab(s)))],
        out_specs=pl.BlockSpec((D_MODEL, ATTN_WIDTH), lambda s: (0, s)),
        out_shape=jax.ShapeDtypeStruct((D_MODEL, N_PROJ_SLABS * ATTN_WIDTH), BF16),
        compiler_params=pltpu.CompilerParams(
            dimension_semantics=("arbitrary",), vmem_limit_bytes=VMEM_LIMIT),
        name="proj_weight_prep",
    )(w_in_all)


def _cast_kernel(w_ref, o_ref):
    o_ref[...] = w_ref[0].astype(BF16)


def _bf16_weight(w_all, layer):
    _, rows, cols = w_all.shape
    block_rows = rows
    while block_rows * cols * 4 > WEIGHT_CAST_BLOCK_BYTES:
        block_rows //= 2
    assert rows % block_rows == 0 and block_rows % 16 == 0
    return pl.pallas_call(
        _cast_kernel,
        grid=(rows // block_rows,),
        in_specs=[pl.BlockSpec((1, block_rows, cols), lambda i: (layer, i, 0))],
        out_specs=pl.BlockSpec((block_rows, cols), lambda i: (i, 0)),
        out_shape=jax.ShapeDtypeStruct((rows, cols), BF16),
        compiler_params=pltpu.CompilerParams(
            dimension_semantics=("arbitrary",), vmem_limit_bytes=VMEM_LIMIT),
        name="weight_cast",
    )(w_all)


def _proj_layer(x2, gpre, w_in, cos2, sin2, seq_len):
    n_tok = x2.shape[0]
    tiles_per_seq = seq_len // TILE
    return pl.pallas_call(
        _proj_kernel,
        grid=(n_tok // TILE, N_PROJ_SLABS),
        in_specs=[
            pl.BlockSpec((TILE, D_MODEL), lambda i, j: (i, 0)),
            pl.BlockSpec((1, D_MODEL), lambda i, j: (0, 0)),
            pl.BlockSpec((D_MODEL, ATTN_WIDTH), lambda i, j: (0, j)),
            pl.BlockSpec((TILE, LANES), lambda i, j: (i % tiles_per_seq, 0)),
            pl.BlockSpec((TILE, LANES), lambda i, j: (i % tiles_per_seq, 0)),
        ],
        out_specs=pl.BlockSpec((TILE, ATTN_WIDTH), lambda i, j: (i, j)),
        out_shape=jax.ShapeDtypeStruct((n_tok, N_PROJ_SLABS * ATTN_WIDTH), BF16),
        scratch_shapes=[
            pltpu.VMEM((D_MODEL // LANES, TILE, LANES), F32),
            pltpu.VMEM((D_MODEL // LANES, TILE, LANES), F32),
            pltpu.VMEM((TILE, D_MODEL), BF16),
        ] + 4 * [pltpu.VMEM((TILE, LANES), F32)],
        compiler_params=pltpu.CompilerParams(
            dimension_semantics=("arbitrary", "arbitrary"), vmem_limit_bytes=VMEM_LIMIT),
        name="attn_proj",
    )(x2, gpre.reshape(1, -1), w_in, cos2, sin2)


def _attn_masks():
    t = ATTN_BLOCK
    kj = np.arange(2 * t)[:, None]
    qc = np.arange(t)[None, :]
    cur = (kj >= t) & (kj - t <= qc)
    prev = (kj < t) & (kj >= qc)
    valid = np.stack([cur, cur | prev])
    return jnp.asarray(np.where(valid, 0.0, NEG_BIG), dtype=BF16)


def _attn_ring_kernel(q_ref, kv_hbm, mask_ref, eye_ref, o_ref, st_ref, k_ring, v_ring, sem, *,
                      chain, first_slab, n_steps):
    n = pl.program_id(1)
    step = pl.program_id(0) * pl.num_programs(1) + n

    def copies(tile_idx, slot):
        rows = pl.ds(pl.multiple_of(tile_idx * TILE, TILE), TILE)
        return [pltpu.make_async_copy(
            kv_hbm.at[rows, pl.ds((first_slab + 1 + c) * ATTN_WIDTH, ATTN_WIDTH)], ring.at[slot], sem.at[c, slot])
            for c, ring in enumerate((k_ring, v_ring))]

    @pl.when(step == 0)
    def _():
        for c, cp in enumerate(copies(0, 0)):
            cp.start(priority=c)

    @pl.when(step + 1 < n_steps)
    def _():
        for c, cp in enumerate(copies(step + 1, lax.rem(step + 1, KV_SLOTS))):
            cp.start(priority=c)

    cur = lax.rem(step, KV_SLOTS)
    for cp in copies(step, cur):
        cp.wait()
    prev = jnp.where(n == 0, cur, lax.rem(step + KV_SLOTS - 1, KV_SLOTS))
    _attn_kernel(q_ref, k_ring.at[prev], k_ring.at[cur], v_ring.at[prev], v_ring.at[cur], mask_ref, eye_ref,
                 o_ref, st_ref, chain=chain, first_tile=n == 0)


def _attn_kernel(q_ref, kp_ref, kc_ref, vp_ref, vc_ref, mask_ref, eye_ref, o_ref, st_ref, *, chain, first_tile):
    t = ATTN_BLOCK
    first_step_mask = mask_ref[jnp.where(first_tile, 0, 1)]
    eye = eye_ref[...]
    ones = jnp.ones((2 * t, LANES), BF16)
    lane = lax.broadcasted_iota(jnp.int32, (1, LANES), 1)
    head_a_qk = ((lane // (HEAD_DIM // 2)) % 2) == 0
    head_a_v = (lane // HEAD_DIM) == 0
    for jb in range(q_ref.shape[0] // t):
        rows = slice(jb * t, (jb + 1) * t)
        if jb % chain > 0:
            prev_k, prev_v, prev_rows, mask_t = kc_ref, vc_ref, slice((jb - 1) * t, jb * t), mask_ref[1]
        else:
            prev_k, prev_v, mask_t = kp_ref, vp_ref, first_step_mask
            start = 0 if kp_ref.shape[0] == t else (jb + chain - 1) * t
            prev_rows = slice(start, start + t)
        st = jnp.zeros((t, LANES), F32)
        for hp in range(N_HEADS // 2):
            sl = slice(hp * LANES, (hp + 1) * LANES)
            q2 = q_ref[rows, sl]
            zero = jnp.zeros_like(q2)
            lhs = jnp.concatenate([
                jnp.concatenate([jnp.where(head_a_qk, q2, zero), eye], axis=1),
                jnp.concatenate([jnp.where(head_a_qk, zero, q2), eye], axis=1)], axis=0)
            k_ext = jnp.concatenate(
                [jnp.concatenate([prev_k[prev_rows, sl], kc_ref[rows, sl]], axis=0), mask_t], axis=1)
            v_ext = jnp.concatenate(
                [jnp.concatenate([prev_v[prev_rows, sl], vc_ref[rows, sl]], axis=0), ones], axis=1)
            s = lax.dot_general(lhs, k_ext, (((1,), (1,)), ((), ())), preferred_element_type=F32)
            m = jnp.max(s, axis=-1, keepdims=True)
            p = jnp.exp2(s - m).astype(BF16)
            pv = jnp.dot(p, v_ext, preferred_element_type=F32)
            o_ref[rows, sl] = jnp.where(head_a_v, pv[:t, :LANES], pv[t:, :LANES]).astype(BF16)
            denom = pv[:, LANES:]
            st = jnp.where(lane == 2 * hp, m[:t], st)
            st = jnp.where(lane == 2 * hp + 1, m[t:], st)
            st = jnp.where(lane == N_HEADS + 2 * hp, denom[:t], st)
            st = jnp.where(lane == N_HEADS + 2 * hp + 1, denom[t:], st)
        st_ref[rows, :] = st


def _attn_group(proj2, g, dilation, batch, seq_len):
    n_tok = batch * seq_len
    t = ATTN_BLOCK
    tiles_per_seq = seq_len // TILE
    chain = TILE // t // dilation
    first = GROUP_FIRST_SLAB[g]
    tile = lambda b, n: b * tiles_per_seq + n
    return pl.pallas_call(
        functools.partial(_attn_ring_kernel, chain=chain, first_slab=first, n_steps=batch * tiles_per_seq),
        grid=(batch, tiles_per_seq),
        in_specs=[pl.BlockSpec((TILE, ATTN_WIDTH), lambda b, n: (tile(b, n), first)),
                  pl.BlockSpec(memory_space=pl.ANY),
                  pl.BlockSpec((2, 2 * t, t), lambda b, n: (0, 0, 0)),
                  pl.BlockSpec((t, t), lambda b, n: (0, 0))],
        out_specs=[
            pl.BlockSpec((TILE, ATTN_WIDTH), lambda b, n: (tile(b, n), 0)),
            pl.BlockSpec((TILE, LANES), lambda b, n: (tile(b, n), 0)),
        ],
        out_shape=[
            jax.ShapeDtypeStruct((n_tok, ATTN_WIDTH), BF16),
            jax.ShapeDtypeStruct((n_tok, LANES), F32),
        ],
        scratch_shapes=[
            pltpu.VMEM((KV_SLOTS, TILE, ATTN_WIDTH), BF16),
            pltpu.VMEM((KV_SLOTS, TILE, ATTN_WIDTH), BF16),
            pltpu.SemaphoreType.DMA((2, KV_SLOTS)),
        ],
        compiler_params=pltpu.CompilerParams(
            dimension_semantics=("arbitrary", "arbitrary"), vmem_limit_bytes=VMEM_LIMIT),
        name=f"dilated_attn_g{g}",
    )(proj2, proj2, _attn_masks(), jnp.eye(t, dtype=BF16))


def _combine_kernel(x_ref, o0_ref, o1_ref, o2_ref, s0_ref, s1_ref, s2_ref, z_ref,
                    expand_ref, wout_ref, gpost_ref, out_ref, on1_sc, on2_sc, tmp_sc, st1_sc, st2_sc,
                    tmp_st_sc, y_sc):
    n_slabs = ATTN_WIDTH // LANES
    is_head_lane = lax.broadcasted_iota(jnp.int32, (1, LANES), 1) < N_HEADS
    for sub in range(COMBINE_TOKENS // COMBINE_SUB):
        base = sub * COMBINE_SUB
        rows = slice(base, base + COMBINE_SUB)
        d1, d2 = DILATED_GROUPS[1][1], DILATED_GROUPS[2][1]
        n = COMBINE_SUB // d1
        src = slice(sub * n, (sub + 1) * n)
        for b in range(d1):
            dst = pl.ds(base + b, n, stride=d1)
            st1_sc[dst, :] = s1_ref[0, b, src, :]
            for c in range(n_slabs):
                on1_sc[c, dst, :] = o1_ref[0, b, src, c * LANES:(c + 1) * LANES].astype(F32)
        ratio = d2 // d1
        n2 = COMBINE_SUB // d2
        src2 = slice(sub * n2, (sub + 1) * n2)
        for b in range(d1):
            mid = slice(base + b * n, base + (b + 1) * n)
            for a in range(ratio):
                dst = pl.ds(base + b * n + a, n2, stride=ratio)
                tmp_st_sc[dst, :] = s2_ref[0, a * d1 + b, src2, :]
                for c in range(n_slabs):
                    tmp_sc[c, dst, :] = o2_ref[0, a * d1 + b, src2, c * LANES:(c + 1) * LANES].astype(F32)
            dst = pl.ds(base + b, n, stride=d1)
            st2_sc[dst, :] = tmp_st_sc[mid, :]
            for c in range(n_slabs):
                on2_sc[c, dst, :] = tmp_sc[c, mid, :]

        sts = [s0_ref[rows, :], st1_sc[rows, :], st2_sc[rows, :]]
        m = jnp.maximum(jnp.maximum(sts[0], sts[1]), sts[2])
        es = [jnp.exp2(s - m) for s in sts]
        total = None
        for e, s in zip(es, sts):
            term = e * pltpu.roll(s, LANES - N_HEADS, 1)
            total = term if total is None else total + term
        inv = 1.0 / jnp.where(is_head_lane, total, 1.0)
        alphas = []
        for e in es:
            alpha = jnp.where(is_head_lane, e * inv, 0.0)
            hi = alpha.astype(BF16)
            lo = (alpha - hi.astype(F32)).astype(BF16)
            alphas.append(jnp.dot(jnp.concatenate([hi, lo], axis=1), expand_ref[...],
                                  preferred_element_type=F32))
        for c in range(n_slabs):
            sl = slice(c * LANES, (c + 1) * LANES)
            o = (alphas[0][:, sl] * o0_ref[rows, sl].astype(F32)
                 + alphas[1][:, sl] * on1_sc[c, rows, :] + alphas[2][:, sl] * on2_sc[c, rows, :])
            y_sc[rows, sl] = (o * _silu(z_ref[rows, sl].astype(F32))).astype(BF16)
        out = jnp.dot(y_sc[rows, :], wout_ref[...], preferred_element_type=F32)
        out_ref[rows, :] = x_ref[rows, :] + _rms_normalize(out, gpost_ref[...])


def _combine_layer(x2, outs, stats, proj2, w_out, gpost):
    n_tok = x2.shape[0]
    tm = COMBINE_TOKENS
    n_tiles = n_tok // TILE
    per_tile = TILE // tm
    head_of_lane = np.arange(ATTN_WIDTH) // HEAD_DIM
    expand = jnp.asarray(np.tile(np.arange(LANES)[:, None] == head_of_lane[None, :], (2, 1)), dtype=BF16)
    tok = lambda i: (i, 0)
    const = lambda i: (0, 0)
    in_tile = lambda i: (i // per_tile, 0, i % per_tile, 0)
    d1, d2 = DILATED_GROUPS[1][1], DILATED_GROUPS[2][1]
    return pl.pallas_call(
        _combine_kernel,
        grid=(n_tok // tm,),
        in_specs=[
            pl.BlockSpec((tm, D_MODEL), tok),
            pl.BlockSpec((tm, ATTN_WIDTH), tok),
            pl.BlockSpec((1, d1, tm // d1, ATTN_WIDTH), in_tile),
            pl.BlockSpec((1, d2, tm // d2, ATTN_WIDTH), in_tile),
            pl.BlockSpec((tm, LANES), tok),
            pl.BlockSpec((1, d1, tm // d1, LANES), in_tile),
            pl.BlockSpec((1, d2, tm // d2, LANES), in_tile),
            pl.BlockSpec((tm, ATTN_WIDTH), lambda i: (i, GATE_OUT_SLAB)),
            pl.BlockSpec((2 * LANES, ATTN_WIDTH), const),
            pl.BlockSpec((ATTN_WIDTH, D_MODEL), const),
            pl.BlockSpec((1, D_MODEL), const),
        ],
        out_specs=pl.BlockSpec((tm, D_MODEL), tok),
        out_shape=jax.ShapeDtypeStruct(x2.shape, F32),
        scratch_shapes=[
            pltpu.VMEM((ATTN_WIDTH // LANES, tm, LANES), F32),
            pltpu.VMEM((ATTN_WIDTH // LANES, tm, LANES), F32),
            pltpu.VMEM((ATTN_WIDTH // LANES, tm, LANES), F32),
            pltpu.VMEM((tm, LANES), F32),
            pltpu.VMEM((tm, LANES), F32),
            pltpu.VMEM((tm, LANES), F32),
            pltpu.VMEM((tm, ATTN_WIDTH), BF16),
        ],
        compiler_params=pltpu.CompilerParams(
            dimension_semantics=("arbitrary",), vmem_limit_bytes=VMEM_LIMIT),
        name="attn_combine",
    )(x2,
      outs[0],
      outs[1].reshape(n_tiles, d1, TILE // d1, ATTN_WIDTH),
      outs[2].reshape(n_tiles, d2, TILE // d2, ATTN_WIDTH),
      stats[0],
      stats[1].reshape(n_tiles, d1, TILE // d1, LANES),
      stats[2].reshape(n_tiles, d2, TILE // d2, LANES),
      proj2, expand, w_out, gpost.reshape(1, -1))


def _rope_tables(seq_len):
    half = HEAD_DIM // 2
    exponent = np.arange(0, HEAD_DIM, 2, dtype=np.float32) / np.float32(HEAD_DIM)
    power = np.power(np.float64(ROPE_THETA), exponent.astype(np.float64)).astype(np.float32)
    inv_freq = np.float32(1.0) / power
    ang = np.arange(seq_len, dtype=np.float32)[:, None] * inv_freq[None, :]
    cos = np.cos(ang.astype(np.float64)).astype(np.float32)
    sin = np.sin(ang.astype(np.float64)).astype(np.float32)
    cos2 = np.tile(cos, (1, LANES // half))
    sin2 = np.concatenate([-sin, -sin, sin, sin], axis=1)
    return jnp.asarray(cos2), jnp.asarray(sin2)


def kernel(x, norm_pre, norm_post, a_w_in, a_ln_g, a_ln_b, a_w_s, a_b_s, a_w_out, b_w_in, b_w_out):
    batch, seq_len, d = x.shape
    depth = norm_pre.shape[0]
    cos2, sin2 = _rope_tables(seq_len)
    x2 = x.reshape(batch * seq_len, d)
    for i in range(depth):
        j = i // 2
        if i % 2 == 0:
            x2 = _gmlp_layer(x2, norm_pre[i], norm_post[i], _bf16_weight(a_w_in, j), a_ln_g[j], a_ln_b[j],
                             a_w_s[j], a_b_s[j], _bf16_weight(a_w_out, j))
        else:
            proj2 = _proj_layer(x2, norm_pre[i], _proj_weight(b_w_in, j), cos2, sin2, seq_len)
            outs, stats = [], []
            for g, (_, dilation) in enumerate(DILATED_GROUPS):
                o, st = _attn_group(proj2, g, dilation, batch, seq_len)
                outs.append(o)
                stats.append(st)
            x2 = _combine_layer(x2, outs, stats, proj2, _bf16_weight(b_w_out, j), norm_post[i])
    return x2.reshape(batch, seq_len, d)
```

```python
import functools

import numpy as np
import jax
import jax.numpy as jnp
from jax import lax
from jax.experimental import pallas as pl
from jax.experimental.pallas import tpu as pltpu

F32 = jnp.float32
BF16 = jnp.bfloat16

D_MODEL = 1024
RMS_EPS = 1e-6
LN_EPS = 1e-5

GMLP_WIDTH = 2048
GMLP_CHUNK = 128
GMLP_GROUPS = 8
GMLP_GROUP_WIDTH = GMLP_WIDTH // GMLP_GROUPS
GMLP_DOT_GROUPS = 4

HEAD_DIM = 64
N_HEADS = 16
ATTN_WIDTH = N_HEADS * HEAD_DIM
DILATED_GROUPS = ((128, 1), (512, 4), (2048, 16))
N_DIL_GROUPS = len(DILATED_GROUPS)
ATTN_BLOCK = 128
KV_SLOTS = 3
ROPE_THETA = 10000.0
N_PROJ_SLABS = 3 * N_DIL_GROUPS + 1
Z_SLAB = 3 * N_DIL_GROUPS
GROUP_FIRST_SLAB = (0, 4, 7)
GATE_OUT_SLAB = 3
TILE = 2048

LANES = 128
NEG_BIG = -1e30

GMLP_TOKENS = 512
PROJ_ROW_CHUNK = 512
COMBINE_TOKENS = 1024
COMBINE_SUB = 512
VMEM_LIMIT = 60 * 1024 * 1024
WEIGHT_CAST_BLOCK_BYTES = 4 * 1024 * 1024


def _gelu(x):
    return 0.5 * x * (1.0 + lax.erf(x * np.float32(np.sqrt(0.5))))


def _silu(x):
    return x * jax.nn.sigmoid(x)


def _rms_normalize(x, gain):
    ms = jnp.mean(x * x, axis=-1, keepdims=True)
    return x * lax.rsqrt(ms + RMS_EPS) * gain


def _gmlp_kernel(x_ref, gpre_ref, gpost_ref, win_ref, lng_ref, lnb_ref, ws_ref,
                 bs_ref, wout_ref, o_ref, h_sc, v_sc, t_sc, y_sc):
    tm = x_ref.shape[0]
    gw = GMLP_GROUP_WIDTH
    x = x_ref[...]
    h_sc[...] = _rms_normalize(x, gpre_ref[...]).astype(BF16)

    pw = GMLP_DOT_GROUPS * gw
    s1 = jnp.zeros((tm, 1), F32)
    for gp in range(GMLP_WIDTH // pw):
        c0 = GMLP_WIDTH + gp * pw
        v = _gelu(jnp.dot(h_sc[...], win_ref[:, c0:c0 + pw], preferred_element_type=F32))
        v_sc[:, gp * pw:(gp + 1) * pw] = v
        s1 = s1 + jnp.sum(v, axis=-1, keepdims=True)
    for gp in range(GMLP_WIDTH // pw):
        psl = slice(gp * pw, (gp + 1) * pw)
        u = _gelu(jnp.dot(h_sc[...], win_ref[:, psl], preferred_element_type=F32))
        z0 = 2 * GMLP_WIDTH + gp * pw
        z = jnp.dot(h_sc[...], win_ref[:, z0:z0 + pw], preferred_element_type=F32)
        t_sc[:, psl] = u * _silu(z)

    mu = s1 * (1.0 / GMLP_WIDTH)
    s2 = jnp.zeros((tm, 1), F32)
    for g in range(GMLP_GROUPS):
        dv = v_sc[:, g * gw:(g + 1) * gw] - mu
        s2 = s2 + jnp.sum(dv * dv, axis=-1, keepdims=True)
    rstd = lax.rsqrt(s2 * (1.0 / GMLP_WIDTH) + LN_EPS)

    row = lax.broadcasted_iota(jnp.int32, (GMLP_CHUNK, GMLP_CHUNK), 0)
    col = lax.broadcasted_iota(jnp.int32, (GMLP_CHUNK, GMLP_CHUNK), 1)
    causal = col <= row
    for g in range(GMLP_GROUPS):
        sl = slice(g * gw, (g + 1) * gw)
        vn = ((v_sc[:, sl] - mu) * rstd * lng_ref[:, sl] + lnb_ref[:, sl]).astype(BF16)
        wm = jnp.where(causal, ws_ref[g], 0.0).astype(BF16)
        bias = bs_ref[:, g:g + 1]
        sv = jnp.concatenate(
            [jnp.dot(wm, vn[c * GMLP_CHUNK:(c + 1) * GMLP_CHUNK], preferred_element_type=F32) + bias
             for c in range(tm // GMLP_CHUNK)], axis=0)
        y_sc[:, sl] = (t_sc[:, sl] * sv).astype(BF16)

    out = jnp.dot(y_sc[...], wout_ref[...], preferred_element_type=F32)
    o_ref[...] = x_ref[...] + _rms_normalize(out, gpost_ref[...])


def _gmlp_layer(x2, gpre, gpost, w_in, ln_g, ln_b, w_s, b_s, w_out):
    n_tok = x2.shape[0]
    tm = GMLP_TOKENS
    const = lambda i: (0, 0)
    return pl.pallas_call(
        _gmlp_kernel,
        grid=(n_tok // tm,),
        in_specs=[
            pl.BlockSpec((tm, D_MODEL), lambda i: (i, 0)),
            pl.BlockSpec((1, D_MODEL), const),
            pl.BlockSpec((1, D_MODEL), const),
            pl.BlockSpec((D_MODEL, 3 * GMLP_WIDTH), const, pipeline_mode=pl.Buffered(1)),
            pl.BlockSpec((1, GMLP_WIDTH), const),
            pl.BlockSpec((1, GMLP_WIDTH), const),
            pl.BlockSpec((GMLP_GROUPS, GMLP_CHUNK, GMLP_CHUNK), lambda i: (0, 0, 0)),
            pl.BlockSpec((GMLP_CHUNK, GMLP_GROUPS), const),
            pl.BlockSpec((GMLP_WIDTH, D_MODEL), const, pipeline_mode=pl.Buffered(1)),
        ],
        out_specs=pl.BlockSpec((tm, D_MODEL), lambda i: (i, 0)),
        out_shape=jax.ShapeDtypeStruct(x2.shape, F32),
        scratch_shapes=[
            pltpu.VMEM((tm, D_MODEL), BF16),
            pltpu.VMEM((tm, GMLP_WIDTH), F32),
            pltpu.VMEM((tm, GMLP_WIDTH), F32),
            pltpu.VMEM((tm, GMLP_WIDTH), BF16),
        ],
        compiler_params=pltpu.CompilerParams(
            dimension_semantics=("arbitrary",), vmem_limit_bytes=VMEM_LIMIT),
        name="gmlp_layer",
    )(x2, gpre.reshape(1, -1), gpost.reshape(1, -1), w_in,
      ln_g.reshape(1, -1), ln_b.reshape(1, -1), w_s, jnp.transpose(b_s), w_out)


def _proj_kernel(x_ref, gpre_ref, w_ref, cos_ref, sin_ref, o_ref,
                 hs_sc, hs1_sc, h_sc, cos1_sc, sin1_sc, cos2_sc, sin2_sc):
    j = pl.program_id(1)
    n_chunks = TILE // PROJ_ROW_CHUNK
    n_slabs = D_MODEL // LANES
    first1, first2 = GROUP_FIRST_SLAB[1], GROUP_FIRST_SLAB[2]
    d1, d2 = DILATED_GROUPS[1][1], DILATED_GROUPS[2][1]
    ratio = d2 // d1

    @pl.when(j == 0)
    def _():
        for rc in range(n_chunks):
            rows = slice(rc * PROJ_ROW_CHUNK, (rc + 1) * PROJ_ROW_CHUNK)
            hs = _rms_normalize(x_ref[rows, :], gpre_ref[...])
            for c in range(n_slabs):
                hs_sc[c, rows, :] = hs[:, c * LANES:(c + 1) * LANES]

    @pl.when(j == first1)
    def _():
        rows = TILE // d1
        for b in range(d1):
            out, src = slice(b * rows, (b + 1) * rows), pl.ds(b, rows, stride=d1)
            for c in range(n_slabs):
                v = hs_sc[c, src, :]
                hs1_sc[c, out, :] = v
                h_sc[out, c * LANES:(c + 1) * LANES] = v.astype(BF16)
            cos1_sc[out, :] = cos_ref[src, :]
            sin1_sc[out, :] = sin_ref[src, :]

    @pl.when(j == first2)
    def _():
        rows = TILE // d2
        for b in range(d1):
            for a in range(ratio):
                r = a * d1 + b
                out, src = slice(r * rows, (r + 1) * rows), pl.ds(b * (TILE // d1) + a, rows, stride=ratio)
                for c in range(n_slabs):
                    h_sc[out, c * LANES:(c + 1) * LANES] = hs1_sc[c, src, :].astype(BF16)
                cos2_sc[out, :] = cos1_sc[src, :]
                sin2_sc[out, :] = sin1_sc[src, :]

    is_q = functools.reduce(jnp.logical_or, [j == s for s in GROUP_FIRST_SLAB])
    is_k = functools.reduce(jnp.logical_or, [j == s + 1 for s in GROUP_FIRST_SLAB])
    is_rope = jnp.logical_or(is_q, is_k)
    in_group0 = j < first1
    in_group1 = jnp.logical_and(j >= first1, j < first2)
    in_group2 = j >= first2

    def lhs_rows(from_token_order, rows):
        if from_token_order:
            return jnp.concatenate([hs_sc[c, rows, :] for c in range(n_slabs)], axis=1).astype(BF16)
        return h_sc[rows, :]

    def rope_slab(from_token_order, cos_tab, sin_tab):
        scale = jnp.where(is_q, np.float32(np.log2(np.e) / np.sqrt(HEAD_DIM)), np.float32(1.0))
        bounds = [0, 512, 1024, 1536, 1792, 1920, 2048]
        for lo, hi in zip(bounds[:-1], bounds[1:]):
            rows = slice(lo, hi)
            p = jnp.dot(lhs_rows(from_token_order, rows), w_ref[...], preferred_element_type=F32)
            cos = cos_tab[rows, :] * scale
            sin = sin_tab[rows, :] * scale
            for s in range(ATTN_WIDTH // LANES):
                t = p[:, s * LANES:(s + 1) * LANES]
                o_ref[rows, s * LANES:(s + 1) * LANES] = (
                    t * cos + pltpu.roll(t, LANES // 2, 1) * sin).astype(BF16)

    def plain_slab(from_token_order):
        for rc in range(n_chunks):
            rows = slice(rc * PROJ_ROW_CHUNK, (rc + 1) * PROJ_ROW_CHUNK)
            o_ref[rows, :] = jnp.dot(lhs_rows(from_token_order, rows), w_ref[...],
                                     preferred_element_type=F32).astype(BF16)

    pl.when(jnp.logical_and(is_rope, in_group0))(functools.partial(rope_slab, True, cos_ref, sin_ref))
    pl.when(jnp.logical_and(is_rope, in_group1))(functools.partial(rope_slab, False, cos1_sc, sin1_sc))
    pl.when(jnp.logical_and(is_rope, in_group2))(functools.partial(rope_slab, False, cos2_sc, sin2_sc))
    is_plain = jnp.logical_not(is_rope)
    pl.when(jnp.logical_and(is_plain, in_group0))(functools.partial(plain_slab, True))
    pl.when(jnp.logical_and(is_plain, jnp.logical_not(in_group0)))(functools.partial(plain_slab, False))


def _proj_weight_kernel(w_ref, o_ref):
    s = pl.program_id(0)
    is_qk = functools.reduce(jnp.logical_or, [jnp.logical_or(s == f, s == f + 1) for f in GROUP_FIRST_SLAB])

    @pl.when(is_qk)
    def _():
        half = HEAD_DIM // 2
        quarter = lax.broadcasted_iota(jnp.int32, (1, LANES), 1) // half
        for c in range(ATTN_WIDTH // LANES):
            sl = slice(c * LANES, (c + 1) * LANES)
            w = w_ref[0, :, sl]
            w = jnp.where(quarter == 1, pltpu.roll(w, LANES - half, 1),
                          jnp.where(quarter == 2, pltpu.roll(w, half, 1), w))
            o_ref[:, sl] = w.astype(BF16)

    @pl.when(jnp.logical_not(is_qk))
    def _():
        o_ref[...] = w_ref[0].astype(BF16)


def _proj_weight(w_in_all, layer):
    def source_slab(s):
        return jnp.where(s == GATE_OUT_SLAB, Z_SLAB, jnp.where(s < GATE_OUT_SLAB, s, s - 1))

    return pl.pallas_call(
        _proj_weight_kernel,
        grid=(N_PROJ_SLABS,),
        in_specs=[pl.BlockSpec((1, D_MODEL, ATTN_WIDTH), lambda s: (layer, 0, source_slab(s)))],
        out_specs=pl.BlockSpec((D_MODEL, ATTN_WIDTH), lambda s: (0, s)),
        out_shape=jax.ShapeDtypeStruct((D_MODEL, N_PROJ_SLABS * ATTN_WIDTH), BF16),
        compiler_params=pltpu.CompilerParams(
            dimension_semantics=("arbitrary",), vmem_limit_bytes=VMEM_LIMIT),
        name="proj_weight_prep",
    )(w_in_all)


def _cast_kernel(w_ref, o_ref):
    o_ref[...] = w_ref[0].astype(BF16)


def _bf16_weight(w_all, layer):
    _, rows, cols = w_all.shape
    block_rows = rows
    while block_rows * cols * 4 > WEIGHT_CAST_BLOCK_BYTES:
        block_rows //= 2
    assert rows % block_rows == 0 and block_rows % 16 == 0
    return pl.pallas_call(
        _cast_kernel,
        grid=(rows // block_rows,),
        in_specs=[pl.BlockSpec((1, block_rows, cols), lambda i: (layer, i, 0))],
        out_specs=pl.BlockSpec((block_rows, cols), lambda i: (i, 0)),
        out_shape=jax.ShapeDtypeStruct((rows, cols), BF16),
        compiler_params=pltpu.CompilerParams(
            dimension_semantics=("arbitrary",), vmem_limit_bytes=VMEM_LIMIT),
        name="weight_cast",
    )(w_all)


def _proj_layer(x2, gpre, w_in, cos2, sin2, seq_len):
    n_tok = x2.shape[0]
    tiles_per_seq = seq_len // TILE
    return pl.pallas_call(
        _proj_kernel,
        grid=(n_tok // TILE, N_PROJ_SLABS),
        in_specs=[
            pl.BlockSpec((TILE, D_MODEL), lambda i, j: (i, 0)),
            pl.BlockSpec((1, D_MODEL), lambda i, j: (0, 0)),
            pl.BlockSpec((D_MODEL, ATTN_WIDTH), lambda i, j: (0, j)),
            pl.BlockSpec((TILE, LANES), lambda i, j: (i % tiles_per_seq, 0)),
            pl.BlockSpec((TILE, LANES), lambda i, j: (i % tiles_per_seq, 0)),
        ],
        out_specs=pl.BlockSpec((TILE, ATTN_WIDTH), lambda i, j: (i, j)),
        out_shape=jax.ShapeDtypeStruct((n_tok, N_PROJ_SLABS * ATTN_WIDTH), BF16),
        scratch_shapes=[
            pltpu.VMEM((D_MODEL // LANES, TILE, LANES), F32),
            pltpu.VMEM((D_MODEL // LANES, TILE, LANES), F32),
            pltpu.VMEM((TILE, D_MODEL), BF16),
        ] + 4 * [pltpu.VMEM((TILE, LANES), F32)],
        compiler_params=pltpu.CompilerParams(
            dimension_semantics=("arbitrary", "arbitrary"), vmem_limit_bytes=VMEM_LIMIT),
        name="attn_proj",
    )(x2, gpre.reshape(1, -1), w_in, cos2, sin2)


def _attn_masks():
    t = ATTN_BLOCK
    kj = np.arange(2 * t)[:, None]
    qc = np.arange(t)[None, :]
    cur = (kj >= t) & (kj - t <= qc)
    prev = (kj < t) & (kj >= qc)
    valid = np.stack([cur, cur | prev])
    return jnp.asarray(np.where(valid, 0.0, NEG_BIG), dtype=BF16)


def _attn_ring_kernel(q_ref, kv_hbm, mask_ref, eye_ref, o_ref, st_ref, k_ring, v_ring, sem, *,
                      chain, first_slab, n_steps):
    n = pl.program_id(1)
    step = pl.program_id(0) * pl.num_programs(1) + n

    def copies(tile_idx, slot):
        rows = pl.ds(pl.multiple_of(tile_idx * TILE, TILE), TILE)
        return [pltpu.make_async_copy(
            kv_hbm.at[rows, pl.ds((first_slab + 1 + c) * ATTN_WIDTH, ATTN_WIDTH)], ring.at[slot], sem.at[c, slot])
            for c, ring in enumerate((k_ring, v_ring))]

    @pl.when(step == 0)
    def _():
        for cp in copies(0, 0):
            cp.start()

    @pl.when(step + 1 < n_steps)
    def _():
        for cp in copies(step + 1, lax.rem(step + 1, KV_SLOTS)):
            cp.start()

    cur = lax.rem(step, KV_SLOTS)
    for cp in copies(step, cur):
        cp.wait()
    prev = jnp.where(n == 0, cur, lax.rem(step + KV_SLOTS - 1, KV_SLOTS))
    _attn_kernel(q_ref, k_ring.at[prev], k_ring.at[cur], v_ring.at[prev], v_ring.at[cur], mask_ref, eye_ref,
                 o_ref, st_ref, chain=chain, first_tile=n == 0)


def _attn_kernel(q_ref, kp_ref, kc_ref, vp_ref, vc_ref, mask_ref, eye_ref, o_ref, st_ref, *, chain, first_tile):
    t = ATTN_BLOCK
    first_step_mask = mask_ref[jnp.where(first_tile, 0, 1)]
    eye = eye_ref[...]
    ones = jnp.ones((2 * t, LANES), BF16)
    lane = lax.broadcasted_iota(jnp.int32, (1, LANES), 1)
    head_a_qk = ((lane // (HEAD_DIM // 2)) % 2) == 0
    head_a_v = (lane // HEAD_DIM) == 0
    for jb in range(q_ref.shape[0] // t):
        rows = slice(jb * t, (jb + 1) * t)
        if jb % chain > 0:
            prev_k, prev_v, prev_rows, mask_t = kc_ref, vc_ref, slice((jb - 1) * t, jb * t), mask_ref[1]
        else:
            prev_k, prev_v, mask_t = kp_ref, vp_ref, first_step_mask
            start = 0 if kp_ref.shape[0] == t else (jb + chain - 1) * t
            prev_rows = slice(start, start + t)
        st = jnp.zeros((t, LANES), F32)
        for hp in range(N_HEADS // 2):
            sl = slice(hp * LANES, (hp + 1) * LANES)
            q2 = q_ref[rows, sl]
            zero = jnp.zeros_like(q2)
            lhs = jnp.concatenate([
                jnp.concatenate([jnp.where(head_a_qk, q2, zero), eye], axis=1),
                jnp.concatenate([jnp.where(head_a_qk, zero, q2), eye], axis=1)], axis=0)
            k_ext = jnp.concatenate(
                [jnp.concatenate([prev_k[prev_rows, sl], kc_ref[rows, sl]], axis=0), mask_t], axis=1)
            v_ext = jnp.concatenate(
                [jnp.concatenate([prev_v[prev_rows, sl], vc_ref[rows, sl]], axis=0), ones], axis=1)
            s = lax.dot_general(lhs, k_ext, (((1,), (1,)), ((), ())), preferred_element_type=F32)
            m = jnp.max(s, axis=-1, keepdims=True)
            p = jnp.exp2(s - m).astype(BF16)
            pv = jnp.dot(p, v_ext, preferred_element_type=F32)
            o_ref[rows, sl] = jnp.where(head_a_v, pv[:t, :LANES], pv[t:, :LANES]).astype(BF16)
            denom = pv[:, LANES:]
            st = jnp.where(lane == 2 * hp, m[:t], st)
            st = jnp.where(lane == 2 * hp + 1, m[t:], st)
            st = jnp.where(lane == N_HEADS + 2 * hp, denom[:t], st)
            st = jnp.where(lane == N_HEADS + 2 * hp + 1, denom[t:], st)
        st_ref[rows, :] = st


def _attn_group(proj2, g, dilation, batch, seq_len):
    n_tok = batch * seq_len
    t = ATTN_BLOCK
    tiles_per_seq = seq_len // TILE
    chain = TILE // t // dilation
    first = GROUP_FIRST_SLAB[g]
    tile = lambda b, n: b * tiles_per_seq + n
    return pl.pallas_call(
        functools.partial(_attn_ring_kernel, chain=chain, first_slab=first, n_steps=batch * tiles_per_seq),
        grid=(batch, tiles_per_seq),
        in_specs=[pl.BlockSpec((TILE, ATTN_WIDTH), lambda b, n: (tile(b, n), first)),
                  pl.BlockSpec(memory_space=pl.ANY),
                  pl.BlockSpec((2, 2 * t, t), lambda b, n: (0, 0, 0)),
                  pl.BlockSpec((t, t), lambda b, n: (0, 0))],
        out_specs=[
            pl.BlockSpec((TILE, ATTN_WIDTH), lambda b, n: (tile(b, n), 0)),
            pl.BlockSpec((TILE, LANES), lambda b, n: (tile(b, n), 0)),
        ],
        out_shape=[
            jax.ShapeDtypeStruct((n_tok, ATTN_WIDTH), BF16),
            jax.ShapeDtypeStruct((n_tok, LANES), F32),
        ],
        scratch_shapes=[
            pltpu.VMEM((KV_SLOTS, TILE, ATTN_WIDTH), BF16),
            pltpu.VMEM((KV_SLOTS, TILE, ATTN_WIDTH), BF16),
            pltpu.SemaphoreType.DMA((2, KV_SLOTS)),
        ],
        compiler_params=pltpu.CompilerParams(
            dimension_semantics=("arbitrary", "arbitrary"), vmem_limit_bytes=VMEM_LIMIT),
        name=f"dilated_attn_g{g}",
    )(proj2, proj2, _attn_masks(), jnp.eye(t, dtype=BF16))


def _combine_kernel(x_ref, o0_ref, o1_ref, o2_ref, s0_ref, s1_ref, s2_ref, z_ref,
                    expand_ref, wout_ref, gpost_ref, out_ref, on1_sc, on2_sc, tmp_sc, st1_sc, st2_sc,
                    tmp_st_sc, y_sc):
    n_slabs = ATTN_WIDTH // LANES
    is_head_lane = lax.broadcasted_iota(jnp.int32, (1, LANES), 1) < N_HEADS
    for sub in range(COMBINE_TOKENS // COMBINE_SUB):
        base = sub * COMBINE_SUB
        rows = slice(base, base + COMBINE_SUB)
        d1, d2 = DILATED_GROUPS[1][1], DILATED_GROUPS[2][1]
        n = COMBINE_SUB // d1
        src = slice(sub * n, (sub + 1) * n)
        for b in range(d1):
            dst = pl.ds(base + b, n, stride=d1)
            st1_sc[dst, :] = s1_ref[0, b, src, :]
            for c in range(n_slabs):
                on1_sc[c, dst, :] = o1_ref[0, b, src, c * LANES:(c + 1) * LANES].astype(F32)
        ratio = d2 // d1
        n2 = COMBINE_SUB // d2
        src2 = slice(sub * n2, (sub + 1) * n2)
        for b in range(d1):
            mid = slice(base + b * n, base + (b + 1) * n)
            for a in range(ratio):
                dst = pl.ds(base + b * n + a, n2, stride=ratio)
                tmp_st_sc[dst, :] = s2_ref[0, a * d1 + b, src2, :]
                for c in range(n_slabs):
                    tmp_sc[c, dst, :] = o2_ref[0, a * d1 + b, src2, c * LANES:(c + 1) * LANES].astype(F32)
            dst = pl.ds(base + b, n, stride=d1)
            st2_sc[dst, :] = tmp_st_sc[mid, :]
            for c in range(n_slabs):
                on2_sc[c, dst, :] = tmp_sc[c, mid, :]

        sts = [s0_ref[rows, :], st1_sc[rows, :], st2_sc[rows, :]]
        m = jnp.maximum(jnp.maximum(sts[0], sts[1]), sts[2])
        es = [jnp.exp2(s - m) for s in sts]
        total = None
        for e, s in zip(es, sts):
            term = e * pltpu.roll(s, LANES - N_HEADS, 1)
            total = term if total is None else total + term
        inv = 1.0 / jnp.where(is_head_lane, total, 1.0)
        alphas = []
        for e in es:
            alpha = jnp.where(is_head_lane, e * inv, 0.0)
            hi = alpha.astype(BF16)
            lo = (alpha - hi.astype(F32)).astype(BF16)
            alphas.append(jnp.dot(jnp.concatenate([hi, lo], axis=1), expand_ref[...],
                                  preferred_element_type=F32))
        for c in range(n_slabs):
            sl = slice(c * LANES, (c + 1) * LANES)
            o = (alphas[0][:, sl] * o0_ref[rows, sl].astype(F32)
                 + alphas[1][:, sl] * on1_sc[c, rows, :] + alphas[2][:, sl] * on2_sc[c, rows, :])
            y_sc[rows, sl] = (o * _silu(z_ref[rows, sl].astype(F32))).astype(BF16)
        out = jnp.dot(y_sc[rows, :], wout_ref[...], preferred_element_type=F32)
        out_ref[rows, :] = x_ref[rows, :] + _rms_normalize(out, gpost_ref[...])


def _combine_layer(x2, outs, stats, proj2, w_out, gpost):
    n_tok = x2.shape[0]
    tm = COMBINE_TOKENS
    n_tiles = n_tok // TILE
    per_tile = TILE // tm
    head_of_lane = np.arange(ATTN_WIDTH) // HEAD_DIM
    expand = jnp.asarray(np.tile(np.arange(LANES)[:, None] == head_of_lane[None, :], (2, 1)), dtype=BF16)
    tok = lambda i: (i, 0)
    const = lambda i: (0, 0)
    in_tile = lambda i: (i // per_tile, 0, i % per_tile, 0)
    d1, d2 = DILATED_GROUPS[1][1], DILATED_GROUPS[2][1]
    return pl.pallas_call(
        _combine_kernel,
        grid=(n_tok // tm,),
        in_specs=[
            pl.BlockSpec((tm, D_MODEL), tok),
            pl.BlockSpec((tm, ATTN_WIDTH), tok),
            pl.BlockSpec((1, d1, tm // d1, ATTN_WIDTH), in_tile),
            pl.BlockSpec((1, d2, tm // d2, ATTN_WIDTH), in_tile),
            pl.BlockSpec((tm, LANES), tok),
            pl.BlockSpec((1, d1, tm // d1, LANES), in_tile),
            pl.BlockSpec((1, d2, tm // d2, LANES), in_tile),
            pl.BlockSpec((tm, ATTN_WIDTH), lambda i: (i, GATE_OUT_SLAB)),
            pl.BlockSpec((2 * LANES, ATTN_WIDTH), const),
            pl.BlockSpec((ATTN_WIDTH, D_MODEL), const),
            pl.BlockSpec((1, D_MODEL), const),
        ],
        out_specs=pl.BlockSpec((tm, D_MODEL), tok),
        out_shape=jax.ShapeDtypeStruct(x2.shape, F32),
        scratch_shapes=[
            pltpu.VMEM((ATTN_WIDTH // LANES, tm, LANES), F32),
            pltpu.VMEM((ATTN_WIDTH // LANES, tm, LANES), F32),
            pltpu.VMEM((ATTN_WIDTH // LANES, tm, LANES), F32),
            pltpu.VMEM((tm, LANES), F32),
            pltpu.VMEM((tm, LANES), F32),
            pltpu.VMEM((tm, LANES), F32),
            pltpu.VMEM((tm, ATTN_WIDTH), BF16),
        ],
        compiler_params=pltpu.CompilerParams(
            dimension_semantics=("arbitrary",), vmem_limit_bytes=VMEM_LIMIT),
        name="attn_combine",
    )(x2,
      outs[0],
      outs[1].reshape(n_tiles, d1, TILE // d1, ATTN_WIDTH),
      outs[2].reshape(n_tiles, d2, TILE // d2, ATTN_WIDTH),
      stats[0],
      stats[1].reshape(n_tiles, d1, TILE // d1, LANES),
      stats[2].reshape(n_tiles, d2, TILE // d2, LANES),
      proj2, expand, w_out, gpost.reshape(1, -1))


def _rope_tables(seq_len):
    half = HEAD_DIM // 2
    exponent = np.arange(0, HEAD_DIM, 2, dtype=np.float32) / np.float32(HEAD_DIM)
    power = np.power(np.float64(ROPE_THETA), exponent.astype(np.float64)).astype(np.float32)
    inv_freq = np.float32(1.0) / power
    ang = np.arange(seq_len, dtype=np.float32)[:, None] * inv_freq[None, :]
    cos = np.cos(ang.astype(np.float64)).astype(np.float32)
    sin = np.sin(ang.astype(np.float64)).astype(np.float32)
    cos2 = np.tile(cos, (1, LANES // half))
    sin2 = np.concatenate([-sin, -sin, sin, sin], axis=1)
    return jnp.asarray(cos2), jnp.asarray(sin2)


def kernel(x, norm_pre, norm_post, a_w_in, a_ln_g, a_ln_b, a_w_s, a_b_s, a_w_out, b_w_in, b_w_out):
    batch, seq_len, d = x.shape
    depth = norm_pre.shape[0]
    cos2, sin2 = _rope_tables(seq_len)
    x2 = x.reshape(batch * seq_len, d)
    for i in range(depth):
        j = i // 2
        if i % 2 == 0:
            x2 = _gmlp_layer(x2, norm_pre[i], norm_post[i], _bf16_weight(a_w_in, j), a_ln_g[j], a_ln_b[j],
                             a_w_s[j], a_b_s[j], _bf16_weight(a_w_out, j))
        else:
            proj2 = _proj_layer(x2, norm_pre[i], _proj_weight(b_w_in, j), cos2, sin2, seq_len)
            outs, stats = [], []
            for g, (_, dilation) in enumerate(DILATED_GROUPS):
                o, st = _attn_group(proj2, g, dilation, batch, seq_len)
                outs.append(o)
                stats.append(st)
            x2 = _combine_layer(x2, outs, stats, proj2, _bf16_weight(b_w_out, j), norm_post[i])
    return x2.reshape(batch, seq_len, d)
```
